```python
import jax, jax.numpy as jnp
from jax import lax
import numpy as np

D_MODEL = 2048
BATCH = 8
SEQ = 4096
DEPTH = 4

N_MIXERS = 3
N_HEADS = 16
HEAD_DIM = D_MODEL // N_HEADS
Q_BLOCK = 128
POOL_WINDOWS = (2, 4, 8, 16)
N_POOL_GROUPS = len(POOL_WINDOWS)
POOL_GROUP = D_MODEL // N_POOL_GROUPS
CONV_W = 3
D_FF = ((8 * D_MODEL // 3 + 255) // 256) * 256
EPS = 1e-6
N_SB = (DEPTH + 2) // 3
N_POOL = (DEPTH + 1) // 3
N_CONV = DEPTH // 3

kernel_name = "hybrid_stickbreak_pool_shortconv_trunk"


def rmsnorm(x, g):
    xf = x.astype(jnp.float32)
    y = xf * lax.rsqrt(jnp.mean(xf * xf, axis=-1, keepdims=True) + EPS)
    return (y * g.astype(jnp.float32)).astype(x.dtype)


def head_rmsnorm(x, g):
    xf = x.astype(jnp.float32)
    return xf * lax.rsqrt(jnp.mean(xf * xf, axis=-1, keepdims=True) + EPS) * g.astype(jnp.float32)


def stick_breaking_attention(h, w_qkv, g_q, g_k, w_o):
    B, S, D = h.shape
    qkv = h @ w_qkv
    q, k, v = jnp.split(qkv, 3, axis=-1)
    q = head_rmsnorm(q.reshape(B, S, N_HEADS, HEAD_DIM), g_q)
    k = head_rmsnorm(k.reshape(B, S, N_HEADS, HEAD_DIM), g_k)
    v = v.reshape(B, S, N_HEADS, HEAD_DIM).astype(jnp.float32)
    scale = HEAD_DIM ** -0.5
    outs = []
    for blk in range(S // Q_BLOCK):
        t0 = blk * Q_BLOCK
        kl = t0 + Q_BLOCK
        z = jnp.einsum('bthd,bshd->bhts', q[:, t0:kl], k[:, :kl]) * scale
        t_idx = t0 + jnp.arange(Q_BLOCK)[:, None]
        s_idx = jnp.arange(kl)[None, :]
        mask = s_idx < t_idx
        log_keep = jnp.where(mask, jax.nn.log_sigmoid(-z), 0.0)
        after = lax.cumsum(log_keep, axis=3, reverse=True) - log_keep
        log_a = jax.nn.log_sigmoid(z) + after
        a = jnp.where(mask, jnp.exp(log_a), 0.0)
        o = jnp.einsum('bhts,bshd->bthd', a, v[:, :kl])
        outs.append(o.astype(h.dtype))
    o = jnp.concatenate(outs, axis=1).reshape(B, S, D)
    return o @ w_o


def multiscale_pool(h, w_pool, scale):
    B, S, D = h.shape
    hg = h.astype(jnp.float32).reshape(B, S, N_POOL_GROUPS, POOL_GROUP)
    c = jnp.cumsum(hg, axis=1)
    pos1 = jnp.arange(1, S + 1)
    pieces = []
    for g, w in enumerate(POOL_WINDOWS):
        cg = c[:, :, g]
        lag = jnp.pad(cg, ((0, 0), (w, 0), (0, 0)))[:, :S]
        cnt = jnp.minimum(pos1, w).astype(jnp.float32)[None, :, None]
        pieces.append((cg - lag) / cnt - hg[:, :, g])
    p = jnp.stack(pieces, axis=2).astype(h.dtype)
    y = jnp.einsum('bsgc,gcd->bsgd', p, w_pool).reshape(B, S, D)
    return y * scale


def short_gated_conv(h, w_in, w_conv, w_out):
    D = h.shape[-1]
    bcx = h @ w_in
    b, c, u = jnp.split(bcx, 3, axis=-1)
    g = c * u
    y = lax.conv_general_dilated(
        g, w_conv[:, None, :].astype(g.dtype), window_strides=(1,),
        padding=[(CONV_W - 1, 0)], dimension_numbers=('NWC', 'WIO', 'NWC'),
        feature_group_count=D)
    return (b * y) @ w_out


def swiglu(h, w_gate, w_up, w_down):
    return (jax.nn.silu(h @ w_gate) * (h @ w_up)) @ w_down


def _fwd_setup_inputs(seed: int = 0) -> dict:
    key = jax.random.key(seed)
    ks = jax.random.split(key, 16)
    f32 = jnp.float32
    D, F = D_MODEL, D_FF
    def nrm(k, shape, s):
        return jax.random.normal(k, shape, f32) * s
    return {
        "x": jax.random.normal(ks[0], (BATCH, SEQ, D), f32),
        "norm_mix_g": 1.0 + nrm(ks[1], (DEPTH, D), 0.02),
        "norm_ffn_g": 1.0 + nrm(ks[2], (DEPTH, D), 0.02),
        "sb_w_qkv": nrm(ks[3], (N_SB, D, 3 * D), D ** -0.5),
        "sb_g_q": 1.0 + nrm(ks[4], (N_SB, HEAD_DIM), 0.02),
        "sb_g_k": 1.0 + nrm(ks[5], (N_SB, HEAD_DIM), 0.02),
        "sb_w_o": nrm(ks[6], (N_SB, D, D), D ** -0.5),
        "pool_w": nrm(ks[7], (N_POOL, N_POOL_GROUPS, POOL_GROUP, POOL_GROUP), POOL_GROUP ** -0.5),
        "pool_scale": 1.0 + nrm(ks[8], (N_POOL, D), 0.02),
        "conv_w_in": nrm(ks[9], (N_CONV, D, 3 * D), D ** -0.5),
        "conv_w": nrm(ks[10], (N_CONV, CONV_W, D), CONV_W ** -0.5),
        "conv_w_out": nrm(ks[11], (N_CONV, D, D), D ** -0.5),
        "ffn_w_gate": nrm(ks[12], (DEPTH, D, F), D ** -0.5),
        "ffn_w_up": nrm(ks[13], (DEPTH, D, F), D ** -0.5),
        "ffn_w_down": nrm(ks[14], (DEPTH, F, D), F ** -0.5),
    }


def _fwd_reference(x, norm_mix_g, norm_ffn_g, sb_w_qkv, sb_g_q, sb_g_k, sb_w_o,
              pool_w, pool_scale, conv_w_in, conv_w, conv_w_out,
              ffn_w_gate, ffn_w_up, ffn_w_down):
    for i in range(DEPTH):
        kind, j = i % N_MIXERS, i // N_MIXERS
        h = rmsnorm(x, norm_mix_g[i])
        if kind == 0:
            x = x + stick_breaking_attention(h, sb_w_qkv[j], sb_g_q[j], sb_g_k[j], sb_w_o[j])
        elif kind == 1:
            x = x + multiscale_pool(h, pool_w[j], pool_scale[j])
        else:
            x = x + short_gated_conv(h, conv_w_in[j], conv_w[j], conv_w_out[j])
        h = rmsnorm(x, norm_ffn_g[i])
        x = x + swiglu(h, ffn_w_gate[i], ffn_w_up[i], ffn_w_down[i])
    return x


import jax as _jax
import jax.numpy as _jnp

TWIN_FORMAT = 'train_step'
FWD_PARAMS = ['x', 'norm_mix_g', 'norm_ffn_g', 'sb_w_qkv', 'sb_g_q', 'sb_g_k', 'sb_w_o', 'pool_w', 'pool_scale', 'conv_w_in', 'conv_w', 'conv_w_out', 'ffn_w_gate', 'ffn_w_up', 'ffn_w_down']
TWIN_WEIGHTS = ['norm_mix_g', 'norm_ffn_g', 'sb_w_qkv', 'sb_g_q', 'sb_g_k', 'sb_w_o', 'pool_w', 'pool_scale', 'conv_w_in', 'conv_w', 'conv_w_out', 'ffn_w_gate', 'ffn_w_up', 'ffn_w_down']
TWIN_DIFF_INPUT = 'x'
TWIN_INPUTS = ['x', 'norm_mix_g', 'norm_ffn_g', 'sb_w_qkv', 'sb_g_q', 'sb_g_k', 'sb_w_o', 'pool_w', 'pool_scale', 'conv_w_in', 'conv_w', 'conv_w_out', 'ffn_w_gate', 'ffn_w_up', 'ffn_w_down', 'loss_target', 'm_norm_mix_g', 'm_norm_ffn_g', 'm_sb_w_qkv', 'm_sb_g_q', 'm_sb_g_k', 'm_sb_w_o', 'm_pool_w', 'm_pool_scale', 'm_conv_w_in', 'm_conv_w', 'm_conv_w_out', 'm_ffn_w_gate', 'm_ffn_w_up', 'm_ffn_w_down', 'v_norm_mix_g', 'v_norm_ffn_g', 'v_sb_w_qkv', 'v_sb_g_q', 'v_sb_g_k', 'v_sb_w_o', 'v_pool_w', 'v_pool_scale', 'v_conv_w_in', 'v_conv_w', 'v_conv_w_out', 'v_ffn_w_gate', 'v_ffn_w_up', 'v_ffn_w_down']
TWIN_OUTPUTS = ['loss', 'grad_x', 'grad_norm_mix_g', 'grad_norm_ffn_g', 'grad_sb_w_qkv', 'grad_sb_g_q', 'grad_sb_g_k', 'grad_sb_w_o', 'grad_pool_w', 'grad_pool_scale', 'grad_conv_w_in', 'grad_conv_w', 'grad_conv_w_out', 'grad_ffn_w_gate', 'grad_ffn_w_up', 'grad_ffn_w_down', 'delta_norm_mix_g', 'delta_norm_ffn_g', 'delta_sb_w_qkv', 'delta_sb_g_q', 'delta_sb_g_k', 'delta_sb_w_o', 'delta_pool_w', 'delta_pool_scale', 'delta_conv_w_in', 'delta_conv_w', 'delta_conv_w_out', 'delta_ffn_w_gate', 'delta_ffn_w_up', 'delta_ffn_w_down', 'new_m_norm_mix_g', 'new_m_norm_ffn_g', 'new_m_sb_w_qkv', 'new_m_sb_g_q', 'new_m_sb_g_k', 'new_m_sb_w_o', 'new_m_pool_w', 'new_m_pool_scale', 'new_m_conv_w_in', 'new_m_conv_w', 'new_m_conv_w_out', 'new_m_ffn_w_gate', 'new_m_ffn_w_up', 'new_m_ffn_w_down', 'new_v_norm_mix_g', 'new_v_norm_ffn_g', 'new_v_sb_w_qkv', 'new_v_sb_g_q', 'new_v_sb_g_k', 'new_v_sb_w_o', 'new_v_pool_w', 'new_v_pool_scale', 'new_v_conv_w_in', 'new_v_conv_w', 'new_v_conv_w_out', 'new_v_ffn_w_gate', 'new_v_ffn_w_up', 'new_v_ffn_w_down']
TWIN_LEAF_KINDS = {'loss': 'loss', 'grad_x': 'grad_x', 'grad_norm_mix_g': 'grad_w', 'grad_norm_ffn_g': 'grad_w', 'grad_sb_w_qkv': 'grad_w', 'grad_sb_g_q': 'grad_w', 'grad_sb_g_k': 'grad_w', 'grad_sb_w_o': 'grad_w', 'grad_pool_w': 'grad_w', 'grad_pool_scale': 'grad_w', 'grad_conv_w_in': 'grad_w', 'grad_conv_w': 'grad_w', 'grad_conv_w_out': 'grad_w', 'grad_ffn_w_gate': 'grad_w', 'grad_ffn_w_up': 'grad_w', 'grad_ffn_w_down': 'grad_w', 'delta_norm_mix_g': 'delta_w', 'delta_norm_ffn_g': 'delta_w', 'delta_sb_w_qkv': 'delta_w', 'delta_sb_g_q': 'delta_w', 'delta_sb_g_k': 'delta_w', 'delta_sb_w_o': 'delta_w', 'delta_pool_w': 'delta_w', 'delta_pool_scale': 'delta_w', 'delta_conv_w_in': 'delta_w', 'delta_conv_w': 'delta_w', 'delta_conv_w_out': 'delta_w', 'delta_ffn_w_gate': 'delta_w', 'delta_ffn_w_up': 'delta_w', 'delta_ffn_w_down': 'delta_w', 'new_m_norm_mix_g': 'new_m', 'new_m_norm_ffn_g': 'new_m', 'new_m_sb_w_qkv': 'new_m', 'new_m_sb_g_q': 'new_m', 'new_m_sb_g_k': 'new_m', 'new_m_sb_w_o': 'new_m', 'new_m_pool_w': 'new_m', 'new_m_pool_scale': 'new_m', 'new_m_conv_w_in': 'new_m', 'new_m_conv_w': 'new_m', 'new_m_conv_w_out': 'new_m', 'new_m_ffn_w_gate': 'new_m', 'new_m_ffn_w_up': 'new_m', 'new_m_ffn_w_down': 'new_m', 'new_v_norm_mix_g': 'new_v', 'new_v_norm_ffn_g': 'new_v', 'new_v_sb_w_qkv': 'new_v', 'new_v_sb_g_q': 'new_v', 'new_v_sb_g_k': 'new_v', 'new_v_sb_w_o': 'new_v', 'new_v_pool_w': 'new_v', 'new_v_pool_scale': 'new_v', 'new_v_conv_w_in': 'new_v', 'new_v_conv_w': 'new_v', 'new_v_conv_w_out': 'new_v', 'new_v_ffn_w_gate': 'new_v', 'new_v_ffn_w_up': 'new_v', 'new_v_ffn_w_down': 'new_v'}


def _forward(args):
    return _fwd_reference(*[args[k] for k in FWD_PARAMS])


def _output_shape():
    def fwd():
        inp = _fwd_setup_inputs(0)
        return _fwd_reference(*[inp[k] for k in FWD_PARAMS])
    out = _jax.eval_shape(fwd)
    return out.shape, out.dtype

N_MICROBATCH = 1
ADAM_LR = 0.001
ADAM_B1 = 0.9
ADAM_B2 = 0.999
ADAM_EPS = 1e-08
ADAM_WD = 0.01
ADAM_STEP = 10
PER_EXAMPLE_BATCH_AXIS = {'x': 0, 'loss_target': 0}
SHARED_INPUTS = []
_WEIGHT_DTYPES = {'norm_mix_g': _jnp.float32, 'norm_ffn_g': _jnp.float32, 'sb_w_qkv': _jnp.float32, 'sb_g_q': _jnp.float32, 'sb_g_k': _jnp.float32, 'sb_w_o': _jnp.float32, 'pool_w': _jnp.float32, 'pool_scale': _jnp.float32, 'conv_w_in': _jnp.float32, 'conv_w': _jnp.float32, 'conv_w_out': _jnp.float32, 'ffn_w_gate': _jnp.float32, 'ffn_w_up': _jnp.float32, 'ffn_w_down': _jnp.float32}
MOMENT_SCALE = {'norm_mix_g': 2.516257e+01, 'norm_ffn_g': 1.234190e+01, 'sb_w_qkv': 3.146482e-01, 'sb_g_q': 1.592120e+01, 'sb_g_k': 1.589160e+01, 'sb_w_o': 4.618592e-01, 'pool_w': 8.959234e-01, 'pool_scale': 1.220549e+01, 'conv_w_in': 5.726778e-01, 'conv_w': 9.183889e+00, 'conv_w_out': 4.431302e-01, 'ffn_w_gate': 1.833053e-01, 'ffn_w_up': 1.904247e-01, 'ffn_w_down': 3.133104e-01}


def _to_microbatches(a, axis):
    t = _jnp.moveaxis(a, axis, 0)
    t = t.reshape((N_MICROBATCH, t.shape[0] // N_MICROBATCH) + t.shape[1:])
    return _jnp.moveaxis(t, 1, axis + 1)


def setup_inputs(seed: int = 0) -> dict:
    inp = _fwd_setup_inputs(seed)
    key = _jax.random.fold_in(_jax.random.key(seed), 7919)
    shape, _ = _output_shape()
    out = dict(inp)
    out["loss_target"] = _jax.random.normal(_jax.random.fold_in(key, 0), shape, _jnp.float32)
    for i, name in enumerate(TWIN_WEIGHTS):
        w = inp[name].astype(_jnp.float32)
        if MOMENT_SCALE is None:
            s = _jnp.sqrt(_jnp.mean(_jnp.square(w)) + 1e-30)
        else:
            s = MOMENT_SCALE[name]
        km, kv = _jax.random.split(_jax.random.fold_in(key, i + 1))
        out[name] = w
        out["m_" + name] = s * _jax.random.normal(km, w.shape, _jnp.float32)
        out["v_" + name] = (s * s) * _jax.random.uniform(kv, w.shape, _jnp.float32, 0.5, 1.5)
    if N_MICROBATCH > 1:
        for name, axis in PER_EXAMPLE_BATCH_AXIS.items():
            out[name] = _to_microbatches(out[name], axis)
    return {'x': out['x'], 'norm_mix_g': out['norm_mix_g'], 'norm_ffn_g': out['norm_ffn_g'], 'sb_w_qkv': out['sb_w_qkv'], 'sb_g_q': out['sb_g_q'], 'sb_g_k': out['sb_g_k'], 'sb_w_o': out['sb_w_o'], 'pool_w': out['pool_w'], 'pool_scale': out['pool_scale'], 'conv_w_in': out['conv_w_in'], 'conv_w': out['conv_w'], 'conv_w_out': out['conv_w_out'], 'ffn_w_gate': out['ffn_w_gate'], 'ffn_w_up': out['ffn_w_up'], 'ffn_w_down': out['ffn_w_down'], 'loss_target': out['loss_target'], 'm_norm_mix_g': out['m_norm_mix_g'], 'm_norm_ffn_g': out['m_norm_ffn_g'], 'm_sb_w_qkv': out['m_sb_w_qkv'], 'm_sb_g_q': out['m_sb_g_q'], 'm_sb_g_k': out['m_sb_g_k'], 'm_sb_w_o': out['m_sb_w_o'], 'm_pool_w': out['m_pool_w'], 'm_pool_scale': out['m_pool_scale'], 'm_conv_w_in': out['m_conv_w_in'], 'm_conv_w': out['m_conv_w'], 'm_conv_w_out': out['m_conv_w_out'], 'm_ffn_w_gate': out['m_ffn_w_gate'], 'm_ffn_w_up': out['m_ffn_w_up'], 'm_ffn_w_down': out['m_ffn_w_down'], 'v_norm_mix_g': out['v_norm_mix_g'], 'v_norm_ffn_g': out['v_norm_ffn_g'], 'v_sb_w_qkv': out['v_sb_w_qkv'], 'v_sb_g_q': out['v_sb_g_q'], 'v_sb_g_k': out['v_sb_g_k'], 'v_sb_w_o': out['v_sb_w_o'], 'v_pool_w': out['v_pool_w'], 'v_pool_scale': out['v_pool_scale'], 'v_conv_w_in': out['v_conv_w_in'], 'v_conv_w': out['v_conv_w'], 'v_conv_w_out': out['v_conv_w_out'], 'v_ffn_w_gate': out['v_ffn_w_gate'], 'v_ffn_w_up': out['v_ffn_w_up'], 'v_ffn_w_down': out['v_ffn_w_down']}


def _loss(weights, diff, rest, loss_target):
    with _jax.named_scope("forward"):
        args = {**rest, TWIN_DIFF_INPUT: diff, **{k: w.astype(_WEIGHT_DTYPES[k]) for k, w in weights.items()}}
        y = _forward(args)
    with _jax.named_scope("loss_head"):
        err = _jnp.square(y.astype(_jnp.float32) - loss_target)
        return 0.5 * _jnp.sum(_jnp.mean(err, axis=-1)) if err.ndim else 0.5 * err


def _adamw(w, g, m, v):
    m = ADAM_B1 * m + (1.0 - ADAM_B1) * g
    v = ADAM_B2 * v + (1.0 - ADAM_B2) * _jnp.square(g)
    m_hat = m / (1.0 - ADAM_B1 ** ADAM_STEP)
    v_hat = v / (1.0 - ADAM_B2 ** ADAM_STEP)
    delta = -ADAM_LR * (m_hat / (_jnp.sqrt(v_hat) + ADAM_EPS) + ADAM_WD * w)
    return delta, m, v


def reference(x, norm_mix_g, norm_ffn_g, sb_w_qkv, sb_g_q, sb_g_k, sb_w_o, pool_w, pool_scale, conv_w_in, conv_w, conv_w_out, ffn_w_gate, ffn_w_up, ffn_w_down, loss_target, m_norm_mix_g, m_norm_ffn_g, m_sb_w_qkv, m_sb_g_q, m_sb_g_k, m_sb_w_o, m_pool_w, m_pool_scale, m_conv_w_in, m_conv_w, m_conv_w_out, m_ffn_w_gate, m_ffn_w_up, m_ffn_w_down, v_norm_mix_g, v_norm_ffn_g, v_sb_w_qkv, v_sb_g_q, v_sb_g_k, v_sb_w_o, v_pool_w, v_pool_scale, v_conv_w_in, v_conv_w, v_conv_w_out, v_ffn_w_gate, v_ffn_w_up, v_ffn_w_down):
    given = dict(x=x, norm_mix_g=norm_mix_g, norm_ffn_g=norm_ffn_g, sb_w_qkv=sb_w_qkv, sb_g_q=sb_g_q, sb_g_k=sb_g_k, sb_w_o=sb_w_o, pool_w=pool_w, pool_scale=pool_scale, conv_w_in=conv_w_in, conv_w=conv_w, conv_w_out=conv_w_out, ffn_w_gate=ffn_w_gate, ffn_w_up=ffn_w_up, ffn_w_down=ffn_w_down, loss_target=loss_target, m_norm_mix_g=m_norm_mix_g, m_norm_ffn_g=m_norm_ffn_g, m_sb_w_qkv=m_sb_w_qkv, m_sb_g_q=m_sb_g_q, m_sb_g_k=m_sb_g_k, m_sb_w_o=m_sb_w_o, m_pool_w=m_pool_w, m_pool_scale=m_pool_scale, m_conv_w_in=m_conv_w_in, m_conv_w=m_conv_w, m_conv_w_out=m_conv_w_out, m_ffn_w_gate=m_ffn_w_gate, m_ffn_w_up=m_ffn_w_up, m_ffn_w_down=m_ffn_w_down, v_norm_mix_g=v_norm_mix_g, v_norm_ffn_g=v_norm_ffn_g, v_sb_w_qkv=v_sb_w_qkv, v_sb_g_q=v_sb_g_q, v_sb_g_k=v_sb_g_k, v_sb_w_o=v_sb_w_o, v_pool_w=v_pool_w, v_pool_scale=v_pool_scale, v_conv_w_in=v_conv_w_in, v_conv_w=v_conv_w, v_conv_w_out=v_conv_w_out, v_ffn_w_gate=v_ffn_w_gate, v_ffn_w_up=v_ffn_w_up, v_ffn_w_down=v_ffn_w_down)
    weights = {n: given[n] for n in TWIN_WEIGHTS}
    shared = {n: given[n] for n in SHARED_INPUTS}
    per_example = {n: given[n] for n in ['x']}
    grad_fn = _jax.value_and_grad(_loss, argnums=(0, 1))

    def one_microbatch(ex, loss_target):
        ex = dict(ex)
        diff = ex.pop(TWIN_DIFF_INPUT)
        return grad_fn(weights, diff, {**shared, **ex}, loss_target)

    if N_MICROBATCH == 1:
        loss, (grad_w, grad_x) = one_microbatch(per_example, given["loss_target"])
    else:
        def body(carry, xs):
            loss_sum, grad_sum = carry
            l_k, (gw_k, gx_k) = one_microbatch(xs[0], xs[1])
            with _jax.named_scope("update"):
                return (loss_sum + l_k, _jax.tree.map(_jnp.add, grad_sum, gw_k)), gx_k

        init = (_jnp.zeros((), _jnp.float32), _jax.tree.map(_jnp.zeros_like, weights))
        (loss, grad_w), grad_x = _jax.lax.scan(body, init, (per_example, given["loss_target"]))
    with _jax.named_scope("update"):
        delta_w, new_m, new_v = {}, {}, {}
        for n in TWIN_WEIGHTS:
            delta_w[n], new_m[n], new_v[n] = _adamw(weights[n], grad_w[n], given["m_" + n], given["v_" + n])
    return (loss, grad_x, *[grad_w[n] for n in TWIN_WEIGHTS], *[delta_w[n] for n in TWIN_WEIGHTS],
            *[new_m[n] for n in TWIN_WEIGHTS], *[new_v[n] for n in TWIN_WEIGHTS])
```

```python
import functools

import jax
import jax.numpy as jnp
from jax import lax
from jax.experimental import pallas as pl
from jax.experimental.pallas import tpu as pltpu

F32 = jnp.float32
BF16 = jnp.bfloat16

HEAD_DIM = 128
N_POOL_GROUPS = 4
EPS = 1e-6
N_SHARD = 4
N_DEV = 8
VMEM_LIMIT_BYTES = 56 * 2**20
KEY_BLOCK = 128
SMALL_ROWS = 16

ADAM_LR = 0.001
ADAM_B1 = 0.9
ADAM_B2 = 0.999
ADAM_EPS = 1e-08
ADAM_WD = 0.01
ADAM_STEP = 10

MESH = pl.DeviceIdType.MESH
ANY = pl.BlockSpec(memory_space=pl.ANY)


def _params(*sem):
    return pltpu.CompilerParams(dimension_semantics=sem, vmem_limit_bytes=VMEM_LIMIT_BYTES)


def _pick(n, pref, unit):
    t = (min(pref, n) // unit) * unit
    while n % t:
        t -= unit
    return t


NN = ((1,), (0,))
NT = ((1,), (1,))
TN = ((0,), (0,))


def _mm(name, grid, a_ops, b_ops, dots, dims, acc_shapes, outs, epi=None, extras=()):
    na, nb, ne, no, nacc = len(a_ops), len(b_ops), len(extras), len(outs), len(acc_shapes)
    nk = grid[2]

    def body(*refs):
        a_refs, b_refs = refs[:na], refs[na:na + nb]
        e_refs = refs[na + nb:na + nb + ne]
        o_refs = refs[na + nb + ne:na + nb + ne + no]
        acc_refs = refs[na + nb + ne + no:]

        def partial_sums():
            sums = [None] * nacc
            for ai, bi, ci in dots:
                d = lax.dot_general(a_refs[ai][...].astype(BF16), b_refs[bi][...].astype(BF16),
                                    (dims, ((), ())), preferred_element_type=F32)
                sums[ci] = d if sums[ci] is None else sums[ci] + d
            return sums

        def finish(accs):
            res = epi(accs, [e[...] for e in e_refs]) if epi is not None else accs
            for o, r in zip(o_refs, res):
                o[...] = r.astype(o.dtype)

        if nk == 1:
            finish(partial_sums())
            return
        k = pl.program_id(2)

        @pl.when(k == 0)
        def _():
            for acc, s in zip(acc_refs, partial_sums()):
                acc[...] = s

        @pl.when(k > 0)
        def _():
            for acc, s in zip(acc_refs, partial_sums()):
                acc[...] += s

        @pl.when(k == nk - 1)
        def _():
            finish([acc[...] for acc in acc_refs])

    ops = list(a_ops) + list(b_ops) + list(extras)
    return pl.pallas_call(
        body, name=name, grid=grid,
        in_specs=[s for _, s in ops],
        out_specs=[s for _, _, s in outs],
        out_shape=[jax.ShapeDtypeStruct(sh, dt) for sh, dt, _ in outs],
        scratch_shapes=[pltpu.VMEM(s, F32) for s in acc_shapes] if nk > 1 else [],
        compiler_params=_params("parallel", "parallel", "arbitrary"),
    )(*[a for a, _ in ops])


def _spec(block, index):
    return pl.BlockSpec(block, index)


def _rmsnorm_fwd(x, g, out_dtype):
    T, D = x.shape
    tm = _pick(T, 256, 8)

    def body(x_ref, g_ref, o_ref):
        xv = x_ref[...]
        r = lax.rsqrt(jnp.mean(xv * xv, axis=-1, keepdims=True) + EPS)
        o_ref[...] = (xv * r * g_ref[...]).astype(o_ref.dtype)

    row = _spec((tm, D), lambda i: (i, 0))
    return pl.pallas_call(
        body, name="rmsnorm_fwd", grid=(T // tm,),
        in_specs=[row, _spec((1, D), lambda i: (0, 0))], out_specs=row,
        out_shape=jax.ShapeDtypeStruct((T, D), out_dtype), compiler_params=_params("parallel"),
    )(x, g)


def _rmsnorm_bwd(x, g, dh, dres):
    T, D = x.shape
    tm = _pick(T, 256, 8)

    def body(x_ref, g_ref, dh_ref, dres_ref, dx_ref, dxb_ref, dg_ref):
        xv = x_ref[...]
        dhv = dh_ref[...].astype(F32)
        r = lax.rsqrt(jnp.mean(xv * xv, axis=-1, keepdims=True) + EPS)
        xh = xv * r
        dxh = dhv * g_ref[...]
        dx = r * (dxh - xh * jnp.mean(dxh * xh, axis=-1, keepdims=True)) + dres_ref[...]
        dx_ref[...] = dx
        dxb_ref[...] = dx.astype(BF16)
        part = jnp.sum(dhv * xh, axis=0, keepdims=True)

        @pl.when(pl.program_id(0) == 0)
        def _():
            dg_ref[...] = part

        @pl.when(pl.program_id(0) > 0)
        def _():
            dg_ref[...] += part

    row = _spec((tm, D), lambda i: (i, 0))
    vec = _spec((1, D), lambda i: (0, 0))
    return pl.pallas_call(
        body, name="rmsnorm_bwd", grid=(T // tm,),
        in_specs=[row, vec, row, row], out_specs=[row, row, vec],
        out_shape=[jax.ShapeDtypeStruct((T, D), F32), jax.ShapeDtypeStruct((T, D), BF16),
                   jax.ShapeDtypeStruct((1, D), F32)],
        compiler_params=_params("arbitrary"),
    )(x, g, dh, dres)


def _loss_head(y, target):
    T, D = y.shape
    tm = _pick(T, 256, 8)

    def body(y_ref, t_ref, dy_ref, dyb_ref, l_ref):
        err = y_ref[...] - t_ref[...]
        dy = err * (1.0 / D)
        dy_ref[...] = dy
        dyb_ref[...] = dy.astype(BF16)
        part = jnp.sum(err * err, axis=0, keepdims=True)

        @pl.when(pl.program_id(0) == 0)
        def _():
            l_ref[...] = part

        @pl.when(pl.program_id(0) > 0)
        def _():
            l_ref[...] += part

    row = _spec((tm, D), lambda i: (i, 0))
    vec = _spec((1, D), lambda i: (0, 0))
    return pl.pallas_call(
        body, name="loss_head", grid=(T // tm,),
        in_specs=[row, row], out_specs=[row, row, vec],
        out_shape=[jax.ShapeDtypeStruct((T, D), F32), jax.ShapeDtypeStruct((T, D), BF16),
                   jax.ShapeDtypeStruct((1, D), F32)],
        compiler_params=_params("arbitrary"),
    )(y, target)


def _headnorm_fwd(qkv, gq, gk):
    _, T, D = qkv.shape
    H = D // HEAD_DIM
    tm = _pick(T, 256, 8)
    scale = HEAD_DIM ** -0.5

    def body(q_ref, k_ref, v_ref, gq_ref, gk_ref, qs_ref, kn_ref, vb_ref):
        for h in range(H):
            sl = slice(h * HEAD_DIM, (h + 1) * HEAD_DIM)
            q = q_ref[:, sl]
            qs_ref[:, sl] = (q * lax.rsqrt(jnp.mean(q * q, axis=-1, keepdims=True) + EPS)
                             * (gq_ref[...] * scale)).astype(BF16)
            k = k_ref[:, sl]
            kn_ref[:, sl] = (k * lax.rsqrt(jnp.mean(k * k, axis=-1, keepdims=True) + EPS)
                             * gk_ref[...]).astype(BF16)
        vb_ref[...] = v_ref[...].astype(BF16)

    part = lambda p: _spec((None, tm, D), lambda i, p=p: (p, i, 0))
    row = _spec((tm, D), lambda i: (i, 0))
    vec = _spec((1, HEAD_DIM), lambda i: (0, 0))
    return pl.pallas_call(
        body, name="headnorm_fwd", grid=(T // tm,),
        in_specs=[part(0), part(1), part(2), vec, vec], out_specs=[row, row, row],
        out_shape=[jax.ShapeDtypeStruct((T, D), BF16)] * 3, compiler_params=_params("parallel"),
    )(qkv, qkv, qkv, gq, gk)


def _headnorm_bwd(qkv, gq, gk, dqs, dkn, dv):
    _, T, D = qkv.shape
    H = D // HEAD_DIM
    tm = _pick(T, 256, 8)
    scale = HEAD_DIM ** -0.5

    def body(q_ref, k_ref, gq_ref, gk_ref, dqs_ref, dkn_ref, dv_ref, dqkv_ref, dgq_ref, dgk_ref):
        dgq = jnp.zeros((1, HEAD_DIM), F32)
        dgk = jnp.zeros((1, HEAD_DIM), F32)
        for h in range(H):
            sl = slice(h * HEAD_DIM, (h + 1) * HEAD_DIM)
            for src, dsrc, g_ref, sc, p in ((q_ref, dqs_ref, gq_ref, scale, 0), (k_ref, dkn_ref, gk_ref, 1.0, 1)):
                v = src[:, sl]
                r = lax.rsqrt(jnp.mean(v * v, axis=-1, keepdims=True) + EPS)
                vh = v * r
                dn = dsrc[:, sl] * sc
                dvh = dn * g_ref[...]
                dqkv_ref[p, :, sl] = (r * (dvh - vh * jnp.mean(dvh * vh, axis=-1, keepdims=True))).astype(BF16)
                dg = jnp.sum(dn * vh, axis=0, keepdims=True)
                if p == 0:
                    dgq = dgq + dg
                else:
                    dgk = dgk + dg
        dqkv_ref[2] = dv_ref[...].astype(BF16)

        @pl.when(pl.program_id(0) == 0)
        def _():
            dgq_ref[...] = dgq
            dgk_ref[...] = dgk

        @pl.when(pl.program_id(0) > 0)
        def _():
            dgq_ref[...] += dgq
            dgk_ref[...] += dgk

    part = lambda p: _spec((None, tm, D), lambda i, p=p: (p, i, 0))
    row = _spec((tm, D), lambda i: (i, 0))
    vec = _spec((1, HEAD_DIM), lambda i: (0, 0))
    return pl.pallas_call(
        body, name="headnorm_bwd", grid=(T // tm,),
        in_specs=[part(0), part(1), vec, vec, row, row, row],
        out_specs=[_spec((3, tm, D), lambda i: (0, i, 0)), vec, vec],
        out_shape=[jax.ShapeDtypeStruct((3, T, D), BF16), jax.ShapeDtypeStruct((1, HEAD_DIM), F32),
                   jax.ShapeDtypeStruct((1, HEAD_DIM), F32)],
        compiler_params=_params("arbitrary"),
    )(qkv, qkv, gq, gk, dqs, dkn, dv)


def _sum_matrix(prefix):
    r = lax.broadcasted_iota(jnp.int32, (KEY_BLOCK, 2 * KEY_BLOCK), 0)
    c = lax.broadcasted_iota(jnp.int32, (KEY_BLOCK, 2 * KEY_BLOCK), 1)
    tri = (r <= c) if prefix else (r > c)
    return jnp.where(tri | (c >= KEY_BLOCK), 1.0, 0.0).astype(BF16)


def _block_sums(v, u):
    hi = v.astype(BF16)
    lo = (v - hi.astype(F32)).astype(BF16)
    s = (jnp.dot(hi, u, preferred_element_type=F32) + jnp.dot(lo, u, preferred_element_type=F32))
    return s[:, :KEY_BLOCK], s[:, KEY_BLOCK:]


def _log_terms(q, kj, mask):
    z = lax.dot_general(q, kj, (NT, ((), ())), preferred_element_type=F32)
    ls = jnp.minimum(z, 0.0) - jnp.log(1.0 + jnp.exp(-jnp.abs(z)))
    lk = ls - z
    if mask is not None:
        lk = jnp.where(mask, lk, 0.0)
    return ls, lk


def _causal_mask(row0, key0, tq):
    t = row0 + lax.broadcasted_iota(jnp.int32, (tq, KEY_BLOCK), 0)
    s = key0 + lax.broadcasted_iota(jnp.int32, (tq, KEY_BLOCK), 1)
    return s < t


def _attn_fwd(qs, kn, vb):
    T, D = qs.shape
    H = D // HEAD_DIM
    tq = _pick(T, 256, KEY_BLOCK)
    nd = tq // KEY_BLOCK

    def body(q_ref, k_ref, v_ref, u_ref, o_ref, tot_ref, acc_ref, run_ref):
        i = pl.program_id(1)
        q = q_ref[...]
        uv = u_ref[...]
        acc_ref[...] = jnp.zeros_like(acc_ref)
        run_ref[...] = jnp.zeros_like(run_ref)

        def step(jb, masked):
            k0 = pl.multiple_of(jb * KEY_BLOCK, KEY_BLOCK)
            mask = _causal_mask(i * tq, k0, tq) if masked else None
            ls, lk = _log_terms(q, k_ref[pl.ds(k0, KEY_BLOCK), :], mask)
            after, rows = _block_sums(lk, uv)
            a = jnp.exp(ls + after + run_ref[...])
            if masked:
                a = jnp.where(mask, a, 0.0)
            acc_ref[...] += jnp.dot(a.astype(BF16), v_ref[pl.ds(k0, KEY_BLOCK), :], preferred_element_type=F32)
            run_ref[...] += rows

        for d in reversed(range(nd)):
            step(i * nd + d, True)

        def below(n, c):
            step(i * nd - 1 - n, False)
            return c

        lax.fori_loop(0, i * nd, below, 0)
        o_ref[...] = acc_ref[...].astype(o_ref.dtype)
        tot_ref[...] = run_ref[...]

    blk = _spec((tq, HEAD_DIM), lambda h, i: (i, h))
    col = _spec((T, HEAD_DIM), lambda h, i: (0, h))
    return pl.pallas_call(
        body, name="attn_fwd", grid=(H, T // tq),
        in_specs=[blk, col, col, _spec((KEY_BLOCK, 2 * KEY_BLOCK), lambda h, i: (0, 0))], out_specs=[blk, blk],
        out_shape=[jax.ShapeDtypeStruct((T, D), BF16), jax.ShapeDtypeStruct((T, D), F32)],
        scratch_shapes=[pltpu.VMEM((tq, HEAD_DIM), F32), pltpu.VMEM((tq, KEY_BLOCK), F32)],
        compiler_params=_params("parallel", "arbitrary"),
    )(qs, kn, vb, _sum_matrix(False))


def _attn_bwd(qs, kn, vb, tot, do):
    T, D = qs.shape
    H = D // HEAD_DIM
    tq = _pick(T, 256, KEY_BLOCK)
    nd = tq // KEY_BLOCK

    def body(q_ref, k_ref, v_ref, tot_ref, do_ref, u_ref, dq_ref, dk_ref, dv_ref, dqa_ref, run_ref, grun_ref):
        i = pl.program_id(1)

        @pl.when(i == 0)
        def _():
            dk_ref[...] = jnp.zeros_like(dk_ref)
            dv_ref[...] = jnp.zeros_like(dv_ref)

        q = q_ref[...]
        dov = do_ref[...]
        uv = u_ref[...]
        dqa_ref[...] = jnp.zeros_like(dqa_ref)
        run_ref[...] = jnp.zeros_like(run_ref)
        grun_ref[...] = jnp.zeros_like(grun_ref)

        def step(jb, masked):
            k0 = pl.multiple_of(jb * KEY_BLOCK, KEY_BLOCK)
            keys = pl.ds(k0, KEY_BLOCK)
            mask = _causal_mask(i * tq, k0, tq) if masked else None
            kj = k_ref[keys, :]
            ls, lk = _log_terms(q, kj, mask)
            upto, rows = _block_sums(lk, uv)
            a = jnp.exp(ls + (tot_ref[...] - run_ref[...] - upto))
            if masked:
                a = jnp.where(mask, a, 0.0)
            da = lax.dot_general(dov, v_ref[keys, :], (NT, ((), ())), preferred_element_type=F32)
            g = a * da
            gupto, grows = _block_sums(g, uv)
            dz = g - jnp.exp(ls) * (grun_ref[...] + gupto)
            if masked:
                dz = jnp.where(mask, dz, 0.0)
            dzb = dz.astype(BF16)
            dqa_ref[...] += jnp.dot(dzb, kj, preferred_element_type=F32)
            dk_ref[keys, :] += lax.dot_general(dzb, q, (TN, ((), ())), preferred_element_type=F32)
            dv_ref[keys, :] += lax.dot_general(a.astype(BF16), dov, (TN, ((), ())), preferred_element_type=F32)
            run_ref[...] += rows
            grun_ref[...] += grows

        def below(n, c):
            step(n, False)
            return c

        lax.fori_loop(0, i * nd, below, 0)
        for d in range(nd):
            step(i * nd + d, True)
        dq_ref[...] = dqa_ref[...]

    blk = _spec((tq, HEAD_DIM), lambda h, i: (i, h))
    col = _spec((T, HEAD_DIM), lambda h, i: (0, h))
    return pl.pallas_call(
        body, name="attn_bwd", grid=(H, T // tq),
        in_specs=[blk, col, col, blk, blk, _spec((KEY_BLOCK, 2 * KEY_BLOCK), lambda h, i: (0, 0))],
        out_specs=[blk, col, col],
        out_shape=[jax.ShapeDtypeStruct((T, D), F32)] * 3,
        scratch_shapes=[pltpu.VMEM((tq, HEAD_DIM), F32), pltpu.VMEM((tq, KEY_BLOCK), F32),
                        pltpu.VMEM((tq, KEY_BLOCK), F32)],
        compiler_params=_params("parallel", "arbitrary"),
    )(qs, kn, vb, tot, do, _sum_matrix(True))


def _shift_down(v, n):
    t = lax.broadcasted_iota(jnp.int32, v.shape, 0)
    return jnp.where(t >= n, pltpu.roll(v, n, 0), 0.0)


def _shift_up(v, n):
    rows = v.shape[0]
    t = lax.broadcasted_iota(jnp.int32, v.shape, 0)
    return jnp.where(t < rows - n, pltpu.roll(v, rows - n, 0), 0.0)


def _pool_window(j, cw, D):
    group = (j * cw) // (D // N_POOL_GROUPS)
    return jnp.left_shift(2, group)


def _pool_count(shape, w):
    t = lax.broadcasted_iota(jnp.int32, shape, 0)
    return jnp.minimum(t + 1, w).astype(F32)


def _pool_fwd(h):
    T, D = h.shape
    cw = min(256, D // N_POOL_GROUPS)

    def body(h_ref, p_ref):
        w = _pool_window(pl.program_id(0), cw, D)
        hv = h_ref[...]
        s = hv
        for n in (1, 2, 4, 8):
            s = jnp.where(n < w, s + _shift_down(s, n), s)
        p_ref[...] = (s / _pool_count(hv.shape, w) - hv).astype(p_ref.dtype)

    slab = _spec((T, cw), lambda j: (0, j))
    return pl.pallas_call(
        body, name="pool_fwd", grid=(D // cw,), in_specs=[slab], out_specs=slab,
        out_shape=jax.ShapeDtypeStruct((T, D), BF16), compiler_params=_params("parallel"),
    )(h)


def _pool_bwd(dp):
    T, D = dp.shape
    cw = min(256, D // N_POOL_GROUPS)

    def body(dp_ref, dh_ref):
        w = _pool_window(pl.program_id(0), cw, D)
        dpv = dp_ref[...]
        s = dpv / _pool_count(dpv.shape, w)
        for n in (1, 2, 4, 8):
            s = jnp.where(n < w, s + _shift_up(s, n), s)
        dh_ref[...] = (s - dpv).astype(dh_ref.dtype)

    slab = _spec((T, cw), lambda j: (0, j))
    return pl.pallas_call(
        body, name="pool_bwd", grid=(D // cw,), in_specs=[slab], out_specs=slab,
        out_shape=jax.ShapeDtypeStruct((T, D), F32), compiler_params=_params("parallel"),
    )(dp)


def _pool_scale_bwd(dx, ypre, scale):
    T, D = dx.shape
    tm = _pick(T, 256, 8)

    def body(dx_ref, y_ref, s_ref, dys_ref, ds_ref):
        dxv = dx_ref[...]
        dys_ref[...] = (dxv * s_ref[...]).astype(BF16)
        part = jnp.sum(dxv * y_ref[...].astype(F32), axis=0, keepdims=True)

        @pl.when(pl.program_id(0) == 0)
        def _():
            ds_ref[...] = part

        @pl.when(pl.program_id(0) > 0)
        def _():
            ds_ref[...] += part

    row = _spec((tm, D), lambda i: (i, 0))
    vec = _spec((1, D), lambda i: (0, 0))
    return pl.pallas_call(
        body, name="pool_scale_bwd", grid=(T // tm,), in_specs=[row, row, vec], out_specs=[row, vec],
        out_shape=[jax.ShapeDtypeStruct((T, D), BF16), jax.ShapeDtypeStruct((1, D), F32)],
        compiler_params=_params("arbitrary"),
    )(dx, ypre, scale)


def _conv_specs(T, D, cw, cs):
    part = lambda p: _spec((None, T, cw), lambda j, p=p: (p, 0, j))
    taps = _spec((None, 8, cw), lambda j: (j // (cs // cw), 0, j % (cs // cw)))
    return part, taps


def _conv_fwd(bcx, taps4):
    _, T, D = bcx.shape
    cs = taps4.shape[2]
    cw = min(128, cs)
    part, taps = _conv_specs(T, D, cw, cs)

    def body(b_ref, c_ref, u_ref, w_ref, q_ref):
        g = c_ref[...].astype(F32) * u_ref[...].astype(F32)
        w = w_ref[...]
        y = w[2:3] * g + w[1:2] * _shift_down(g, 1) + w[0:1] * _shift_down(g, 2)
        q_ref[...] = (b_ref[...].astype(F32) * y).astype(q_ref.dtype)

    return pl.pallas_call(
        body, name="conv_fwd", grid=(D // cw,), in_specs=[part(0), part(1), part(2), taps],
        out_specs=_spec((T, cw), lambda j: (0, j)),
        out_shape=jax.ShapeDtypeStruct((T, D), BF16), compiler_params=_params("parallel"),
    )(bcx, bcx, bcx, taps4)


def _conv_bwd(dq, bcx, taps4):
    _, T, D = bcx.shape
    cs = taps4.shape[2]
    cw = min(128, cs)
    part, taps = _conv_specs(T, D, cw, cs)

    def body(dq_ref, b_ref, c_ref, u_ref, w_ref, d_ref, dw_ref):
        b = b_ref[...].astype(F32)
        c = c_ref[...].astype(F32)
        uu = u_ref[...].astype(F32)
        dqv = dq_ref[...].astype(F32)
        w = w_ref[...]
        g = c * uu
        g1 = _shift_down(g, 1)
        g2 = _shift_down(g, 2)
        d_ref[0] = (dqv * (w[2:3] * g + w[1:2] * g1 + w[0:1] * g2)).astype(BF16)
        dy = dqv * b
        dg = w[2:3] * dy + w[1:2] * _shift_up(dy, 1) + w[0:1] * _shift_up(dy, 2)
        d_ref[1] = (dg * uu).astype(BF16)
        d_ref[2] = (dg * c).astype(BF16)
        dw_ref[0:1, :] = jnp.sum(dy * g2, axis=0, keepdims=True)
        dw_ref[1:2, :] = jnp.sum(dy * g1, axis=0, keepdims=True)
        dw_ref[2:3, :] = jnp.sum(dy * g, axis=0, keepdims=True)
        dw_ref[3:8, :] = jnp.zeros((5, cw), F32)

    return pl.pallas_call(
        body, name="conv_bwd", grid=(D // cw,),
        in_specs=[_spec((T, cw), lambda j: (0, j)), part(0), part(1), part(2), taps],
        out_specs=[_spec((3, T, cw), lambda j: (0, 0, j)), taps],
        out_shape=[jax.ShapeDtypeStruct((3, T, D), BF16), jax.ShapeDtypeStruct((N_SHARD, 8, cs), F32)],
        compiler_params=_params("parallel"),
    )(dq, bcx, bcx, bcx, taps4)


def _cast_bf16(w):
    R, C = w.shape
    tr = _pick(R, 512, 8)

    def body(w_ref, o_ref):
        o_ref[...] = w_ref[...].astype(BF16)

    row = _spec((tr, C), lambda i: (i, 0))
    return pl.pallas_call(
        body, name="cast_bf16", grid=(R // tr,), in_specs=[row], out_specs=row,
        out_shape=jax.ShapeDtypeStruct((R, C), BF16), compiler_params=_params("parallel"),
    )(w)


def _sum_devices(parts):
    n, R, C = parts.shape
    tr = _pick(R, 256, 8)

    def body(p_ref, o_ref):
        s = p_ref[0].astype(F32)
        for d in range(1, n):
            s = s + p_ref[d].astype(F32)
        o_ref[...] = s

    return pl.pallas_call(
        body, name="sum_devices", grid=(R // tr,),
        in_specs=[_spec((n, tr, C), lambda i: (0, i, 0))], out_specs=_spec((tr, C), lambda i: (i, 0)),
        out_shape=jax.ShapeDtypeStruct((R, C), F32), compiler_params=_params("parallel"),
    )(parts)


def _adamw(w, g, m, v):
    R, C = w.shape
    tr = _pick(R, 256, 8)

    def body(w_ref, g_ref, m_ref, v_ref, d_ref, nm_ref, nv_ref):
        gv = g_ref[...]
        m2 = ADAM_B1 * m_ref[...] + (1.0 - ADAM_B1) * gv
        v2 = ADAM_B2 * v_ref[...] + (1.0 - ADAM_B2) * (gv * gv)
        m_hat = m2 / (1.0 - ADAM_B1 ** ADAM_STEP)
        v_hat = v2 / (1.0 - ADAM_B2 ** ADAM_STEP)
        d_ref[...] = -ADAM_LR * (m_hat / (jnp.sqrt(v_hat) + ADAM_EPS) + ADAM_WD * w_ref[...])
        nm_ref[...] = m2
        nv_ref[...] = v2

    row = _spec((tr, C), lambda i: (i, 0))
    return pl.pallas_call(
        body, name="adamw", grid=(R // tr,), in_specs=[row] * 4, out_specs=[row] * 3,
        out_shape=[jax.ShapeDtypeStruct((R, C), F32)] * 3, compiler_params=_params("parallel"),
    )(w, g, m, v)


def _place():
    return lax.axis_index("x"), lax.axis_index("y"), lax.axis_index("c")


def _half(ref_rows, c):
    return pl.ds(c * (ref_rows // 2), ref_rows // 2)


def _allgather_shards(shards):
    n = len(shards)

    def body(*refs):
        ins, outs = refs[:n], refs[n:2 * n]
        send, recv, lsem = refs[2 * n:]
        x, y, c = _place()
        mine = 2 * x + y
        chips = [(1 - x, y), (x, 1 - y), (1 - x, 1 - y)]

        def copy(a, k, quarter, to, src=None):
            rows = _half(ins[a].shape[0], c)
            dst = outs[a].at[quarter, rows]
            return pltpu.make_async_remote_copy(
                src_ref=dst if src is None else src, dst_ref=dst, send_sem=send.at[a, k], recv_sem=recv.at[a, k],
                device_id=to, device_id_type=MESH)

        own, first, passed = [], [], []
        for a in range(n):
            cp = pltpu.make_async_copy(ins[a], outs[a].at[mine], lsem.at[a])
            cp.start()
            own.append(cp)
            for k, (cx, cy) in enumerate(chips):
                cp = copy(a, k, mine, (cx, cy, c), src=ins[a].at[_half(ins[a].shape[0], c)])
                cp.start()
                first.append(cp)
        for a in range(n):
            for k, (cx, cy) in enumerate(chips):
                copy(a, k, 2 * cx + cy, (x, y, c)).wait_recv()
                cp = copy(a, 3 + k, 2 * cx + cy, (x, y, 1 - c))
                cp.start()
                passed.append(cp)
        for a in range(n):
            for k, (cx, cy) in enumerate(chips):
                rows = _half(ins[a].shape[0], 1 - c)
                got = outs[a].at[2 * cx + cy, rows]
                pltpu.make_async_remote_copy(
                    src_ref=got, dst_ref=got, send_sem=send.at[a, 3 + k], recv_sem=recv.at[a, 3 + k],
                    device_id=(x, y, 1 - c), device_id_type=MESH).wait_recv()
        for cp in first + passed:
            cp.wait_send()
        for cp in own:
            cp.wait()

    return pl.pallas_call(
        body, name="allgather_shards", in_specs=[ANY] * n, out_specs=[ANY] * n,
        out_shape=[jax.ShapeDtypeStruct((N_SHARD,) + s.shape, s.dtype) for s in shards],
        scratch_shapes=[pltpu.SemaphoreType.DMA((n, 6)), pltpu.SemaphoreType.DMA((n, 6)),
                        pltpu.SemaphoreType.DMA((n,))],
        compiler_params=pltpu.CompilerParams(has_side_effects=True),
    )(*shards)


def _flips():
    return [(fx, fy, fc) for fx in (0, 1) for fy in (0, 1) for fc in (0, 1) if (fx, fy, fc) != (0, 0, 0)]


def _exchange_grads(grads):
    n = len(grads)

    def body(*refs):
        ins, outs = refs[:n], refs[n:2 * n]
        send, recv, lsem = refs[2 * n:]
        x, y, c = _place()
        me = 4 * x + 2 * y + c
        sends, own = [], []
        for a in range(n):
            rows = ins[a].shape[1]
            cp = pltpu.make_async_copy(ins[a].at[2 * x + y, _half(rows, c)], outs[a].at[me], lsem.at[a])
            cp.start()
            own.append(cp)
            for k, (fx, fy, fc) in enumerate(_flips()):
                px, py, pc = x ^ fx, y ^ fy, c ^ fc
                cp = pltpu.make_async_remote_copy(
                    src_ref=ins[a].at[2 * px + py, _half(rows, pc)], dst_ref=outs[a].at[me],
                    send_sem=send.at[a, k], recv_sem=recv.at[a, k], device_id=(px, py, pc), device_id_type=MESH)
                cp.start()
                sends.append(cp)
        for a in range(n):
            rows = ins[a].shape[1]
            for k, (fx, fy, fc) in enumerate(_flips()):
                px, py, pc = x ^ fx, y ^ fy, c ^ fc
                pltpu.make_async_remote_copy(
                    src_ref=ins[a].at[0, _half(rows, 0)], dst_ref=outs[a].at[4 * px + 2 * py + pc],
                    send_sem=send.at[a, k], recv_sem=recv.at[a, k], device_id=(px, py, pc),
                    device_id_type=MESH).wait_recv()
        for cp in sends:
            cp.wait_send()
        for cp in own:
            cp.wait()

    return pl.pallas_call(
        body, name="exchange_grads", in_specs=[ANY] * n, out_specs=[ANY] * n,
        out_shape=[jax.ShapeDtypeStruct((N_DEV, g.shape[1] // 2, g.shape[2]), g.dtype) for g in grads],
        scratch_shapes=[pltpu.SemaphoreType.DMA((n, 7)), pltpu.SemaphoreType.DMA((n, 7)),
                        pltpu.SemaphoreType.DMA((n,))],
        compiler_params=pltpu.CompilerParams(has_side_effects=True),
    )(*grads)


def _share_halves(halves, layout):
    n = len(halves)
    tensors = sorted({t for t, _ in layout})
    n_layers = {t: 1 + max(l for tt, l in layout if tt == t) for t in tensors}
    shape_of = {t: next(h.shape for h, (tt, _) in zip(halves, layout) if tt == t) for t in tensors}

    def body(*refs):
        ins, outs = refs[:n], refs[n:n + len(tensors)]
        send, recv, lsem = refs[n + len(tensors):]
        x, y, c = _place()
        copies = []
        for a, (t, l) in enumerate(layout):
            rows = ins[a].shape[0]
            dst = outs[tensors.index(t)].at[l, pl.ds(c * rows, rows)]
            cp = pltpu.make_async_copy(ins[a], dst, lsem.at[a])
            cp.start()
            rc = pltpu.make_async_remote_copy(
                src_ref=ins[a], dst_ref=dst, send_sem=send.at[a], recv_sem=recv.at[a],
                device_id=(x, y, 1 - c), device_id_type=MESH)
            rc.start()
            copies.append((cp, rc))
        for cp, rc in copies:
            rc.wait()
            cp.wait()

    return pl.pallas_call(
        body, name="share_halves", in_specs=[ANY] * n, out_specs=[ANY] * len(tensors),
        out_shape=[jax.ShapeDtypeStruct((n_layers[t], 2 * shape_of[t][0], shape_of[t][1]), F32) for t in tensors],
        scratch_shapes=[pltpu.SemaphoreType.DMA((n,)), pltpu.SemaphoreType.DMA((n,)), pltpu.SemaphoreType.DMA((n,))],
        compiler_params=pltpu.CompilerParams(has_side_effects=True),
    )(*halves)


def _allgather_small(v):
    def body(v_ref, o_ref, send, recv, lsem):
        x, y, c = _place()
        me = 4 * x + 2 * y + c
        own = pltpu.make_async_copy(v_ref, o_ref.at[me], lsem)
        own.start()
        sends = []
        for k, (fx, fy, fc) in enumerate(_flips()):
            cp = pltpu.make_async_remote_copy(
                src_ref=v_ref, dst_ref=o_ref.at[me], send_sem=send.at[k], recv_sem=recv.at[k],
                device_id=(x ^ fx, y ^ fy, c ^ fc), device_id_type=MESH)
            cp.start()
            sends.append(cp)
        for k, (fx, fy, fc) in enumerate(_flips()):
            px, py, pc = x ^ fx, y ^ fy, c ^ fc
            pltpu.make_async_remote_copy(
                src_ref=v_ref, dst_ref=o_ref.at[4 * px + 2 * py + pc], send_sem=send.at[k], recv_sem=recv.at[k],
                device_id=(px, py, pc), device_id_type=MESH).wait_recv()
        for cp in sends:
            cp.wait_send()
        own.wait()

    return pl.pallas_call(
        body, name="allgather_small", in_specs=[ANY], out_specs=ANY,
        out_shape=jax.ShapeDtypeStruct((N_DEV,) + v.shape, v.dtype),
        scratch_shapes=[pltpu.SemaphoreType.DMA((7,)), pltpu.SemaphoreType.DMA((7,)), pltpu.SemaphoreType.DMA(())],
        compiler_params=pltpu.CompilerParams(has_side_effects=True),
    )(v)


def _mm_col_fwd(name, a, w4, l, cb, out_dtype, parts, epi_act=None, w4b=None, tm_pref=512):
    T, K = a.shape
    cs = w4.shape[3]
    N = N_SHARD * cs
    tm = _pick(T, tm_pref, 8)
    nps = cs // cb
    npp = (N // parts) // cb
    a_spec = _spec((tm, K), lambda j, i, k: (i, 0))
    b_spec = _spec((None, None, K, cb), lambda j, i, k: (j // nps, l, 0, j % nps))
    if parts == 1:
        o_spec = _spec((tm, cb), lambda j, i, k: (i, j))
        o_shape = (T, N)
    else:
        o_spec = _spec((None, tm, cb), lambda j, i, k: (j // npp, i, j % npp))
        o_shape = (parts, T, N // parts)
    b_ops = [(w4, b_spec)] if w4b is None else [(w4, b_spec), (w4b, b_spec)]
    dots = [(0, 0, 0)] if w4b is None else [(0, 0, 0), (0, 1, 1)]
    out_dtypes = (out_dtype,) if w4b is None else (F32, F32, out_dtype)
    return _mm(name, (N // cb, T // tm, 1), [(a, a_spec)], b_ops, dots, NN, [(tm, cb)] * len(b_ops),
               [(o_shape, dt, o_spec) for dt in out_dtypes], epi=epi_act)


def _mm_row_fwd(name, a, w4, l, res):
    T, K = a.shape
    rs, N = w4.shape[2], w4.shape[3]
    tm = _pick(T, 512, 8)
    tn = _pick(N, 1024, 128)
    a_spec = _spec((tm, rs), lambda j, i, k: (i, k))
    b_spec = _spec((None, None, rs, tn), lambda j, i, k: (k, l, 0, j))
    o_spec = _spec((tm, tn), lambda j, i, k: (i, j))
    return _mm(name, (N // tn, T // tm, N_SHARD), [(a, a_spec)], [(w4, b_spec)], [(0, 0, 0)], NN, [(tm, tn)],
               [((T, N), F32, o_spec)], epi=lambda accs, ex: [accs[0] + ex[0]], extras=[(res, o_spec)])


def _mm_row_bwd_data(name, dy, w4, l, out_dtype=BF16, epi=None, extras=(), n_out=1, tm_pref=512):
    T, N = dy.shape
    rs = w4.shape[2]
    K = N_SHARD * rs
    tm = _pick(T, tm_pref, 8)
    a_spec = _spec((tm, N), lambda j, i, k: (i, 0))
    b_spec = _spec((None, None, rs, N), lambda j, i, k: (j, l, 0, 0))
    o_spec = _spec((tm, rs), lambda j, i, k: (i, j))
    return _mm(name, (N_SHARD, T // tm, 1), [(dy, a_spec)], [(w4, b_spec)], [(0, 0, 0)], NT, [(tm, rs)],
               [((T, K), out_dtype, o_spec)] * n_out, epi=epi, extras=[(e, o_spec) for e in extras])


def _mm_row_bwd_weight(name, a, dy, rs):
    T, K = a.shape
    N = dy.shape[1]
    tk = _pick(T, 512, 8)
    tn = _pick(N, 1024, 128)
    a_spec = _spec((tk, rs), lambda i, j, k: (k, i))
    b_spec = _spec((tk, tn), lambda i, j, k: (k, j))
    o_spec = _spec((None, rs, tn), lambda i, j, k: (i, 0, j))
    return _mm(name, (N_SHARD, N // tn, T // tk), [(a, a_spec)], [(dy, b_spec)], [(0, 0, 0)], TN, [(rs, tn)],
               [((N_SHARD, rs, N), BF16, o_spec)])[0]


def _mm_col_bwd_data(name, dys, w4s, l, cb, parts, tm_pref=512):
    T = dys[0].shape[-2]
    K, cs = w4s[0].shape[2], w4s[0].shape[3]
    N = N_SHARD * cs
    tm = _pick(T, tm_pref, 8)
    nps = cs // cb
    npp = (N // parts) // cb
    if parts == 1:
        a_spec = _spec((tm, cb), lambda i, j, k: (i, k))
    else:
        a_spec = _spec((None, tm, cb), lambda i, j, k: (k // npp, i, k % npp))
    b_spec = _spec((None, None, K, cb), lambda i, j, k: (k // nps, l, 0, k % nps))
    o_spec = _spec((tm, K), lambda i, j, k: (i, 0))
    return _mm(name, (T // tm, 1, N // cb), [(d, a_spec) for d in dys], [(w, b_spec) for w in w4s],
               [(p, p, 0) for p in range(len(dys))], NT, [(tm, K)], [((T, K), F32, o_spec)])[0]


def _mm_col_bwd_weight(name, a, dys, cs, cb, parts):
    T, K = a.shape
    N = N_SHARD * cs
    tk = _pick(T, 512, 8)
    tm = _pick(K, 512, 128)
    nps = cs // cb
    npp = (N // parts) // cb
    a_spec = _spec((tk, tm), lambda i, j, k: (k, i))
    if parts == 1:
        b_spec = _spec((tk, cb), lambda i, j, k: (k, j))
    else:
        b_spec = _spec((None, tk, cb), lambda i, j, k: (j // npp, k, j % npp))
    o_spec = _spec((None, tm, cb), lambda i, j, k: (j // nps, i, j % nps))
    nd = len(dys)
    return _mm(name, (K // tm, N // cb, T // tk), [(a, a_spec)], [(d, b_spec) for d in dys],
               [(0, p, p) for p in range(nd)], TN, [(tm, cb)] * nd, [((N_SHARD, K, cs), BF16, o_spec)] * nd)


def _swiglu(accs, _):
    g, up = accs
    return [g, up, g * jax.nn.sigmoid(g) * up]


def _swiglu_bwd(accs, ex):
    da = accs[0]
    g = ex[0].astype(F32)
    up = ex[1].astype(F32)
    s = jax.nn.sigmoid(g)
    return [da * up * (s * (1.0 + g * (1.0 - s))), da * (g * s)]


def _ffn_fwd(x1, g_ffn, wg4, wu4, wd4, l):
    h2 = _rmsnorm_fwd(x1, g_ffn, BF16)
    fs = wg4.shape[3]
    gate, up, act = _mm_col_fwd("ffn_up", h2, wg4, l, fs, BF16, 1, epi_act=_swiglu, w4b=wu4, tm_pref=256)
    x2 = _mm_row_fwd("ffn_down", act, wd4, l, x1)[0]
    return x2, (x1, h2, gate, up, act)


def _ffn_bwd(dx2, dx2b, saved, g_ffn, wg4, wu4, wd4, l):
    x1, h2, gate, up, act = saved
    fs = wg4.shape[3]
    dgate, dup = _mm_row_bwd_data("ffn_down_bwd_data", dx2b, wd4, l, epi=_swiglu_bwd, extras=(gate, up), n_out=2,
                                  tm_pref=256)
    g_down = _mm_row_bwd_weight("ffn_down_bwd_weight", act, dx2b, fs)
    dh2 = _mm_col_bwd_data("ffn_up_bwd_data", [dgate, dup], [wg4, wu4], l, fs, 1, tm_pref=256)
    g_gate, g_up = _mm_col_bwd_weight("ffn_up_bwd_weight", h2, [dgate, dup], fs, fs, 1)
    dx1, dx1b, dg = _rmsnorm_bwd(x1, g_ffn, dh2, dx2)
    return dx1, dx1b, dg, g_gate, g_up, g_down


def _sb_fwd(x, g_mix, gq, gk, wqkv4, wo4, l):
    D = x.shape[1]
    h = _rmsnorm_fwd(x, g_mix, BF16)
    qkv = _mm_col_fwd("qkv_proj", h, wqkv4, l, D // 4, F32, 3)[0]
    qs, kn, vb = _headnorm_fwd(qkv, gq, gk)
    o, tot = _attn_fwd(qs, kn, vb)
    x1 = _mm_row_fwd("attn_out", o, wo4, l, x)[0]
    return x1, (x, h, qkv, qs, kn, vb, o, tot)


def _sb_bwd(dx1, dx1b, saved, g_mix, gq, gk, wqkv4, wo4, l):
    x, h, qkv, qs, kn, vb, o, tot = saved
    D = x.shape[1]
    do = _mm_row_bwd_data("attn_out_bwd_data", dx1b, wo4, l)[0]
    g_wo = _mm_row_bwd_weight("attn_out_bwd_weight", o, dx1b, wo4.shape[2])
    dqs, dkn, dv = _attn_bwd(qs, kn, vb, tot, do)
    dqkv, dgq, dgk = _headnorm_bwd(qkv, gq, gk, dqs, dkn, dv)
    dh = _mm_col_bwd_data("qkv_bwd_data", [dqkv], [wqkv4], l, D // 4, 3)
    g_wqkv = _mm_col_bwd_weight("qkv_bwd_weight", h, [dqkv], wqkv4.shape[3], D // 4, 3)[0]
    dx, dxb, dg = _rmsnorm_bwd(x, g_mix, dh, dx1)
    return dx, dxb, dg, dgq, dgk, g_wqkv, g_wo


def _pool_mix_fwd(x, g_mix, wp4, scale):
    T, D = x.shape
    C = D // N_POOL_GROUPS
    rq = C // N_SHARD
    h = _rmsnorm_fwd(x, g_mix, F32)
    p = _pool_fwd(h)
    tm = _pick(T, 512, 8)
    a_spec = _spec((tm, rq), lambda g, i, k: (i, g * N_SHARD + k))
    b_spec = _spec((None, None, rq, C), lambda g, i, k: (k, 0, g, 0))
    o_spec = _spec((tm, C), lambda g, i, k: (i, g))
    s_spec = _spec((1, C), lambda g, i, k: (0, g))
    x1, ypre = _mm("pool_mix", (N_POOL_GROUPS, T // tm, N_SHARD), [(p, a_spec)], [(wp4, b_spec)], [(0, 0, 0)], NN,
                   [(tm, C)], [((T, D), F32, o_spec), ((T, D), BF16, o_spec)],
                   epi=lambda accs, ex: [ex[0] + accs[0] * ex[1], accs[0]], extras=[(x, o_spec), (scale, s_spec)])
    return x1, (x, h, p, ypre)


def _pool_mix_bwd(dx1, dx1b, saved, g_mix, wp4, scale):
    x, h, p, ypre = saved
    T, D = x.shape
    C = D // N_POOL_GROUPS
    rq = C // N_SHARD
    dys, dscale = _pool_scale_bwd(dx1, ypre, scale)
    tm = _pick(T, 512, 8)
    dp = _mm("pool_mix_bwd_data", (N_POOL_GROUPS * N_SHARD, T // tm, 1),
             [(dys, _spec((tm, C), lambda j, i, k: (i, j // N_SHARD)))],
             [(wp4, _spec((None, None, rq, C), lambda j, i, k: (j % N_SHARD, 0, j // N_SHARD, 0)))],
             [(0, 0, 0)], NT, [(tm, rq)], [((T, D), F32, _spec((tm, rq), lambda j, i, k: (i, j)))])[0]
    tk = _pick(T, 512, 8)
    g_wp = _mm("pool_mix_bwd_weight", (N_POOL_GROUPS * N_SHARD, 1, T // tk),
               [(p, _spec((tk, rq), lambda j, n, k: (k, j)))],
               [(dys, _spec((tk, C), lambda j, n, k: (k, j // N_SHARD)))],
               [(0, 0, 0)], TN, [(rq, C)],
               [((N_SHARD, N_POOL_GROUPS * rq, C), BF16,
                 _spec((None, rq, C), lambda j, n, k: (j % N_SHARD, j // N_SHARD, 0)))])[0]
    dh = _pool_bwd(dp)
    dx, dxb, dg = _rmsnorm_bwd(x, g_mix, dh, dx1)
    return dx, dxb, dg, dscale, g_wp


def _conv_mix_fwd(x, g_mix, win4, taps4, wout4):
    D = x.shape[1]
    h = _rmsnorm_fwd(x, g_mix, BF16)
    bcx = _mm_col_fwd("conv_in", h, win4, 0, D // 4, F32, 3)[0]
    q = _conv_fwd(bcx, taps4)
    x1 = _mm_row_fwd("conv_out", q, wout4, 0, x)[0]
    return x1, (x, h, bcx, q)


def _conv_mix_bwd(dx1, dx1b, saved, g_mix, win4, taps4, wout4):
    x, h, bcx, q = saved
    D = x.shape[1]
    dq = _mm_row_bwd_data("conv_out_bwd_data", dx1b, wout4, 0, out_dtype=F32)[0]
    g_wout = _mm_row_bwd_weight("conv_out_bwd_weight", q, dx1b, wout4.shape[2])
    dbcx, dtaps = _conv_bwd(dq, bcx, taps4)
    dh = _mm_col_bwd_data("conv_in_bwd_data", [dbcx], [win4], 0, D // 4, 3)
    g_win = _mm_col_bwd_weight("conv_in_bwd_weight", h, [dbcx], win4.shape[3], D // 4, 3)[0]
    dx, dxb, dg = _rmsnorm_bwd(x, g_mix, dh, dx1)
    return dx, dxb, dg, dtaps, g_wout, g_win


def _rows2d(w):
    return w.reshape(-1, w.shape[-1])


def _pad_rows(v, rows):
    return jnp.pad(v, ((0, rows - v.shape[0]), (0, 0)))


def kernel(x, norm_mix_g, norm_ffn_g, sb_w_qkv, sb_g_q, sb_g_k, sb_w_o, pool_w, pool_scale, conv_w_in, conv_w, conv_w_out, ffn_w_gate, ffn_w_up, ffn_w_down, loss_target, m_norm_mix_g, m_norm_ffn_g, m_sb_w_qkv, m_sb_g_q, m_sb_g_k, m_sb_w_o, m_pool_w, m_pool_scale, m_conv_w_in, m_conv_w, m_conv_w_out, m_ffn_w_gate, m_ffn_w_up, m_ffn_w_down, v_norm_mix_g, v_norm_ffn_g, v_sb_w_qkv, v_sb_g_q, v_sb_g_k, v_sb_w_o, v_pool_w, v_pool_scale, v_conv_w_in, v_conv_w, v_conv_w_out, v_ffn_w_gate, v_ffn_w_up, v_ffn_w_down):
    T, D = x.shape[1], x.shape[2]
    depth = norm_mix_g.shape[0]
    big = dict(sb_w_qkv=(sb_w_qkv, m_sb_w_qkv, v_sb_w_qkv), sb_w_o=(sb_w_o, m_sb_w_o, v_sb_w_o),
               pool_w=(pool_w, m_pool_w, v_pool_w), conv_w_in=(conv_w_in, m_conv_w_in, v_conv_w_in),
               conv_w_out=(conv_w_out, m_conv_w_out, v_conv_w_out), ffn_w_gate=(ffn_w_gate, m_ffn_w_gate, v_ffn_w_gate),
               ffn_w_up=(ffn_w_up, m_ffn_w_up, v_ffn_w_up), ffn_w_down=(ffn_w_down, m_ffn_w_down, v_ffn_w_down))
    names = list(big)

    cs_conv = conv_w.shape[2]
    taps_local = _pad_rows(conv_w[0], 16)
    shards = [_cast_bf16(_rows2d(big[n][0])) for n in names] + [taps_local]
    gathered = _allgather_shards(shards)
    w4 = {}
    for n, g4 in zip(names, gathered[:-1]):
        layers = big[n][0].shape[0]
        w4[n] = g4.reshape(N_SHARD, layers, g4.shape[1] // layers, g4.shape[2])
    taps4 = gathered[-1]

    xs = x.reshape(T, D)
    saved = []
    for i in range(depth):
        kind, j = i % 3, i // 3
        g_mix = norm_mix_g[i:i + 1]
        if kind == 0:
            xs, sv = _sb_fwd(xs, g_mix, sb_g_q[j:j + 1], sb_g_k[j:j + 1], w4["sb_w_qkv"], w4["sb_w_o"], j)
        elif kind == 1:
            xs, sv = _pool_mix_fwd(xs, g_mix, w4["pool_w"], pool_scale[j:j + 1])
        else:
            xs, sv = _conv_mix_fwd(xs, g_mix, w4["conv_w_in"], taps4, w4["conv_w_out"])
        xs, sf = _ffn_fwd(xs, norm_ffn_g[i:i + 1], w4["ffn_w_gate"], w4["ffn_w_up"], w4["ffn_w_down"], i)
        saved.append((sv, sf))

    dxs, dxb, err2 = _loss_head(xs, loss_target.reshape(T, D))
    loss = lax.psum(0.5 * jnp.sum(err2) / D, ("x", "y", "c"))
    grads, layout = [], []
    small = {}
    for i in reversed(range(depth)):
        kind, j = i % 3, i // 3
        sv, sf = saved[i]
        dxs, dxb, dg, g_gate, g_up, g_down = _ffn_bwd(dxs, dxb, sf, norm_ffn_g[i:i + 1], w4["ffn_w_gate"],
                                                      w4["ffn_w_up"], w4["ffn_w_down"], i)
        small[("norm_ffn_g", i)] = dg
        grads += [g_gate, g_up, g_down]
        layout += [("ffn_w_gate", i), ("ffn_w_up", i), ("ffn_w_down", i)]
        g_mix = norm_mix_g[i:i + 1]
        if kind == 0:
            dxs, dxb, dg, dgq, dgk, g_wqkv, g_wo = _sb_bwd(dxs, dxb, sv, g_mix, sb_g_q[j:j + 1], sb_g_k[j:j + 1],
                                                          w4["sb_w_qkv"], w4["sb_w_o"], j)
            small[("sb_g_q", j)], small[("sb_g_k", j)] = dgq, dgk
            grads += [g_wqkv, g_wo]
            layout += [("sb_w_qkv", j), ("sb_w_o", j)]
        elif kind == 1:
            dxs, dxb, dg, dscale, g_wp = _pool_mix_bwd(dxs, dxb, sv, g_mix, w4["pool_w"], pool_scale[j:j + 1])
            small[("pool_scale", j)] = dscale
            grads += [g_wp]
            layout += [("pool_w", j)]
        else:
            dxs, dxb, dg, dtaps, g_wout, g_win = _conv_mix_bwd(dxs, dxb, sv, g_mix, w4["conv_w_in"], taps4,
                                                              w4["conv_w_out"])
            small[("conv_w", j)] = dtaps
            grads += [g_win, g_wout]
            layout += [("conv_w_in", j), ("conv_w_out", j)]
        small[("norm_mix_g", i)] = dg
    grad_x = dxs.reshape(x.shape)

    received = _exchange_grads(grads)
    halves = [_sum_devices(r) for r in received]
    full = _share_halves(halves, layout)
    tensors = sorted(set(t for t, _ in layout))
    g_big = {t: f for t, f in zip(tensors, full)}

    n_sb = sb_g_q.shape[0]
    gqk = jnp.concatenate([small[(n, j)] for j in range(n_sb) for n in ("sb_g_q", "sb_g_k")], axis=1)
    dtaps = small[("conv_w", 0)]
    taps_full = jnp.concatenate([dtaps[s, :3] for s in range(N_SHARD)], axis=1)
    pack = jnp.concatenate(
        [small[("norm_mix_g", i)] for i in range(depth)] + [small[("norm_ffn_g", i)] for i in range(depth)]
        + [small[("pool_scale", 0)], jnp.pad(gqk, ((0, 0), (0, D - gqk.shape[1]))), taps_full], axis=0)
    pack = _pad_rows(pack, SMALL_ROWS)
    g_small = _sum_devices(_allgather_small(pack))
    mine = 2 * lax.axis_index("x") + lax.axis_index("y")
    g_taps = lax.dynamic_slice(g_small, (2 * depth + 2, mine * cs_conv), (3, cs_conv))

    def pack_small(norm_mix, norm_ffn, scale, gq, gk, taps):
        qk = jnp.concatenate([v[j:j + 1] for j in range(n_sb) for v in (gq, gk)], axis=1)
        rows = jnp.concatenate([norm_mix, norm_ffn, scale, jnp.pad(qk, ((0, 0), (0, D - qk.shape[1]))),
                                jnp.pad(taps[0], ((0, 0), (0, D - cs_conv)))], axis=0)
        return _pad_rows(rows, SMALL_ROWS)

    g_pack = jnp.concatenate([g_small[:2 * depth + 2], jnp.pad(g_taps, ((0, 0), (0, D - cs_conv))),
                              jnp.zeros((SMALL_ROWS - 2 * depth - 5, D), F32)], axis=0)
    w_pack = pack_small(norm_mix_g, norm_ffn_g, pool_scale, sb_g_q, sb_g_k, conv_w)
    m_pack = pack_small(m_norm_mix_g, m_norm_ffn_g, m_pool_scale, m_sb_g_q, m_sb_g_k, m_conv_w)
    v_pack = pack_small(v_norm_mix_g, v_norm_ffn_g, v_pool_scale, v_sb_g_q, v_sb_g_k, v_conv_w)
    small_out = (g_pack,) + tuple(_adamw(w_pack, g_pack, m_pack, v_pack))

    def unpack_small(p):
        qk = p[2 * depth + 1]
        gq = jnp.stack([qk[(2 * j) * HEAD_DIM:(2 * j + 1) * HEAD_DIM] for j in range(n_sb)])
        gk = jnp.stack([qk[(2 * j + 1) * HEAD_DIM:(2 * j + 2) * HEAD_DIM] for j in range(n_sb)])
        return dict(norm_mix_g=p[:depth], norm_ffn_g=p[depth:2 * depth], pool_scale=p[2 * depth:2 * depth + 1],
                    sb_g_q=gq, sb_g_k=gk, conv_w=p[2 * depth + 2:2 * depth + 5, :cs_conv][None])

    results = [unpack_small(p) for p in small_out]
    for n in names:
        w, m, v = big[n]
        g = _rows2d(g_big[n])
        outs = (g,) + tuple(_adamw(_rows2d(w), g, _rows2d(m), _rows2d(v)))
        for r, o in zip(results, outs):
            r[n] = o.reshape(w.shape)

    order = ["norm_mix_g", "norm_ffn_g", "sb_w_qkv", "sb_g_q", "sb_g_k", "sb_w_o", "pool_w", "pool_scale",
             "conv_w_in", "conv_w", "conv_w_out", "ffn_w_gate", "ffn_w_up", "ffn_w_down"]
    return (loss, grad_x) + tuple(r[n] for r in results for n in order)
```

```python
import functools

import jax
import jax.numpy as jnp
from jax import lax
from jax.experimental import pallas as pl
from jax.experimental.pallas import tpu as pltpu

F32 = jnp.float32
BF16 = jnp.bfloat16

HEAD_DIM = 128
N_POOL_GROUPS = 4
EPS = 1e-6
N_SHARD = 4
N_DEV = 8
VMEM_LIMIT_BYTES = 56 * 2**20
KEY_BLOCK = 128
ATTN_ROWS = 512
ATTN_GROUP_FWD = 4
ATTN_GROUP_BWD = 2
SMALL_ROWS = 16

ADAM_LR = 0.001
ADAM_B1 = 0.9
ADAM_B2 = 0.999
ADAM_EPS = 1e-08
ADAM_WD = 0.01
ADAM_STEP = 10

MESH = pl.DeviceIdType.MESH
ANY = pl.BlockSpec(memory_space=pl.ANY)


def _params(*sem):
    return pltpu.CompilerParams(dimension_semantics=sem, vmem_limit_bytes=VMEM_LIMIT_BYTES)


def _pick(n, pref, unit):
    t = (min(pref, n) // unit) * unit
    while n % t:
        t -= unit
    return t


NN = ((1,), (0,))
NT = ((1,), (1,))
TN = ((0,), (0,))


def _mm(name, grid, a_ops, b_ops, dots, dims, acc_shapes, outs, epi=None, extras=()):
    na, nb, ne, no, nacc = len(a_ops), len(b_ops), len(extras), len(outs), len(acc_shapes)
    nk = grid[2]

    def body(*refs):
        a_refs, b_refs = refs[:na], refs[na:na + nb]
        e_refs = refs[na + nb:na + nb + ne]
        o_refs = refs[na + nb + ne:na + nb + ne + no]
        acc_refs = refs[na + nb + ne + no:]

        def partial_sums():
            sums = [None] * nacc
            for ai, bi, ci in dots:
                d = lax.dot_general(a_refs[ai][...].astype(BF16), b_refs[bi][...].astype(BF16),
                                    (dims, ((), ())), preferred_element_type=F32)
                sums[ci] = d if sums[ci] is None else sums[ci] + d
            return sums

        def finish(accs):
            res = epi(accs, [e[...] for e in e_refs]) if epi is not None else accs
            for o, r in zip(o_refs, res):
                o[...] = r.astype(o.dtype)

        if nk == 1:
            finish(partial_sums())
            return
        k = pl.program_id(2)

        @pl.when(k == 0)
        def _():
            for acc, s in zip(acc_refs, partial_sums()):
                acc[...] = s

        @pl.when(k > 0)
        def _():
            for acc, s in zip(acc_refs, partial_sums()):
                acc[...] += s

        @pl.when(k == nk - 1)
        def _():
            finish([acc[...] for acc in acc_refs])

    ops = list(a_ops) + list(b_ops) + list(extras)
    return pl.pallas_call(
        body, name=name, grid=grid,
        in_specs=[s for _, s in ops],
        out_specs=[s for _, _, s in outs],
        out_shape=[jax.ShapeDtypeStruct(sh, dt) for sh, dt, _ in outs],
        scratch_shapes=[pltpu.VMEM(s, F32) for s in acc_shapes] if nk > 1 else [],
        compiler_params=_params("parallel", "parallel", "arbitrary"),
    )(*[a for a, _ in ops])


def _spec(block, index):
    return pl.BlockSpec(block, index)


def _rmsnorm_fwd(x, g, out_dtype):
    T, D = x.shape
    tm = _pick(T, 256, 8)

    def body(x_ref, g_ref, o_ref):
        xv = x_ref[...]
        r = lax.rsqrt(jnp.mean(xv * xv, axis=-1, keepdims=True) + EPS)
        o_ref[...] = (xv * r * g_ref[...]).astype(o_ref.dtype)

    row = _spec((tm, D), lambda i: (i, 0))
    return pl.pallas_call(
        body, name="rmsnorm_fwd", grid=(T // tm,),
        in_specs=[row, _spec((1, D), lambda i: (0, 0))], out_specs=row,
        out_shape=jax.ShapeDtypeStruct((T, D), out_dtype), compiler_params=_params("parallel"),
    )(x, g)


def _rmsnorm_bwd(x, g, dh, dres):
    T, D = x.shape
    tm = _pick(T, 256, 8)

    def body(x_ref, g_ref, dh_ref, dres_ref, dx_ref, dxb_ref, dg_ref):
        xv = x_ref[...]
        dhv = dh_ref[...].astype(F32)
        r = lax.rsqrt(jnp.mean(xv * xv, axis=-1, keepdims=True) + EPS)
        xh = xv * r
        dxh = dhv * g_ref[...]
        dx = r * (dxh - xh * jnp.mean(dxh * xh, axis=-1, keepdims=True)) + dres_ref[...]
        dx_ref[...] = dx
        dxb_ref[...] = dx.astype(BF16)
        part = jnp.sum(dhv * xh, axis=0, keepdims=True)

        @pl.when(pl.program_id(0) == 0)
        def _():
            dg_ref[...] = part

        @pl.when(pl.program_id(0) > 0)
        def _():
            dg_ref[...] += part

    row = _spec((tm, D), lambda i: (i, 0))
    vec = _spec((1, D), lambda i: (0, 0))
    return pl.pallas_call(
        body, name="rmsnorm_bwd", grid=(T // tm,),
        in_specs=[row, vec, row, row], out_specs=[row, row, vec],
        out_shape=[jax.ShapeDtypeStruct((T, D), F32), jax.ShapeDtypeStruct((T, D), BF16),
                   jax.ShapeDtypeStruct((1, D), F32)],
        compiler_params=_params("arbitrary"),
    )(x, g, dh, dres)


def _loss_head(y, target):
    T, D = y.shape
    tm = _pick(T, 256, 8)

    def body(y_ref, t_ref, dy_ref, dyb_ref, l_ref):
        err = y_ref[...] - t_ref[...]
        dy = err * (1.0 / D)
        dy_ref[...] = dy
        dyb_ref[...] = dy.astype(BF16)
        part = jnp.sum(err * err, axis=0, keepdims=True)

        @pl.when(pl.program_id(0) == 0)
        def _():
            l_ref[...] = part

        @pl.when(pl.program_id(0) > 0)
        def _():
            l_ref[...] += part

    row = _spec((tm, D), lambda i: (i, 0))
    vec = _spec((1, D), lambda i: (0, 0))
    return pl.pallas_call(
        body, name="loss_head", grid=(T // tm,),
        in_specs=[row, row], out_specs=[row, row, vec],
        out_shape=[jax.ShapeDtypeStruct((T, D), F32), jax.ShapeDtypeStruct((T, D), BF16),
                   jax.ShapeDtypeStruct((1, D), F32)],
        compiler_params=_params("arbitrary"),
    )(y, target)


def _headnorm_fwd(qkv, gq, gk):
    _, T, D = qkv.shape
    H = D // HEAD_DIM
    tm = _pick(T, 256, 8)
    scale = HEAD_DIM ** -0.5

    def body(q_ref, k_ref, v_ref, gq_ref, gk_ref, qs_ref, kn_ref, vb_ref):
        for h in range(H):
            sl = slice(h * HEAD_DIM, (h + 1) * HEAD_DIM)
            q = q_ref[:, sl]
            qs_ref[:, sl] = (q * lax.rsqrt(jnp.mean(q * q, axis=-1, keepdims=True) + EPS)
                             * (gq_ref[...] * scale)).astype(BF16)
            k = k_ref[:, sl]
            kn_ref[:, sl] = (k * lax.rsqrt(jnp.mean(k * k, axis=-1, keepdims=True) + EPS)
                             * gk_ref[...]).astype(BF16)
        vb_ref[...] = v_ref[...].astype(BF16)

    part = lambda p: _spec((None, tm, D), lambda i, p=p: (p, i, 0))
    row = _spec((tm, D), lambda i: (i, 0))
    vec = _spec((1, HEAD_DIM), lambda i: (0, 0))
    return pl.pallas_call(
        body, name="headnorm_fwd", grid=(T // tm,),
        in_specs=[part(0), part(1), part(2), vec, vec], out_specs=[row, row, row],
        out_shape=[jax.ShapeDtypeStruct((T, D), BF16)] * 3, compiler_params=_params("parallel"),
    )(qkv, qkv, qkv, gq, gk)


def _headnorm_bwd(qkv, gq, gk, dqs, dkn, dv):
    _, T, D = qkv.shape
    H = D // HEAD_DIM
    tm = _pick(T, 256, 8)
    scale = HEAD_DIM ** -0.5

    def body(q_ref, k_ref, gq_ref, gk_ref, dqs_ref, dkn_ref, dv_ref, dqkv_ref, dgq_ref, dgk_ref):
        dgq = jnp.zeros((1, HEAD_DIM), F32)
        dgk = jnp.zeros((1, HEAD_DIM), F32)
        for h in range(H):
            sl = slice(h * HEAD_DIM, (h + 1) * HEAD_DIM)
            for src, dsrc, g_ref, sc, p in ((q_ref, dqs_ref, gq_ref, scale, 0), (k_ref, dkn_ref, gk_ref, 1.0, 1)):
                v = src[:, sl]
                r = lax.rsqrt(jnp.mean(v * v, axis=-1, keepdims=True) + EPS)
                vh = v * r
                dn = dsrc[:, sl] * sc
                dvh = dn * g_ref[...]
                dqkv_ref[p, :, sl] = (r * (dvh - vh * jnp.mean(dvh * vh, axis=-1, keepdims=True))).astype(BF16)
                dg = jnp.sum(dn * vh, axis=0, keepdims=True)
                if p == 0:
                    dgq = dgq + dg
                else:
                    dgk = dgk + dg
        dqkv_ref[2] = dv_ref[...].astype(BF16)

        @pl.when(pl.program_id(0) == 0)
        def _():
            dgq_ref[...] = dgq
            dgk_ref[...] = dgk

        @pl.when(pl.program_id(0) > 0)
        def _():
            dgq_ref[...] += dgq
            dgk_ref[...] += dgk

    part = lambda p: _spec((None, tm, D), lambda i, p=p: (p, i, 0))
    row = _spec((tm, D), lambda i: (i, 0))
    vec = _spec((1, HEAD_DIM), lambda i: (0, 0))
    return pl.pallas_call(
        body, name="headnorm_bwd", grid=(T // tm,),
        in_specs=[part(0), part(1), vec, vec, row, row, row],
        out_specs=[_spec((3, tm, D), lambda i: (0, i, 0)), vec, vec],
        out_shape=[jax.ShapeDtypeStruct((3, T, D), BF16), jax.ShapeDtypeStruct((1, HEAD_DIM), F32),
                   jax.ShapeDtypeStruct((1, HEAD_DIM), F32)],
        compiler_params=_params("arbitrary"),
    )(qkv, qkv, gq, gk, dqs, dkn, dv)


def _sum_matrix(prefix):
    r = lax.broadcasted_iota(jnp.int32, (KEY_BLOCK, 2 * KEY_BLOCK), 0)
    c = lax.broadcasted_iota(jnp.int32, (KEY_BLOCK, 2 * KEY_BLOCK), 1)
    tri = (r <= c) if prefix else (r > c)
    return jnp.where(tri | (c >= KEY_BLOCK), 1.0, 0.0).astype(BF16)


def _block_sums(v, u):
    hi = v.astype(BF16)
    lo = (v - hi.astype(F32)).astype(BF16)
    s = (jnp.dot(hi, u, preferred_element_type=F32) + jnp.dot(lo, u, preferred_element_type=F32))
    return s[:, :KEY_BLOCK], s[:, KEY_BLOCK:]


def _log_terms(q, kj, mask):
    z = lax.dot_general(q, kj, (NT, ((), ())), preferred_element_type=F32)
    ls = jnp.minimum(z, 0.0) - jnp.log(1.0 + jnp.exp(-jnp.abs(z)))
    lk = ls - z
    if mask is not None:
        lk = jnp.where(mask, lk, 0.0)
    return ls, lk


def _causal_mask(row0, key0, tq):
    t = row0 + lax.broadcasted_iota(jnp.int32, (tq, KEY_BLOCK), 0)
    s = key0 + lax.broadcasted_iota(jnp.int32, (tq, KEY_BLOCK), 1)
    return s < t


def _attn_fwd(qs, kn, vb):
    T, D = qs.shape
    H = D // HEAD_DIM
    ATTN_GROUP = ATTN_GROUP_FWD
    tq = _pick(T, ATTN_ROWS, ATTN_GROUP * KEY_BLOCK)
    nd = tq // KEY_BLOCK

    def body(q_ref, k_ref, v_ref, u_ref, o_ref, tot_ref, acc_ref, run_ref):
        i = pl.program_id(1)
        q = q_ref[...]
        uv = u_ref[...]
        acc_ref[...] = jnp.zeros_like(acc_ref)
        run_ref[...] = jnp.zeros_like(run_ref)

        def group(first, masked):
            parts = []
            for n in range(ATTN_GROUP):
                k0 = pl.multiple_of((first - n) * KEY_BLOCK, KEY_BLOCK)
                mask = _causal_mask(i * tq, k0, tq) if masked else None
                ls, lk = _log_terms(q, k_ref[pl.ds(k0, KEY_BLOCK), :], mask)
                after, rows = _block_sums(lk, uv)
                parts.append((k0, mask, ls + after, rows))
            run = run_ref[...]
            acc = acc_ref[...]
            for k0, mask, base, rows in parts:
                a = jnp.exp(base + run)
                if masked:
                    a = jnp.where(mask, a, 0.0)
                acc = acc + jnp.dot(a.astype(BF16), v_ref[pl.ds(k0, KEY_BLOCK), :], preferred_element_type=F32)
                run = run + rows
            acc_ref[...] = acc
            run_ref[...] = run

        for d in range(nd // ATTN_GROUP):
            group(i * nd + nd - 1 - d * ATTN_GROUP, True)

        def below(n, c):
            group(i * nd - 1 - n * ATTN_GROUP, False)
            return c

        lax.fori_loop(0, i * (nd // ATTN_GROUP), below, 0)
        o_ref[...] = acc_ref[...].astype(o_ref.dtype)
        tot_ref[...] = run_ref[...]

    blk = _spec((tq, HEAD_DIM), lambda h, i: (i, h))
    col = _spec((T, HEAD_DIM), lambda h, i: (0, h))
    return pl.pallas_call(
        body, name="attn_fwd", grid=(H, T // tq),
        in_specs=[blk, col, col, _spec((KEY_BLOCK, 2 * KEY_BLOCK), lambda h, i: (0, 0))], out_specs=[blk, blk],
        out_shape=[jax.ShapeDtypeStruct((T, D), BF16), jax.ShapeDtypeStruct((T, D), F32)],
        scratch_shapes=[pltpu.VMEM((tq, HEAD_DIM), F32), pltpu.VMEM((tq, KEY_BLOCK), F32)],
        compiler_params=_params("parallel", "arbitrary"),
    )(qs, kn, vb, _sum_matrix(False))


def _attn_bwd(qs, kn, vb, tot, do):
    T, D = qs.shape
    H = D // HEAD_DIM
    ATTN_GROUP = ATTN_GROUP_BWD
    tq = _pick(T, ATTN_ROWS, ATTN_GROUP * KEY_BLOCK)
    nd = tq // KEY_BLOCK

    def body(q_ref, k_ref, v_ref, tot_ref, do_ref, u_ref, dq_ref, dk_ref, dv_ref, run_ref, grun_ref):
        i = pl.program_id(1)

        @pl.when(i == 0)
        def _():
            dk_ref[...] = jnp.zeros_like(dk_ref)
            dv_ref[...] = jnp.zeros_like(dv_ref)

        q = q_ref[...]
        dov = do_ref[...]
        uv = u_ref[...]
        dq_ref[...] = jnp.zeros_like(dq_ref)
        run_ref[...] = jnp.zeros_like(run_ref)
        grun_ref[...] = jnp.zeros_like(grun_ref)

        def group(first, masked):
            parts = []
            for n in range(ATTN_GROUP):
                k0 = pl.multiple_of((first + n) * KEY_BLOCK, KEY_BLOCK)
                keys = pl.ds(k0, KEY_BLOCK)
                mask = _causal_mask(i * tq, k0, tq) if masked else None
                kj = k_ref[keys, :]
                ls, lk = _log_terms(q, kj, mask)
                upto, rows = _block_sums(lk, uv)
                da = lax.dot_general(dov, v_ref[keys, :], (NT, ((), ())), preferred_element_type=F32)
                parts.append((keys, mask, kj, ls, tot_ref[...] - upto, rows, da))
            run = run_ref[...]
            grun = grun_ref[...]
            dqa = dq_ref[...]
            for keys, mask, kj, ls, right, rows, da in parts:
                a = jnp.exp(ls + (right - run))
                if masked:
                    a = jnp.where(mask, a, 0.0)
                g = a * da
                gupto, grows = _block_sums(g, uv)
                dz = g - jnp.exp(ls) * (grun + gupto)
                if masked:
                    dz = jnp.where(mask, dz, 0.0)
                dzb = dz.astype(BF16)
                dqa = dqa + jnp.dot(dzb, kj, preferred_element_type=F32)
                dk_ref[keys, :] += lax.dot_general(dzb, q, (TN, ((), ())), preferred_element_type=F32)
                dv_ref[keys, :] += lax.dot_general(a.astype(BF16), dov, (TN, ((), ())), preferred_element_type=F32)
                run = run + rows
                grun = grun + grows
            run_ref[...] = run
            grun_ref[...] = grun
            dq_ref[...] = dqa

        def below(n, c):
            group(n * ATTN_GROUP, False)
            return c

        lax.fori_loop(0, i * (nd // ATTN_GROUP), below, 0)
        for d in range(nd // ATTN_GROUP):
            group(i * nd + d * ATTN_GROUP, True)

    blk = _spec((tq, HEAD_DIM), lambda h, i: (i, h))
    col = _spec((T, HEAD_DIM), lambda h, i: (0, h))
    return pl.pallas_call(
        body, name="attn_bwd", grid=(H, T // tq),
        in_specs=[blk, col, col, blk, blk, _spec((KEY_BLOCK, 2 * KEY_BLOCK), lambda h, i: (0, 0))],
        out_specs=[blk, col, col],
        out_shape=[jax.ShapeDtypeStruct((T, D), F32)] * 3,
        scratch_shapes=[pltpu.VMEM((tq, KEY_BLOCK), F32), pltpu.VMEM((tq, KEY_BLOCK), F32)],
        compiler_params=_params("parallel", "arbitrary"),
    )(qs, kn, vb, tot, do, _sum_matrix(True))


def _shift_down(v, n):
    t = lax.broadcasted_iota(jnp.int32, v.shape, 0)
    return jnp.where(t >= n, pltpu.roll(v, n, 0), 0.0)


def _shift_up(v, n):
    rows = v.shape[0]
    t = lax.broadcasted_iota(jnp.int32, v.shape, 0)
    return jnp.where(t < rows - n, pltpu.roll(v, rows - n, 0), 0.0)


def _pool_window(j, cw, D):
    group = (j * cw) // (D // N_POOL_GROUPS)
    return jnp.left_shift(2, group)


def _pool_count(shape, w):
    t = lax.broadcasted_iota(jnp.int32, shape, 0)
    return jnp.minimum(t + 1, w).astype(F32)


def _pool_fwd(h):
    T, D = h.shape
    cw = min(256, D // N_POOL_GROUPS)

    def body(h_ref, p_ref):
        w = _pool_window(pl.program_id(0), cw, D)
        hv = h_ref[...]
        s = hv
        for n in (1, 2, 4, 8):
            s = jnp.where(n < w, s + _shift_down(s, n), s)
        p_ref[...] = (s / _pool_count(hv.shape, w) - hv).astype(p_ref.dtype)

    slab = _spec((T, cw), lambda j: (0, j))
    return pl.pallas_call(
        body, name="pool_fwd", grid=(D // cw,), in_specs=[slab], out_specs=slab,
        out_shape=jax.ShapeDtypeStruct((T, D), BF16), compiler_params=_params("parallel"),
    )(h)


def _pool_bwd(dp):
    T, D = dp.shape
    cw = min(256, D // N_POOL_GROUPS)

    def body(dp_ref, dh_ref):
        w = _pool_window(pl.program_id(0), cw, D)
        dpv = dp_ref[...]
        s = dpv / _pool_count(dpv.shape, w)
        for n in (1, 2, 4, 8):
            s = jnp.where(n < w, s + _shift_up(s, n), s)
        dh_ref[...] = (s - dpv).astype(dh_ref.dtype)

    slab = _spec((T, cw), lambda j: (0, j))
    return pl.pallas_call(
        body, name="pool_bwd", grid=(D // cw,), in_specs=[slab], out_specs=slab,
        out_shape=jax.ShapeDtypeStruct((T, D), F32), compiler_params=_params("parallel"),
    )(dp)


def _pool_scale_bwd(dx, ypre, scale):
    T, D = dx.shape
    tm = _pick(T, 256, 8)

    def body(dx_ref, y_ref, s_ref, dys_ref, ds_ref):
        dxv = dx_ref[...]
        dys_ref[...] = (dxv * s_ref[...]).astype(BF16)
        part = jnp.sum(dxv * y_ref[...].astype(F32), axis=0, keepdims=True)

        @pl.when(pl.program_id(0) == 0)
        def _():
            ds_ref[...] = part

        @pl.when(pl.program_id(0) > 0)
        def _():
            ds_ref[...] += part

    row = _spec((tm, D), lambda i: (i, 0))
    vec = _spec((1, D), lambda i: (0, 0))
    return pl.pallas_call(
        body, name="pool_scale_bwd", grid=(T // tm,), in_specs=[row, row, vec], out_specs=[row, vec],
        out_shape=[jax.ShapeDtypeStruct((T, D), BF16), jax.ShapeDtypeStruct((1, D), F32)],
        compiler_params=_params("arbitrary"),
    )(dx, ypre, scale)


def _conv_specs(T, D, cw, cs):
    part = lambda p: _spec((None, T, cw), lambda j, p=p: (p, 0, j))
    taps = _spec((None, 8, cw), lambda j: (j // (cs // cw), 0, j % (cs // cw)))
    return part, taps


def _conv_fwd(bcx, taps4):
    _, T, D = bcx.shape
    cs = taps4.shape[2]
    cw = min(128, cs)
    part, taps = _conv_specs(T, D, cw, cs)

    def body(b_ref, c_ref, u_ref, w_ref, q_ref):
        g = c_ref[...].astype(F32) * u_ref[...].astype(F32)
        w = w_ref[...]
        y = w[2:3] * g + w[1:2] * _shift_down(g, 1) + w[0:1] * _shift_down(g, 2)
        q_ref[...] = (b_ref[...].astype(F32) * y).astype(q_ref.dtype)

    return pl.pallas_call(
        body, name="conv_fwd", grid=(D // cw,), in_specs=[part(0), part(1), part(2), taps],
        out_specs=_spec((T, cw), lambda j: (0, j)),
        out_shape=jax.ShapeDtypeStruct((T, D), BF16), compiler_params=_params("parallel"),
    )(bcx, bcx, bcx, taps4)


def _conv_bwd(dq, bcx, taps4):
    _, T, D = bcx.shape
    cs = taps4.shape[2]
    cw = min(128, cs)
    part, taps = _conv_specs(T, D, cw, cs)

    def body(dq_ref, b_ref, c_ref, u_ref, w_ref, d_ref, dw_ref):
        b = b_ref[...].astype(F32)
        c = c_ref[...].astype(F32)
        uu = u_ref[...].astype(F32)
        dqv = dq_ref[...].astype(F32)
        w = w_ref[...]
        g = c * uu
        g1 = _shift_down(g, 1)
        g2 = _shift_down(g, 2)
        d_ref[0] = (dqv * (w[2:3] * g + w[1:2] * g1 + w[0:1] * g2)).astype(BF16)
        dy = dqv * b
        dg = w[2:3] * dy + w[1:2] * _shift_up(dy, 1) + w[0:1] * _shift_up(dy, 2)
        d_ref[1] = (dg * uu).astype(BF16)
        d_ref[2] = (dg * c).astype(BF16)
        dw_ref[0:1, :] = jnp.sum(dy * g2, axis=0, keepdims=True)
        dw_ref[1:2, :] = jnp.sum(dy * g1, axis=0, keepdims=True)
        dw_ref[2:3, :] = jnp.sum(dy * g, axis=0, keepdims=True)
        dw_ref[3:8, :] = jnp.zeros((5, cw), F32)

    return pl.pallas_call(
        body, name="conv_bwd", grid=(D // cw,),
        in_specs=[_spec((T, cw), lambda j: (0, j)), part(0), part(1), part(2), taps],
        out_specs=[_spec((3, T, cw), lambda j: (0, 0, j)), taps],
        out_shape=[jax.ShapeDtypeStruct((3, T, D), BF16), jax.ShapeDtypeStruct((N_SHARD, 8, cs), F32)],
        compiler_params=_params("parallel"),
    )(dq, bcx, bcx, bcx, taps4)


def _quarter():
    return 2 * lax.axis_index("x") + lax.axis_index("y")


def _cast_into_slot(w, dtype):
    R, C = w.shape
    tr = _pick(R, 512, 8)

    def body(w_ref, o_ref):
        o_ref[...] = w_ref[...].astype(o_ref.dtype)

    return pl.pallas_call(
        body, name="cast_into_slot", grid=(R // tr,),
        in_specs=[_spec((tr, C), lambda i: (i, 0))],
        out_specs=_spec((None, tr, C), lambda i: (_quarter(), i, 0)),
        out_shape=jax.ShapeDtypeStruct((N_SHARD, R, C), dtype), compiler_params=_params("parallel"),
    )(w)


def _sum_into(recv, own, l, prev):
    n, L, R2, C = recv.shape
    tr = _pick(R2, 256, 8)
    nb = R2 // tr

    def body(r_ref, o_ref, *rest):
        s = o_ref[...].astype(F32)
        for k in range(n):
            s = s + r_ref[k].astype(F32)
        rest[-1][...] = s

    in_specs = [_spec((n, None, tr, C), lambda i: (0, l, i, 0)),
                _spec((None, tr, C), lambda i: (_quarter(), lax.axis_index("c") * nb + i, 0))]
    args = [recv, own]
    if prev is not None:
        in_specs.append(ANY)
        args.append(prev)
    return pl.pallas_call(
        body, name="sum_into", grid=(nb,), in_specs=in_specs,
        out_specs=_spec((None, tr, C), lambda i: (l, lax.axis_index("c") * nb + i, 0)),
        out_shape=jax.ShapeDtypeStruct((L, 2 * R2, C), F32),
        input_output_aliases={} if prev is None else {2: 0},
        compiler_params=_params("parallel"),
    )(*args)


def _sum_devices(parts):
    n, R, C = parts.shape
    tr = _pick(R, 256, 8)

    def body(p_ref, o_ref):
        s = p_ref[0].astype(F32)
        for d in range(1, n):
            s = s + p_ref[d].astype(F32)
        o_ref[...] = s

    return pl.pallas_call(
        body, name="sum_devices", grid=(R // tr,),
        in_specs=[_spec((n, tr, C), lambda i: (0, i, 0))], out_specs=_spec((tr, C), lambda i: (i, 0)),
        out_shape=jax.ShapeDtypeStruct((R, C), F32), compiler_params=_params("parallel"),
    )(parts)


def _adamw(w, g, m, v):
    R, C = w.shape
    tr = _pick(R, 256, 8)

    def body(w_ref, g_ref, m_ref, v_ref, d_ref, nm_ref, nv_ref):
        gv = g_ref[...]
        m2 = ADAM_B1 * m_ref[...] + (1.0 - ADAM_B1) * gv
        v2 = ADAM_B2 * v_ref[...] + (1.0 - ADAM_B2) * (gv * gv)
        m_hat = m2 / (1.0 - ADAM_B1 ** ADAM_STEP)
        v_hat = v2 / (1.0 - ADAM_B2 ** ADAM_STEP)
        d_ref[...] = -ADAM_LR * (m_hat / (jnp.sqrt(v_hat) + ADAM_EPS) + ADAM_WD * w_ref[...])
        nm_ref[...] = m2
        nv_ref[...] = v2

    row = _spec((tr, C), lambda i: (i, 0))
    return pl.pallas_call(
        body, name="adamw", grid=(R // tr,), in_specs=[row] * 4, out_specs=[row] * 3,
        out_shape=[jax.ShapeDtypeStruct((R, C), F32)] * 3, compiler_params=_params("parallel"),
    )(w, g, m, v)


def _place():
    return lax.axis_index("x"), lax.axis_index("y"), lax.axis_index("c")


def _half(ref_rows, c):
    return pl.ds(c * (ref_rows // 2), ref_rows // 2)


def _allgather_shards(bufs):
    n = len(bufs)

    def body(*refs):
        outs = refs[n:2 * n]
        send, recv = refs[2 * n:]
        x, y, c = _place()
        chips = [(1 - x, y), (x, 1 - y), (1 - x, 1 - y)]

        def copy(a, k, quarter, core, to):
            part = outs[a].at[quarter, _half(outs[a].shape[1], core)]
            return pltpu.make_async_remote_copy(
                src_ref=part, dst_ref=part, send_sem=send.at[a, k], recv_sem=recv.at[a, k],
                device_id=to, device_id_type=MESH)

        first, passed = [], []
        for a in range(n):
            for k, (cx, cy) in enumerate(chips):
                cp = copy(a, k, 2 * x + y, c, (cx, cy, c))
                cp.start()
                first.append(cp)
        for a in range(n):
            for k, (cx, cy) in enumerate(chips):
                copy(a, k, 2 * cx + cy, c, (x, y, c)).wait_recv()
                cp = copy(a, 3 + k, 2 * cx + cy, c, (x, y, 1 - c))
                cp.start()
                passed.append(cp)
        for a in range(n):
            for k, (cx, cy) in enumerate(chips):
                copy(a, 3 + k, 2 * cx + cy, 1 - c, (x, y, 1 - c)).wait_recv()
        for cp in first + passed:
            cp.wait_send()

    return pl.pallas_call(
        body, name="allgather_shards", in_specs=[ANY] * n, out_specs=[ANY] * n,
        out_shape=[jax.ShapeDtypeStruct(b.shape, b.dtype) for b in bufs],
        input_output_aliases={a: a for a in range(n)},
        scratch_shapes=[pltpu.SemaphoreType.DMA((n, 6)), pltpu.SemaphoreType.DMA((n, 6))],
        compiler_params=pltpu.CompilerParams(has_side_effects=True),
    )(*bufs)


def _flips():
    return [(fx, fy, fc) for fx in (0, 1) for fy in (0, 1) for fc in (0, 1) if (fx, fy, fc) != (0, 0, 0)]


def _exchange_grads(grads, layout):
    n = len(grads)
    tensors = sorted({t for t, _ in layout})
    n_layers = {t: 1 + max(l for tt, l in layout if tt == t) for t in tensors}
    shape_of = {t: next(g.shape for g, (tt, _) in zip(grads, layout) if tt == t) for t in tensors}

    def body(*refs):
        ins, outs = refs[:n], refs[n:n + len(tensors)]
        send, recv = refs[n + len(tensors):]
        x, y, c = _place()
        sends = []
        for a, (t, l) in enumerate(layout):
            rows = ins[a].shape[1]
            for k, (fx, fy, fc) in enumerate(_flips()):
                px, py, pc = x ^ fx, y ^ fy, c ^ fc
                cp = pltpu.make_async_remote_copy(
                    src_ref=ins[a].at[2 * px + py, _half(rows, pc)], dst_ref=outs[tensors.index(t)].at[k, l],
                    send_sem=send.at[a, k], recv_sem=recv.at[a, k], device_id=(px, py, pc), device_id_type=MESH)
                cp.start()
                sends.append(cp)
        for cp in sends:
            cp.wait_recv()
        for cp in sends:
            cp.wait_send()

    return pl.pallas_call(
        body, name="exchange_grads", in_specs=[ANY] * n, out_specs=[ANY] * len(tensors),
        out_shape=[jax.ShapeDtypeStruct((N_DEV - 1, n_layers[t], shape_of[t][1] // 2, shape_of[t][2]), BF16)
                   for t in tensors],
        scratch_shapes=[pltpu.SemaphoreType.DMA((n, 7)), pltpu.SemaphoreType.DMA((n, 7))],
        compiler_params=pltpu.CompilerParams(has_side_effects=True),
    )(*grads)


def _share_halves(fulls):
    n = len(fulls)
    index = [(a, l) for a in range(n) for l in range(fulls[a].shape[0])]

    def body(*refs):
        outs = refs[n:2 * n]
        send, recv = refs[2 * n:]
        x, y, c = _place()

        def copy(j, core):
            a, l = index[j]
            part = outs[a].at[l, _half(outs[a].shape[1], core)]
            return pltpu.make_async_remote_copy(
                src_ref=part, dst_ref=part, send_sem=send.at[j], recv_sem=recv.at[j],
                device_id=(x, y, 1 - c), device_id_type=MESH)

        for j in range(len(index)):
            copy(j, c).start()
        for j in range(len(index)):
            copy(j, 1 - c).wait_recv()
        for j in range(len(index)):
            copy(j, c).wait_send()

    return pl.pallas_call(
        body, name="share_halves", in_specs=[ANY] * n, out_specs=[ANY] * n,
        out_shape=[jax.ShapeDtypeStruct(f.shape, f.dtype) for f in fulls],
        input_output_aliases={a: a for a in range(n)},
        scratch_shapes=[pltpu.SemaphoreType.DMA((len(index),)), pltpu.SemaphoreType.DMA((len(index),))],
        compiler_params=pltpu.CompilerParams(has_side_effects=True),
    )(*fulls)


def _allgather_small(v):
    def body(v_ref, o_ref, send, recv, lsem):
        x, y, c = _place()
        me = 4 * x + 2 * y + c
        own = pltpu.make_async_copy(v_ref, o_ref.at[me], lsem)
        own.start()
        sends = []
        for k, (fx, fy, fc) in enumerate(_flips()):
            cp = pltpu.make_async_remote_copy(
                src_ref=v_ref, dst_ref=o_ref.at[me], send_sem=send.at[k], recv_sem=recv.at[k],
                device_id=(x ^ fx, y ^ fy, c ^ fc), device_id_type=MESH)
            cp.start()
            sends.append(cp)
        for k, (fx, fy, fc) in enumerate(_flips()):
            px, py, pc = x ^ fx, y ^ fy, c ^ fc
            pltpu.make_async_remote_copy(
                src_ref=v_ref, dst_ref=o_ref.at[4 * px + 2 * py + pc], send_sem=send.at[k], recv_sem=recv.at[k],
                device_id=(px, py, pc), device_id_type=MESH).wait_recv()
        for cp in sends:
            cp.wait_send()
        own.wait()

    return pl.pallas_call(
        body, name="allgather_small", in_specs=[ANY], out_specs=ANY,
        out_shape=jax.ShapeDtypeStruct((N_DEV,) + v.shape, v.dtype),
        scratch_shapes=[pltpu.SemaphoreType.DMA((7,)), pltpu.SemaphoreType.DMA((7,)), pltpu.SemaphoreType.DMA(())],
        compiler_params=pltpu.CompilerParams(has_side_effects=True),
    )(v)


def _mm_col_fwd(name, a, w4, l, cb, out_dtype, parts, epi_act=None, w4b=None, tm_pref=512):
    T, K = a.shape
    cs = w4.shape[3]
    N = N_SHARD * cs
    tm = _pick(T, tm_pref, 8)
    nps = cs // cb
    npp = (N // parts) // cb
    a_spec = _spec((tm, K), lambda j, i, k: (i, 0))
    b_spec = _spec((None, None, K, cb), lambda j, i, k: (j // nps, l, 0, j % nps))
    if parts == 1:
        o_spec = _spec((tm, cb), lambda j, i, k: (i, j))
        o_shape = (T, N)
    else:
        o_spec = _spec((None, tm, cb), lambda j, i, k: (j // npp, i, j % npp))
        o_shape = (parts, T, N // parts)
    b_ops = [(w4, b_spec)] if w4b is None else [(w4, b_spec), (w4b, b_spec)]
    dots = [(0, 0, 0)] if w4b is None else [(0, 0, 0), (0, 1, 1)]
    out_dtypes = (out_dtype,) if w4b is None else (F32, F32, out_dtype)
    return _mm(name, (N // cb, T // tm, 1), [(a, a_spec)], b_ops, dots, NN, [(tm, cb)] * len(b_ops),
               [(o_shape, dt, o_spec) for dt in out_dtypes], epi=epi_act)


def _mm_row_fwd(name, a, w4, l, res):
    T, K = a.shape
    rs, N = w4.shape[2], w4.shape[3]
    tm = _pick(T, 512, 8)
    tn = _pick(N, 1024, 128)
    a_spec = _spec((tm, rs), lambda j, i, k: (i, k))
    b_spec = _spec((None, None, rs, tn), lambda j, i, k: (k, l, 0, j))
    o_spec = _spec((tm, tn), lambda j, i, k: (i, j))
    return _mm(name, (N // tn, T // tm, N_SHARD), [(a, a_spec)], [(w4, b_spec)], [(0, 0, 0)], NN, [(tm, tn)],
               [((T, N), F32, o_spec)], epi=lambda accs, ex: [accs[0] + ex[0]], extras=[(res, o_spec)])


def _mm_row_bwd_data(name, dy, w4, l, out_dtype=BF16, epi=None, extras=(), n_out=1, tm_pref=512):
    T, N = dy.shape
    rs = w4.shape[2]
    K = N_SHARD * rs
    tm = _pick(T, tm_pref, 8)
    a_spec = _spec((tm, N), lambda j, i, k: (i, 0))
    b_spec = _spec((None, None, rs, N), lambda j, i, k: (j, l, 0, 0))
    o_spec = _spec((tm, rs), lambda j, i, k: (i, j))
    return _mm(name, (N_SHARD, T // tm, 1), [(dy, a_spec)], [(w4, b_spec)], [(0, 0, 0)], NT, [(tm, rs)],
               [((T, K), out_dtype, o_spec)] * n_out, epi=epi, extras=[(e, o_spec) for e in extras])


def _mm_row_bwd_weight(name, a, dy, rs):
    T, K = a.shape
    N = dy.shape[1]
    tk = _pick(T, 512, 8)
    tn = _pick(N, 1024, 128)
    a_spec = _spec((tk, rs), lambda i, j, k: (k, i))
    b_spec = _spec((tk, tn), lambda i, j, k: (k, j))
    o_spec = _spec((None, rs, tn), lambda i, j, k: (i, 0, j))
    return _mm(name, (N_SHARD, N // tn, T // tk), [(a, a_spec)], [(dy, b_spec)], [(0, 0, 0)], TN, [(rs, tn)],
               [((N_SHARD, rs, N), BF16, o_spec)])[0]


def _mm_col_bwd_data(name, dys, w4s, l, cb, parts, tm_pref=512):
    T = dys[0].shape[-2]
    K, cs = w4s[0].shape[2], w4s[0].shape[3]
    N = N_SHARD * cs
    tm = _pick(T, tm_pref, 8)
    nps = cs // cb
    npp = (N // parts) // cb
    if parts == 1:
        a_spec = _spec((tm, cb), lambda i, j, k: (i, k))
    else:
        a_spec = _spec((None, tm, cb), lambda i, j, k: (k // npp, i, k % npp))
    b_spec = _spec((None, None, K, cb), lambda i, j, k: (k // nps, l, 0, k % nps))
    o_spec = _spec((tm, K), lambda i, j, k: (i, 0))
    return _mm(name, (T // tm, 1, N // cb), [(d, a_spec) for d in dys], [(w, b_spec) for w in w4s],
               [(p, p, 0) for p in range(len(dys))], NT, [(tm, K)], [((T, K), F32, o_spec)])[0]


def _mm_col_bwd_weight(name, a, dys, cs, cb, parts):
    T, K = a.shape
    N = N_SHARD * cs
    tk = _pick(T, 512, 8)
    tm = _pick(K, 512, 128)
    nps = cs // cb
    npp = (N // parts) // cb
    a_spec = _spec((tk, tm), lambda i, j, k: (k, i))
    if parts == 1:
        b_spec = _spec((tk, cb), lambda i, j, k: (k, j))
    else:
        b_spec = _spec((None, tk, cb), lambda i, j, k: (j // npp, k, j % npp))
    o_spec = _spec((None, tm, cb), lambda i, j, k: (j // nps, i, j % nps))
    nd = len(dys)
    return _mm(name, (K // tm, N // cb, T // tk), [(a, a_spec)], [(d, b_spec) for d in dys],
               [(0, p, p) for p in range(nd)], TN, [(tm, cb)] * nd, [((N_SHARD, K, cs), BF16, o_spec)] * nd)


def _swiglu(accs, _):
    g, up = accs
    return [g, up, g * jax.nn.sigmoid(g) * up]


def _swiglu_bwd(accs, ex):
    da = accs[0]
    g = ex[0].astype(F32)
    up = ex[1].astype(F32)
    s = jax.nn.sigmoid(g)
    return [da * up * (s * (1.0 + g * (1.0 - s))), da * (g * s)]


def _ffn_fwd(x1, g_ffn, wg4, wu4, wd4, l):
    h2 = _rmsnorm_fwd(x1, g_ffn, BF16)
    fs = wg4.shape[3]
    gate, up, act = _mm_col_fwd("ffn_up", h2, wg4, l, fs, BF16, 1, epi_act=_swiglu, w4b=wu4, tm_pref=256)
    x2 = _mm_row_fwd("ffn_down", act, wd4, l, x1)[0]
    return x2, (x1, h2, gate, up, act)


def _ffn_bwd(dx2, dx2b, saved, g_ffn, wg4, wu4, wd4, l):
    x1, h2, gate, up, act = saved
    fs = wg4.shape[3]
    dgate, dup = _mm_row_bwd_data("ffn_down_bwd_data", dx2b, wd4, l, epi=_swiglu_bwd, extras=(gate, up), n_out=2,
                                  tm_pref=256)
    g_down = _mm_row_bwd_weight("ffn_down_bwd_weight", act, dx2b, fs)
    dh2 = _mm_col_bwd_data("ffn_up_bwd_data", [dgate, dup], [wg4, wu4], l, fs, 1, tm_pref=256)
    g_gate, g_up = _mm_col_bwd_weight("ffn_up_bwd_weight", h2, [dgate, dup], fs, fs, 1)
    dx1, dx1b, dg = _rmsnorm_bwd(x1, g_ffn, dh2, dx2)
    return dx1, dx1b, dg, g_gate, g_up, g_down


def _sb_fwd(x, g_mix, gq, gk, wqkv4, wo4, l):
    D = x.shape[1]
    h = _rmsnorm_fwd(x, g_mix, BF16)
    qkv = _mm_col_fwd("qkv_proj", h, wqkv4, l, D // 4, F32, 3)[0]
    qs, kn, vb = _headnorm_fwd(qkv, gq, gk)
    o, tot = _attn_fwd(qs, kn, vb)
    x1 = _mm_row_fwd("attn_out", o, wo4, l, x)[0]
    return x1, (x, h, qkv, qs, kn, vb, o, tot)


def _sb_bwd(dx1, dx1b, saved, g_mix, gq, gk, wqkv4, wo4, l):
    x, h, qkv, qs, kn, vb, o, tot = saved
    D = x.shape[1]
    do = _mm_row_bwd_data("attn_out_bwd_data", dx1b, wo4, l)[0]
    g_wo = _mm_row_bwd_weight("attn_out_bwd_weight", o, dx1b, wo4.shape[2])
    dqs, dkn, dv = _attn_bwd(qs, kn, vb, tot, do)
    dqkv, dgq, dgk = _headnorm_bwd(qkv, gq, gk, dqs, dkn, dv)
    dh = _mm_col_bwd_data("qkv_bwd_data", [dqkv], [wqkv4], l, D // 4, 3)
    g_wqkv = _mm_col_bwd_weight("qkv_bwd_weight", h, [dqkv], wqkv4.shape[3], D // 4, 3)[0]
    dx, dxb, dg = _rmsnorm_bwd(x, g_mix, dh, dx1)
    return dx, dxb, dg, dgq, dgk, g_wqkv, g_wo


def _pool_mix_fwd(x, g_mix, wp4, scale):
    T, D = x.shape
    C = D // N_POOL_GROUPS
    rq = C // N_SHARD
    h = _rmsnorm_fwd(x, g_mix, F32)
    p = _pool_fwd(h)
    tm = _pick(T, 512, 8)
    a_spec = _spec((tm, rq), lambda g, i, k: (i, g * N_SHARD + k))
    b_spec = _spec((None, None, rq, C), lambda g, i, k: (k, 0, g, 0))
    o_spec = _spec((tm, C), lambda g, i, k: (i, g))
    s_spec = _spec((1, C), lambda g, i, k: (0, g))
    x1, ypre = _mm("pool_mix", (N_POOL_GROUPS, T // tm, N_SHARD), [(p, a_spec)], [(wp4, b_spec)], [(0, 0, 0)], NN,
                   [(tm, C)], [((T, D), F32, o_spec), ((T, D), BF16, o_spec)],
                   epi=lambda accs, ex: [ex[0] + accs[0] * ex[1], accs[0]], extras=[(x, o_spec), (scale, s_spec)])
    return x1, (x, h, p, ypre)


def _pool_mix_bwd(dx1, dx1b, saved, g_mix, wp4, scale):
    x, h, p, ypre = saved
    T, D = x.shape
    C = D // N_POOL_GROUPS
    rq = C // N_SHARD
    dys, dscale = _pool_scale_bwd(dx1, ypre, scale)
    tm = _pick(T, 512, 8)
    dp = _mm("pool_mix_bwd_data", (N_POOL_GROUPS * N_SHARD, T // tm, 1),
             [(dys, _spec((tm, C), lambda j, i, k: (i, j // N_SHARD)))],
             [(wp4, _spec((None, None, rq, C), lambda j, i, k: (j % N_SHARD, 0, j // N_SHARD, 0)))],
             [(0, 0, 0)], NT, [(tm, rq)], [((T, D), F32, _spec((tm, rq), lambda j, i, k: (i, j)))])[0]
    tk = _pick(T, 512, 8)
    g_wp = _mm("pool_mix_bwd_weight", (N_POOL_GROUPS * N_SHARD, 1, T // tk),
               [(p, _spec((tk, rq), lambda j, n, k: (k, j)))],
               [(dys, _spec((tk, C), lambda j, n, k: (k, j // N_SHARD)))],
               [(0, 0, 0)], TN, [(rq, C)],
               [((N_SHARD, N_POOL_GROUPS * rq, C), BF16,
                 _spec((None, rq, C), lambda j, n, k: (j % N_SHARD, j // N_SHARD, 0)))])[0]
    dh = _pool_bwd(dp)
    dx, dxb, dg = _rmsnorm_bwd(x, g_mix, dh, dx1)
    return dx, dxb, dg, dscale, g_wp


def _conv_mix_fwd(x, g_mix, win4, taps4, wout4):
    D = x.shape[1]
    h = _rmsnorm_fwd(x, g_mix, BF16)
    bcx = _mm_col_fwd("conv_in", h, win4, 0, D // 4, F32, 3)[0]
    q = _conv_fwd(bcx, taps4)
    x1 = _mm_row_fwd("conv_out", q, wout4, 0, x)[0]
    return x1, (x, h, bcx, q)


def _conv_mix_bwd(dx1, dx1b, saved, g_mix, win4, taps4, wout4):
    x, h, bcx, q = saved
    D = x.shape[1]
    dq = _mm_row_bwd_data("conv_out_bwd_data", dx1b, wout4, 0, out_dtype=F32)[0]
    g_wout = _mm_row_bwd_weight("conv_out_bwd_weight", q, dx1b, wout4.shape[2])
    dbcx, dtaps = _conv_bwd(dq, bcx, taps4)
    dh = _mm_col_bwd_data("conv_in_bwd_data", [dbcx], [win4], 0, D // 4, 3)
    g_win = _mm_col_bwd_weight("conv_in_bwd_weight", h, [dbcx], win4.shape[3], D // 4, 3)[0]
    dx, dxb, dg = _rmsnorm_bwd(x, g_mix, dh, dx1)
    return dx, dxb, dg, dtaps, g_wout, g_win


def _rows2d(w):
    return w.reshape(-1, w.shape[-1])


def _pad_rows(v, rows):
    return jnp.pad(v, ((0, rows - v.shape[0]), (0, 0)))


def kernel(x, norm_mix_g, norm_ffn_g, sb_w_qkv, sb_g_q, sb_g_k, sb_w_o, pool_w, pool_scale, conv_w_in, conv_w, conv_w_out, ffn_w_gate, ffn_w_up, ffn_w_down, loss_target, m_norm_mix_g, m_norm_ffn_g, m_sb_w_qkv, m_sb_g_q, m_sb_g_k, m_sb_w_o, m_pool_w, m_pool_scale, m_conv_w_in, m_conv_w, m_conv_w_out, m_ffn_w_gate, m_ffn_w_up, m_ffn_w_down, v_norm_mix_g, v_norm_ffn_g, v_sb_w_qkv, v_sb_g_q, v_sb_g_k, v_sb_w_o, v_pool_w, v_pool_scale, v_conv_w_in, v_conv_w, v_conv_w_out, v_ffn_w_gate, v_ffn_w_up, v_ffn_w_down):
    T, D = x.shape[1], x.shape[2]
    depth = norm_mix_g.shape[0]
    big = dict(sb_w_qkv=(sb_w_qkv, m_sb_w_qkv, v_sb_w_qkv), sb_w_o=(sb_w_o, m_sb_w_o, v_sb_w_o),
               pool_w=(pool_w, m_pool_w, v_pool_w), conv_w_in=(conv_w_in, m_conv_w_in, v_conv_w_in),
               conv_w_out=(conv_w_out, m_conv_w_out, v_conv_w_out), ffn_w_gate=(ffn_w_gate, m_ffn_w_gate, v_ffn_w_gate),
               ffn_w_up=(ffn_w_up, m_ffn_w_up, v_ffn_w_up), ffn_w_down=(ffn_w_down, m_ffn_w_down, v_ffn_w_down))
    names = list(big)

    cs_conv = conv_w.shape[2]
    taps_local = _pad_rows(conv_w[0], 16)
    gathered = _allgather_shards([_cast_into_slot(_rows2d(big[n][0]), BF16) for n in names]
                                 + [_cast_into_slot(taps_local, F32)])
    w4 = {}
    for n, g4 in zip(names, gathered[:-1]):
        layers = big[n][0].shape[0]
        w4[n] = g4.reshape(N_SHARD, layers, g4.shape[1] // layers, g4.shape[2])
    taps4 = gathered[-1]

    xs = x.reshape(T, D)
    saved = []
    for i in range(depth):
        kind, j = i % 3, i // 3
        g_mix = norm_mix_g[i:i + 1]
        if kind == 0:
            xs, sv = _sb_fwd(xs, g_mix, sb_g_q[j:j + 1], sb_g_k[j:j + 1], w4["sb_w_qkv"], w4["sb_w_o"], j)
        elif kind == 1:
            xs, sv = _pool_mix_fwd(xs, g_mix, w4["pool_w"], pool_scale[j:j + 1])
        else:
            xs, sv = _conv_mix_fwd(xs, g_mix, w4["conv_w_in"], taps4, w4["conv_w_out"])
        xs, sf = _ffn_fwd(xs, norm_ffn_g[i:i + 1], w4["ffn_w_gate"], w4["ffn_w_up"], w4["ffn_w_down"], i)
        saved.append((sv, sf))

    dxs, dxb, err2 = _loss_head(xs, loss_target.reshape(T, D))
    loss = lax.psum(0.5 * jnp.sum(err2) / D, ("x", "y", "c"))
    grads, layout = [], []
    small = {}
    for i in reversed(range(depth)):
        kind, j = i % 3, i // 3
        sv, sf = saved[i]
        dxs, dxb, dg, g_gate, g_up, g_down = _ffn_bwd(dxs, dxb, sf, norm_ffn_g[i:i + 1], w4["ffn_w_gate"],
                                                      w4["ffn_w_up"], w4["ffn_w_down"], i)
        small[("norm_ffn_g", i)] = dg
        grads += [g_gate, g_up, g_down]
        layout += [("ffn_w_gate", i), ("ffn_w_up", i), ("ffn_w_down", i)]
        g_mix = norm_mix_g[i:i + 1]
        if kind == 0:
            dxs, dxb, dg, dgq, dgk, g_wqkv, g_wo = _sb_bwd(dxs, dxb, sv, g_mix, sb_g_q[j:j + 1], sb_g_k[j:j + 1],
                                                          w4["sb_w_qkv"], w4["sb_w_o"], j)
            small[("sb_g_q", j)], small[("sb_g_k", j)] = dgq, dgk
            grads += [g_wqkv, g_wo]
            layout += [("sb_w_qkv", j), ("sb_w_o", j)]
        elif kind == 1:
            dxs, dxb, dg, dscale, g_wp = _pool_mix_bwd(dxs, dxb, sv, g_mix, w4["pool_w"], pool_scale[j:j + 1])
            small[("pool_scale", j)] = dscale
            grads += [g_wp]
            layout += [("pool_w", j)]
        else:
            dxs, dxb, dg, dtaps, g_wout, g_win = _conv_mix_bwd(dxs, dxb, sv, g_mix, w4["conv_w_in"], taps4,
                                                              w4["conv_w_out"])
            small[("conv_w", j)] = dtaps
            grads += [g_win, g_wout]
            layout += [("conv_w_in", j), ("conv_w_out", j)]
        small[("norm_mix_g", i)] = dg
    grad_x = dxs.reshape(x.shape)

    tensors = sorted(set(t for t, _ in layout))
    received = dict(zip(tensors, _exchange_grads(grads, layout)))
    full = {}
    for g, (t, l) in zip(grads, layout):
        full[t] = _sum_into(received[t], g, l, full.get(t))
    g_big = dict(zip(tensors, _share_halves([full[t] for t in tensors])))

    n_sb = sb_g_q.shape[0]
    gqk = jnp.concatenate([small[(n, j)] for j in range(n_sb) for n in ("sb_g_q", "sb_g_k")], axis=1)
    dtaps = small[("conv_w", 0)]
    taps_full = jnp.concatenate([dtaps[s, :3] for s in range(N_SHARD)], axis=1)
    pack = jnp.concatenate(
        [small[("norm_mix_g", i)] for i in range(depth)] + [small[("norm_ffn_g", i)] for i in range(depth)]
        + [small[("pool_scale", 0)], jnp.pad(gqk, ((0, 0), (0, D - gqk.shape[1]))), taps_full], axis=0)
    pack = _pad_rows(pack, SMALL_ROWS)
    g_small = _sum_devices(_allgather_small(pack))
    mine = 2 * lax.axis_index("x") + lax.axis_index("y")
    g_taps = lax.dynamic_slice(g_small, (2 * depth + 2, mine * cs_conv), (3, cs_conv))

    def pack_small(norm_mix, norm_ffn, scale, gq, gk, taps):
        qk = jnp.concatenate([v[j:j + 1] for j in range(n_sb) for v in (gq, gk)], axis=1)
        rows = jnp.concatenate([norm_mix, norm_ffn, scale, jnp.pad(qk, ((0, 0), (0, D - qk.shape[1]))),
                                jnp.pad(taps[0], ((0, 0), (0, D - cs_conv)))], axis=0)
        return _pad_rows(rows, SMALL_ROWS)

    g_pack = jnp.concatenate([g_small[:2 * depth + 2], jnp.pad(g_taps, ((0, 0), (0, D - cs_conv))),
                              jnp.zeros((SMALL_ROWS - 2 * depth - 5, D), F32)], axis=0)
    w_pack = pack_small(norm_mix_g, norm_ffn_g, pool_scale, sb_g_q, sb_g_k, conv_w)
    m_pack = pack_small(m_norm_mix_g, m_norm_ffn_g, m_pool_scale, m_sb_g_q, m_sb_g_k, m_conv_w)
    v_pack = pack_small(v_norm_mix_g, v_norm_ffn_g, v_pool_scale, v_sb_g_q, v_sb_g_k, v_conv_w)
    small_out = (g_pack,) + tuple(_adamw(w_pack, g_pack, m_pack, v_pack))

    def unpack_small(p):
        qk = p[2 * depth + 1]
        gq = jnp.stack([qk[(2 * j) * HEAD_DIM:(2 * j + 1) * HEAD_DIM] for j in range(n_sb)])
        gk = jnp.stack([qk[(2 * j + 1) * HEAD_DIM:(2 * j + 2) * HEAD_DIM] for j in range(n_sb)])
        return dict(norm_mix_g=p[:depth], norm_ffn_g=p[depth:2 * depth], pool_scale=p[2 * depth:2 * depth + 1],
                    sb_g_q=gq, sb_g_k=gk, conv_w=p[2 * depth + 2:2 * depth + 5, :cs_conv][None])

    results = [unpack_small(p) for p in small_out]
    for n in names:
        w, m, v = big[n]
        g = _rows2d(g_big[n])
        outs = (g,) + tuple(_adamw(_rows2d(w), g, _rows2d(m), _rows2d(v)))
        for r, o in zip(results, outs):
            r[n] = o.reshape(w.shape)

    order = ["norm_mix_g", "norm_ffn_g", "sb_w_qkv", "sb_g_q", "sb_g_k", "sb_w_o", "pool_w", "pool_scale",
             "conv_w_in", "conv_w", "conv_w_out", "ffn_w_gate", "ffn_w_up", "ffn_w_down"]
    return (loss, grad_x) + tuple(r[n] for r in results for n in order)
```

```python
import functools

import jax
import jax.numpy as jnp
from jax import lax
from jax.experimental import pallas as pl
from jax.experimental.pallas import tpu as pltpu

F32 = jnp.float32
BF16 = jnp.bfloat16

HEAD_DIM = 128
N_POOL_GROUPS = 4
EPS = 1e-6
N_SHARD = 4
N_DEV = 8
VMEM_LIMIT_BYTES = 56 * 2**20
KEY_BLOCK = 128
ATTN_ROWS = 512
ATTN_GROUP_FWD = 4
ATTN_GROUP_BWD = 2
SMALL_ROWS = 16

ADAM_LR = 0.001
ADAM_B1 = 0.9
ADAM_B2 = 0.999
ADAM_EPS = 1e-08
ADAM_WD = 0.01
ADAM_STEP = 10

MESH = pl.DeviceIdType.MESH
ANY = pl.BlockSpec(memory_space=pl.ANY)


def _params(*sem):
    return pltpu.CompilerParams(dimension_semantics=sem, vmem_limit_bytes=VMEM_LIMIT_BYTES)


def _pick(n, pref, unit):
    t = (min(pref, n) // unit) * unit
    while n % t:
        t -= unit
    return t


NN = ((1,), (0,))
NT = ((1,), (1,))
TN = ((0,), (0,))


def _mm(name, grid, a_ops, b_ops, dots, dims, acc_shapes, outs, epi=None, extras=(), after=()):
    na, nb, ne, no, nacc = len(a_ops), len(b_ops), len(extras), len(outs), len(acc_shapes)
    nk = grid[2]
    n_in = na + nb + ne + len(after)

    def body(*refs):
        a_refs, b_refs = refs[:na], refs[na:na + nb]
        e_refs = refs[na + nb:na + nb + ne]
        o_refs = refs[n_in:n_in + no]
        acc_refs = refs[n_in + no:]

        def partial_sums():
            sums = [None] * nacc
            for ai, bi, ci in dots:
                d = lax.dot_general(a_refs[ai][...].astype(BF16), b_refs[bi][...].astype(BF16),
                                    (dims, ((), ())), preferred_element_type=F32)
                sums[ci] = d if sums[ci] is None else sums[ci] + d
            return sums

        def finish(accs):
            res = epi(accs, [e[...] for e in e_refs]) if epi is not None else accs
            for o, r in zip(o_refs, res):
                o[...] = r.astype(o.dtype)

        if nk == 1:
            finish(partial_sums())
            return
        k = pl.program_id(2)

        @pl.when(k == 0)
        def _():
            for acc, s in zip(acc_refs, partial_sums()):
                acc[...] = s

        @pl.when(k > 0)
        def _():
            for acc, s in zip(acc_refs, partial_sums()):
                acc[...] += s

        @pl.when(k == nk - 1)
        def _():
            finish([acc[...] for acc in acc_refs])

    ops = list(a_ops) + list(b_ops) + list(extras) + [(t, ANY) for t in after]
    return pl.pallas_call(
        body, name=name, grid=grid,
        in_specs=[s for _, s in ops],
        out_specs=[s for _, _, s in outs],
        out_shape=[jax.ShapeDtypeStruct(sh, dt) for sh, dt, _ in outs],
        scratch_shapes=[pltpu.VMEM(s, F32) for s in acc_shapes] if nk > 1 else [],
        compiler_params=_params("parallel", "parallel", "arbitrary"),
    )(*[a for a, _ in ops])


def _spec(block, index):
    return pl.BlockSpec(block, index)


def _rmsnorm_fwd(x, g, out_dtype):
    T, D = x.shape
    tm = _pick(T, 256, 8)

    def body(x_ref, g_ref, o_ref):
        xv = x_ref[...]
        r = lax.rsqrt(jnp.mean(xv * xv, axis=-1, keepdims=True) + EPS)
        o_ref[...] = (xv * r * g_ref[...]).astype(o_ref.dtype)

    row = _spec((tm, D), lambda i: (i, 0))
    return pl.pallas_call(
        body, name="rmsnorm_fwd", grid=(T // tm,),
        in_specs=[row, _spec((1, D), lambda i: (0, 0))], out_specs=row,
        out_shape=jax.ShapeDtypeStruct((T, D), out_dtype), compiler_params=_params("parallel"),
    )(x, g)


def _rmsnorm_bwd(x, g, dh, dres):
    T, D = x.shape
    tm = _pick(T, 256, 8)

    def body(x_ref, g_ref, dh_ref, dres_ref, dx_ref, dxb_ref, dg_ref):
        xv = x_ref[...]
        dhv = dh_ref[...].astype(F32)
        r = lax.rsqrt(jnp.mean(xv * xv, axis=-1, keepdims=True) + EPS)
        xh = xv * r
        dxh = dhv * g_ref[...]
        dx = r * (dxh - xh * jnp.mean(dxh * xh, axis=-1, keepdims=True)) + dres_ref[...]
        dx_ref[...] = dx
        dxb_ref[...] = dx.astype(BF16)
        part = jnp.sum(dhv * xh, axis=0, keepdims=True)

        @pl.when(pl.program_id(0) == 0)
        def _():
            dg_ref[...] = part

        @pl.when(pl.program_id(0) > 0)
        def _():
            dg_ref[...] += part

    row = _spec((tm, D), lambda i: (i, 0))
    vec = _spec((1, D), lambda i: (0, 0))
    return pl.pallas_call(
        body, name="rmsnorm_bwd", grid=(T // tm,),
        in_specs=[row, vec, row, row], out_specs=[row, row, vec],
        out_shape=[jax.ShapeDtypeStruct((T, D), F32), jax.ShapeDtypeStruct((T, D), BF16),
                   jax.ShapeDtypeStruct((1, D), F32)],
        compiler_params=_params("arbitrary"),
    )(x, g, dh, dres)


def _loss_head(y, target):
    T, D = y.shape
    tm = _pick(T, 256, 8)

    def body(y_ref, t_ref, dy_ref, dyb_ref, l_ref):
        err = y_ref[...] - t_ref[...]
        dy = err * (1.0 / D)
        dy_ref[...] = dy
        dyb_ref[...] = dy.astype(BF16)
        part = jnp.sum(err * err, axis=0, keepdims=True)

        @pl.when(pl.program_id(0) == 0)
        def _():
            l_ref[...] = part

        @pl.when(pl.program_id(0) > 0)
        def _():
            l_ref[...] += part

    row = _spec((tm, D), lambda i: (i, 0))
    vec = _spec((1, D), lambda i: (0, 0))
    return pl.pallas_call(
        body, name="loss_head", grid=(T // tm,),
        in_specs=[row, row], out_specs=[row, row, vec],
        out_shape=[jax.ShapeDtypeStruct((T, D), F32), jax.ShapeDtypeStruct((T, D), BF16),
                   jax.ShapeDtypeStruct((1, D), F32)],
        compiler_params=_params("arbitrary"),
    )(y, target)


def _headnorm_fwd(qkv, gq, gk):
    _, T, D = qkv.shape
    H = D // HEAD_DIM
    tm = _pick(T, 256, 8)
    scale = HEAD_DIM ** -0.5

    def body(q_ref, k_ref, v_ref, gq_ref, gk_ref, qs_ref, kn_ref, vb_ref):
        for h in range(H):
            sl = slice(h * HEAD_DIM, (h + 1) * HEAD_DIM)
            q = q_ref[:, sl]
            qs_ref[:, sl] = (q * lax.rsqrt(jnp.mean(q * q, axis=-1, keepdims=True) + EPS)
                             * (gq_ref[...] * scale)).astype(BF16)
            k = k_ref[:, sl]
            kn_ref[:, sl] = (k * lax.rsqrt(jnp.mean(k * k, axis=-1, keepdims=True) + EPS)
                             * gk_ref[...]).astype(BF16)
        vb_ref[...] = v_ref[...].astype(BF16)

    part = lambda p: _spec((None, tm, D), lambda i, p=p: (p, i, 0))
    row = _spec((tm, D), lambda i: (i, 0))
    vec = _spec((1, HEAD_DIM), lambda i: (0, 0))
    return pl.pallas_call(
        body, name="headnorm_fwd", grid=(T // tm,),
        in_specs=[part(0), part(1), part(2), vec, vec], out_specs=[row, row, row],
        out_shape=[jax.ShapeDtypeStruct((T, D), BF16)] * 3, compiler_params=_params("parallel"),
    )(qkv, qkv, qkv, gq, gk)


def _headnorm_bwd(qkv, gq, gk, dqs, dkn, dv):
    _, T, D = qkv.shape
    H = D // HEAD_DIM
    tm = _pick(T, 256, 8)
    scale = HEAD_DIM ** -0.5

    def body(q_ref, k_ref, gq_ref, gk_ref, dqs_ref, dkn_ref, dv_ref, dqkv_ref, dgq_ref, dgk_ref):
        dgq = jnp.zeros((1, HEAD_DIM), F32)
        dgk = jnp.zeros((1, HEAD_DIM), F32)
        for h in range(H):
            sl = slice(h * HEAD_DIM, (h + 1) * HEAD_DIM)
            for src, dsrc, g_ref, sc, p in ((q_ref, dqs_ref, gq_ref, scale, 0), (k_ref, dkn_ref, gk_ref, 1.0, 1)):
                v = src[:, sl]
                r = lax.rsqrt(jnp.mean(v * v, axis=-1, keepdims=True) + EPS)
                vh = v * r
                dn = dsrc[:, sl] * sc
                dvh = dn * g_ref[...]
                dqkv_ref[p, :, sl] = (r * (dvh - vh * jnp.mean(dvh * vh, axis=-1, keepdims=True))).astype(BF16)
                dg = jnp.sum(dn * vh, axis=0, keepdims=True)
                if p == 0:
                    dgq = dgq + dg
                else:
                    dgk = dgk + dg
        dqkv_ref[2] = dv_ref[...].astype(BF16)

        @pl.when(pl.program_id(0) == 0)
        def _():
            dgq_ref[...] = dgq
            dgk_ref[...] = dgk

        @pl.when(pl.program_id(0) > 0)
        def _():
            dgq_ref[...] += dgq
            dgk_ref[...] += dgk

    part = lambda p: _spec((None, tm, D), lambda i, p=p: (p, i, 0))
    row = _spec((tm, D), lambda i: (i, 0))
    vec = _spec((1, HEAD_DIM), lambda i: (0, 0))
    return pl.pallas_call(
        body, name="headnorm_bwd", grid=(T // tm,),
        in_specs=[part(0), part(1), vec, vec, row, row, row],
        out_specs=[_spec((3, tm, D), lambda i: (0, i, 0)), vec, vec],
        out_shape=[jax.ShapeDtypeStruct((3, T, D), BF16), jax.ShapeDtypeStruct((1, HEAD_DIM), F32),
                   jax.ShapeDtypeStruct((1, HEAD_DIM), F32)],
        compiler_params=_params("arbitrary"),
    )(qkv, qkv, gq, gk, dqs, dkn, dv)


def _sum_matrix(prefix):
    r = lax.broadcasted_iota(jnp.int32, (KEY_BLOCK, 2 * KEY_BLOCK), 0)
    c = lax.broadcasted_iota(jnp.int32, (KEY_BLOCK, 2 * KEY_BLOCK), 1)
    tri = (r <= c) if prefix else (r > c)
    return jnp.where(tri | (c >= KEY_BLOCK), 1.0, 0.0).astype(BF16)


def _block_sums(v, u):
    hi = v.astype(BF16)
    lo = (v - hi.astype(F32)).astype(BF16)
    s = (jnp.dot(hi, u, preferred_element_type=F32) + jnp.dot(lo, u, preferred_element_type=F32))
    return s[:, :KEY_BLOCK], s[:, KEY_BLOCK:]


def _log_terms(q, kj, mask):
    z = lax.dot_general(q, kj, (NT, ((), ())), preferred_element_type=F32)
    ls = jnp.minimum(z, 0.0) - jnp.log(1.0 + jnp.exp(-jnp.abs(z)))
    lk = ls - z
    if mask is not None:
        lk = jnp.where(mask, lk, 0.0)
    return ls, lk


def _causal_mask(row0, key0, tq):
    t = row0 + lax.broadcasted_iota(jnp.int32, (tq, KEY_BLOCK), 0)
    s = key0 + lax.broadcasted_iota(jnp.int32, (tq, KEY_BLOCK), 1)
    return s < t


def _attn_fwd(qs, kn, vb):
    T, D = qs.shape
    H = D // HEAD_DIM
    ATTN_GROUP = ATTN_GROUP_FWD
    tq = _pick(T, ATTN_ROWS, ATTN_GROUP * KEY_BLOCK)
    nd = tq // KEY_BLOCK

    def body(q_ref, k_ref, v_ref, u_ref, o_ref, tot_ref, acc_ref, run_ref):
        i = pl.program_id(1)
        q = q_ref[...]
        uv = u_ref[...]
        acc_ref[...] = jnp.zeros_like(acc_ref)
        run_ref[...] = jnp.zeros_like(run_ref)

        def group(first, masked):
            parts = []
            for n in range(ATTN_GROUP):
                k0 = pl.multiple_of((first - n) * KEY_BLOCK, KEY_BLOCK)
                mask = _causal_mask(i * tq, k0, tq) if masked else None
                ls, lk = _log_terms(q, k_ref[pl.ds(k0, KEY_BLOCK), :], mask)
                after, rows = _block_sums(lk, uv)
                parts.append((k0, mask, ls + after, rows))
            run = run_ref[...]
            acc = acc_ref[...]
            for k0, mask, base, rows in parts:
                a = jnp.exp(base + run)
                if masked:
                    a = jnp.where(mask, a, 0.0)
                acc = acc + jnp.dot(a.astype(BF16), v_ref[pl.ds(k0, KEY_BLOCK), :], preferred_element_type=F32)
                run = run + rows
            acc_ref[...] = acc
            run_ref[...] = run

        for d in range(nd // ATTN_GROUP):
            group(i * nd + nd - 1 - d * ATTN_GROUP, True)

        def below(n, c):
            group(i * nd - 1 - n * ATTN_GROUP, False)
            return c

        lax.fori_loop(0, i * (nd // ATTN_GROUP), below, 0)
        o_ref[...] = acc_ref[...].astype(o_ref.dtype)
        tot_ref[...] = run_ref[...]

    blk = _spec((tq, HEAD_DIM), lambda h, i: (i, h))
    col = _spec((T, HEAD_DIM), lambda h, i: (0, h))
    return pl.pallas_call(
        body, name="attn_fwd", grid=(H, T // tq),
        in_specs=[blk, col, col, _spec((KEY_BLOCK, 2 * KEY_BLOCK), lambda h, i: (0, 0))], out_specs=[blk, blk],
        out_shape=[jax.ShapeDtypeStruct((T, D), BF16), jax.ShapeDtypeStruct((T, D), F32)],
        scratch_shapes=[pltpu.VMEM((tq, HEAD_DIM), F32), pltpu.VMEM((tq, KEY_BLOCK), F32)],
        compiler_params=_params("parallel", "arbitrary"),
    )(qs, kn, vb, _sum_matrix(False))


def _attn_bwd(qs, kn, vb, tot, do):
    T, D = qs.shape
    H = D // HEAD_DIM
    ATTN_GROUP = ATTN_GROUP_BWD
    tq = _pick(T, ATTN_ROWS, ATTN_GROUP * KEY_BLOCK)
    nd = tq // KEY_BLOCK

    def body(q_ref, k_ref, v_ref, tot_ref, do_ref, u_ref, dq_ref, dk_ref, dv_ref, run_ref, grun_ref):
        i = pl.program_id(1)

        @pl.when(i == 0)
        def _():
            dk_ref[...] = jnp.zeros_like(dk_ref)
            dv_ref[...] = jnp.zeros_like(dv_ref)

        q = q_ref[...]
        dov = do_ref[...]
        uv = u_ref[...]
        dq_ref[...] = jnp.zeros_like(dq_ref)
        run_ref[...] = jnp.zeros_like(run_ref)
        grun_ref[...] = jnp.zeros_like(grun_ref)

        def group(first, masked):
            parts = []
            for n in range(ATTN_GROUP):
                k0 = pl.multiple_of((first + n) * KEY_BLOCK, KEY_BLOCK)
                keys = pl.ds(k0, KEY_BLOCK)
                mask = _causal_mask(i * tq, k0, tq) if masked else None
                kj = k_ref[keys, :]
                ls, lk = _log_terms(q, kj, mask)
                upto, rows = _block_sums(lk, uv)
                da = lax.dot_general(dov, v_ref[keys, :], (NT, ((), ())), preferred_element_type=F32)
                parts.append((keys, mask, kj, ls, tot_ref[...] - upto, rows, da))
            run = run_ref[...]
            grun = grun_ref[...]
            dqa = dq_ref[...]
            for keys, mask, kj, ls, right, rows, da in parts:
                a = jnp.exp(ls + (right - run))
                if masked:
                    a = jnp.where(mask, a, 0.0)
                g = a * da
                gupto, grows = _block_sums(g, uv)
                dz = g - jnp.exp(ls) * (grun + gupto)
                if masked:
                    dz = jnp.where(mask, dz, 0.0)
                dzb = dz.astype(BF16)
                dqa = dqa + jnp.dot(dzb, kj, preferred_element_type=F32)
                dk_ref[keys, :] += lax.dot_general(dzb, q, (TN, ((), ())), preferred_element_type=F32)
                dv_ref[keys, :] += lax.dot_general(a.astype(BF16), dov, (TN, ((), ())), preferred_element_type=F32)
                run = run + rows
                grun = grun + grows
            run_ref[...] = run
            grun_ref[...] = grun
            dq_ref[...] = dqa

        def below(n, c):
            group(n * ATTN_GROUP, False)
            return c

        lax.fori_loop(0, i * (nd // ATTN_GROUP), below, 0)
        for d in range(nd // ATTN_GROUP):
            group(i * nd + d * ATTN_GROUP, True)

    blk = _spec((tq, HEAD_DIM), lambda h, i: (i, h))
    col = _spec((T, HEAD_DIM), lambda h, i: (0, h))
    return pl.pallas_call(
        body, name="attn_bwd", grid=(H, T // tq),
        in_specs=[blk, col, col, blk, blk, _spec((KEY_BLOCK, 2 * KEY_BLOCK), lambda h, i: (0, 0))],
        out_specs=[blk, col, col],
        out_shape=[jax.ShapeDtypeStruct((T, D), F32)] * 3,
        scratch_shapes=[pltpu.VMEM((tq, KEY_BLOCK), F32), pltpu.VMEM((tq, KEY_BLOCK), F32)],
        compiler_params=_params("parallel", "arbitrary"),
    )(qs, kn, vb, tot, do, _sum_matrix(True))


def _shift_down(v, n):
    t = lax.broadcasted_iota(jnp.int32, v.shape, 0)
    return jnp.where(t >= n, pltpu.roll(v, n, 0), 0.0)


def _shift_up(v, n):
    rows = v.shape[0]
    t = lax.broadcasted_iota(jnp.int32, v.shape, 0)
    return jnp.where(t < rows - n, pltpu.roll(v, rows - n, 0), 0.0)


def _pool_window(j, cw, D):
    group = (j * cw) // (D // N_POOL_GROUPS)
    return jnp.left_shift(2, group)


def _pool_count(shape, w):
    t = lax.broadcasted_iota(jnp.int32, shape, 0)
    return jnp.minimum(t + 1, w).astype(F32)


def _pool_fwd(h):
    T, D = h.shape
    cw = min(256, D // N_POOL_GROUPS)

    def body(h_ref, p_ref):
        w = _pool_window(pl.program_id(0), cw, D)
        hv = h_ref[...]
        s = hv
        for n in (1, 2, 4, 8):
            s = jnp.where(n < w, s + _shift_down(s, n), s)
        p_ref[...] = (s / _pool_count(hv.shape, w) - hv).astype(p_ref.dtype)

    slab = _spec((T, cw), lambda j: (0, j))
    return pl.pallas_call(
        body, name="pool_fwd", grid=(D // cw,), in_specs=[slab], out_specs=slab,
        out_shape=jax.ShapeDtypeStruct((T, D), BF16), compiler_params=_params("parallel"),
    )(h)


def _pool_bwd(dp):
    T, D = dp.shape
    cw = min(256, D // N_POOL_GROUPS)

    def body(dp_ref, dh_ref):
        w = _pool_window(pl.program_id(0), cw, D)
        dpv = dp_ref[...]
        s = dpv / _pool_count(dpv.shape, w)
        for n in (1, 2, 4, 8):
            s = jnp.where(n < w, s + _shift_up(s, n), s)
        dh_ref[...] = (s - dpv).astype(dh_ref.dtype)

    slab = _spec((T, cw), lambda j: (0, j))
    return pl.pallas_call(
        body, name="pool_bwd", grid=(D // cw,), in_specs=[slab], out_specs=slab,
        out_shape=jax.ShapeDtypeStruct((T, D), F32), compiler_params=_params("parallel"),
    )(dp)


def _pool_scale_bwd(dx, ypre, scale):
    T, D = dx.shape
    tm = _pick(T, 256, 8)

    def body(dx_ref, y_ref, s_ref, dys_ref, ds_ref):
        dxv = dx_ref[...]
        dys_ref[...] = (dxv * s_ref[...]).astype(BF16)
        part = jnp.sum(dxv * y_ref[...].astype(F32), axis=0, keepdims=True)

        @pl.when(pl.program_id(0) == 0)
        def _():
            ds_ref[...] = part

        @pl.when(pl.program_id(0) > 0)
        def _():
            ds_ref[...] += part

    row = _spec((tm, D), lambda i: (i, 0))
    vec = _spec((1, D), lambda i: (0, 0))
    return pl.pallas_call(
        body, name="pool_scale_bwd", grid=(T // tm,), in_specs=[row, row, vec], out_specs=[row, vec],
        out_shape=[jax.ShapeDtypeStruct((T, D), BF16), jax.ShapeDtypeStruct((1, D), F32)],
        compiler_params=_params("arbitrary"),
    )(dx, ypre, scale)


def _conv_specs(T, D, cw, cs):
    part = lambda p: _spec((None, T, cw), lambda j, p=p: (p, 0, j))
    taps = _spec((None, 8, cw), lambda j: (j // (cs // cw), 0, j % (cs // cw)))
    return part, taps


def _conv_fwd(bcx, taps4):
    _, T, D = bcx.shape
    cs = taps4.shape[2]
    cw = min(128, cs)
    part, taps = _conv_specs(T, D, cw, cs)

    def body(b_ref, c_ref, u_ref, w_ref, q_ref):
        g = c_ref[...].astype(F32) * u_ref[...].astype(F32)
        w = w_ref[...]
        y = w[2:3] * g + w[1:2] * _shift_down(g, 1) + w[0:1] * _shift_down(g, 2)
        q_ref[...] = (b_ref[...].astype(F32) * y).astype(q_ref.dtype)

    return pl.pallas_call(
        body, name="conv_fwd", grid=(D // cw,), in_specs=[part(0), part(1), part(2), taps],
        out_specs=_spec((T, cw), lambda j: (0, j)),
        out_shape=jax.ShapeDtypeStruct((T, D), BF16), compiler_params=_params("parallel"),
    )(bcx, bcx, bcx, taps4)


def _conv_bwd(dq, bcx, taps4):
    _, T, D = bcx.shape
    cs = taps4.shape[2]
    cw = min(128, cs)
    part, taps = _conv_specs(T, D, cw, cs)

    def body(dq_ref, b_ref, c_ref, u_ref, w_ref, d_ref, dw_ref):
        b = b_ref[...].astype(F32)
        c = c_ref[...].astype(F32)
        uu = u_ref[...].astype(F32)
        dqv = dq_ref[...].astype(F32)
        w = w_ref[...]
        g = c * uu
        g1 = _shift_down(g, 1)
        g2 = _shift_down(g, 2)
        d_ref[0] = (dqv * (w[2:3] * g + w[1:2] * g1 + w[0:1] * g2)).astype(BF16)
        dy = dqv * b
        dg = w[2:3] * dy + w[1:2] * _shift_up(dy, 1) + w[0:1] * _shift_up(dy, 2)
        d_ref[1] = (dg * uu).astype(BF16)
        d_ref[2] = (dg * c).astype(BF16)
        dw_ref[0:1, :] = jnp.sum(dy * g2, axis=0, keepdims=True)
        dw_ref[1:2, :] = jnp.sum(dy * g1, axis=0, keepdims=True)
        dw_ref[2:3, :] = jnp.sum(dy * g, axis=0, keepdims=True)
        dw_ref[3:8, :] = jnp.zeros((5, cw), F32)

    return pl.pallas_call(
        body, name="conv_bwd", grid=(D // cw,),
        in_specs=[_spec((T, cw), lambda j: (0, j)), part(0), part(1), part(2), taps],
        out_specs=[_spec((3, T, cw), lambda j: (0, 0, j)), taps],
        out_shape=[jax.ShapeDtypeStruct((3, T, D), BF16), jax.ShapeDtypeStruct((N_SHARD, 8, cs), F32)],
        compiler_params=_params("parallel"),
    )(dq, bcx, bcx, bcx, taps4)


def _quarter():
    return 2 * lax.axis_index("x") + lax.axis_index("y")


def _cast_into_slot(w, dtype):
    R, C = w.shape
    tr = _pick(R, 512, 8)

    def body(w_ref, o_ref):
        o_ref[...] = w_ref[...].astype(o_ref.dtype)

    return pl.pallas_call(
        body, name="cast_into_slot", grid=(R // tr,),
        in_specs=[_spec((tr, C), lambda i: (i, 0))],
        out_specs=_spec((None, tr, C), lambda i: (_quarter(), i, 0)),
        out_shape=jax.ShapeDtypeStruct((N_SHARD, R, C), dtype), compiler_params=_params("parallel"),
    )(w)


def _sum_into(recv, own, l, L, prev):
    n, R2, C = recv.shape
    tr = _pick(R2, 256, 8)
    nb = R2 // tr

    def body(r_ref, o_ref, *rest):
        s = o_ref[...].astype(F32)
        for k in range(n):
            s = s + r_ref[k].astype(F32)
        rest[-1][...] = s

    in_specs = [_spec((n, tr, C), lambda i: (0, i, 0)),
                _spec((None, tr, C), lambda i: (_quarter(), lax.axis_index("c") * nb + i, 0))]
    args = [recv, own]
    if prev is not None:
        in_specs.append(ANY)
        args.append(prev)
    return pl.pallas_call(
        body, name="sum_into", grid=(nb,), in_specs=in_specs,
        out_specs=_spec((None, tr, C), lambda i: (l, lax.axis_index("c") * nb + i, 0)),
        out_shape=jax.ShapeDtypeStruct((L, 2 * R2, C), F32),
        input_output_aliases={} if prev is None else {2: 0},
        compiler_params=_params("parallel"),
    )(*args)


def _sum_devices(parts):
    n, R, C = parts.shape
    tr = _pick(R, 256, 8)

    def body(p_ref, o_ref):
        s = p_ref[0].astype(F32)
        for d in range(1, n):
            s = s + p_ref[d].astype(F32)
        o_ref[...] = s

    return pl.pallas_call(
        body, name="sum_devices", grid=(R // tr,),
        in_specs=[_spec((n, tr, C), lambda i: (0, i, 0))], out_specs=_spec((tr, C), lambda i: (i, 0)),
        out_shape=jax.ShapeDtypeStruct((R, C), F32), compiler_params=_params("parallel"),
    )(parts)


def _adamw(w, g, m, v):
    R, C = w.shape
    tr = _pick(R, 256, 8)

    def body(w_ref, g_ref, m_ref, v_ref, d_ref, nm_ref, nv_ref):
        gv = g_ref[...]
        m2 = ADAM_B1 * m_ref[...] + (1.0 - ADAM_B1) * gv
        v2 = ADAM_B2 * v_ref[...] + (1.0 - ADAM_B2) * (gv * gv)
        m_hat = m2 / (1.0 - ADAM_B1 ** ADAM_STEP)
        v_hat = v2 / (1.0 - ADAM_B2 ** ADAM_STEP)
        d_ref[...] = -ADAM_LR * (m_hat / (jnp.sqrt(v_hat) + ADAM_EPS) + ADAM_WD * w_ref[...])
        nm_ref[...] = m2
        nv_ref[...] = v2

    row = _spec((tr, C), lambda i: (i, 0))
    return pl.pallas_call(
        body, name="adamw", grid=(R // tr,), in_specs=[row] * 4, out_specs=[row] * 3,
        out_shape=[jax.ShapeDtypeStruct((R, C), F32)] * 3, compiler_params=_params("parallel"),
    )(w, g, m, v)


def _place():
    return lax.axis_index("x"), lax.axis_index("y"), lax.axis_index("c")


def _half(ref_rows, c):
    return pl.ds(c * (ref_rows // 2), ref_rows // 2)


def _allgather_shards(bufs):
    n = len(bufs)

    def body(*refs):
        outs = refs[n:2 * n]
        send, recv = refs[2 * n:]
        x, y, c = _place()
        chips = [(1 - x, y), (x, 1 - y), (1 - x, 1 - y)]

        def copy(a, k, quarter, core, to):
            part = outs[a].at[quarter, _half(outs[a].shape[1], core)]
            return pltpu.make_async_remote_copy(
                src_ref=part, dst_ref=part, send_sem=send.at[a, k], recv_sem=recv.at[a, k],
                device_id=to, device_id_type=MESH)

        first, passed = [], []
        for a in range(n):
            for k, (cx, cy) in enumerate(chips):
                cp = copy(a, k, 2 * x + y, c, (cx, cy, c))
                cp.start()
                first.append(cp)
        for a in range(n):
            for k, (cx, cy) in enumerate(chips):
                copy(a, k, 2 * cx + cy, c, (x, y, c)).wait_recv()
                cp = copy(a, 3 + k, 2 * cx + cy, c, (x, y, 1 - c))
                cp.start()
                passed.append(cp)
        for a in range(n):
            for k, (cx, cy) in enumerate(chips):
                copy(a, 3 + k, 2 * cx + cy, 1 - c, (x, y, 1 - c)).wait_recv()
        for cp in first + passed:
            cp.wait_send()

    return pl.pallas_call(
        body, name="allgather_shards", in_specs=[ANY] * n, out_specs=[ANY] * n,
        out_shape=[jax.ShapeDtypeStruct(b.shape, b.dtype) for b in bufs],
        input_output_aliases={a: a for a in range(n)},
        scratch_shapes=[pltpu.SemaphoreType.DMA((n, 6)), pltpu.SemaphoreType.DMA((n, 6))],
        compiler_params=pltpu.CompilerParams(has_side_effects=True),
    )(*bufs)


def _flips():
    return [(fx, fy, fc) for fx in (0, 1) for fy in (0, 1) for fc in (0, 1) if (fx, fy, fc) != (0, 0, 0)]


HBM = pl.BlockSpec(memory_space=pltpu.HBM)
SEM = pl.BlockSpec(memory_space=pltpu.SEMAPHORE)
DATAFLOW = pltpu.SideEffectType.DATAFLOW_SIDE_EFFECTING


def _exchange_copies(grads, zones, send, recv):
    x, y, c = _place()
    copies = []
    for a in range(len(grads)):
        rows = grads[a].shape[1]
        for k, (fx, fy, fc) in enumerate(_flips()):
            px, py, pc = x ^ fx, y ^ fy, c ^ fc
            j = a * (N_DEV - 1) + k
            copies.append(pltpu.make_async_remote_copy(
                src_ref=grads[a].at[2 * px + py, _half(rows, pc)], dst_ref=zones[a].at[k],
                send_sem=send.at[j], recv_sem=recv.at[j], device_id=(px, py, pc), device_id_type=MESH))
    return copies


def _exchange_start(name, grads):
    n = len(grads)
    zones = [lax.empty((N_DEV - 1, g.shape[1] // 2, g.shape[2]), g.dtype) for g in grads]

    def body(*refs):
        for cp in _exchange_copies(refs[:n], refs[n:2 * n], refs[2 * n], refs[2 * n + 1]):
            cp.start()
        refs[-1][...] = jnp.zeros_like(refs[-1])

    outs = pl.pallas_call(
        body, name=name, in_specs=[HBM] * (2 * n),
        out_specs=(SEM, SEM) + (HBM,) * (2 * n) + (pl.BlockSpec(memory_space=pltpu.VMEM),),
        out_shape=(pltpu.SemaphoreType.DMA((n * (N_DEV - 1),)), pltpu.SemaphoreType.DMA((n * (N_DEV - 1),)))
        + tuple(pltpu.HBM(v.shape, v.dtype) for v in list(grads) + zones) + (jax.ShapeDtypeStruct((8, 128), F32),),
        input_output_aliases={a: 2 + a for a in range(2 * n)},
        compiler_params=pltpu.CompilerParams(has_side_effects=DATAFLOW),
    )(*[pltpu.with_memory_space_constraint(v, pltpu.HBM) for v in list(grads) + zones])
    return outs[:-1], outs[-1]


def _exchange_wait(name, started, after):
    send, recv = started[0], started[1]
    n = (len(started) - 2) // 2

    def body(*refs):
        for cp in _exchange_copies(refs[:n], refs[n:2 * n], refs[2 * n], refs[2 * n + 1]):
            cp.wait_send()
            cp.wait_recv()

    outs = pl.pallas_call(
        body, name=name, in_specs=[HBM] * (2 * n) + [SEM, SEM, ANY], out_specs=(HBM,) * (2 * n),
        out_shape=tuple(pltpu.HBM(v.shape, v.dtype) for v in started[2:]),
        input_output_aliases={a: a for a in range(2 * n)},
        compiler_params=pltpu.CompilerParams(has_side_effects=DATAFLOW),
    )(*started[2:], send, recv, after)
    return outs[:n], outs[n:]


def _share_halves(fulls):
    n = len(fulls)
    index = [(a, l) for a in range(n) for l in range(fulls[a].shape[0])]

    def body(*refs):
        outs = refs[n:2 * n]
        send, recv = refs[2 * n:]
        x, y, c = _place()

        def copy(j, core):
            a, l = index[j]
            part = outs[a].at[l, _half(outs[a].shape[1], core)]
            return pltpu.make_async_remote_copy(
                src_ref=part, dst_ref=part, send_sem=send.at[j], recv_sem=recv.at[j],
                device_id=(x, y, 1 - c), device_id_type=MESH)

        for j in range(len(index)):
            copy(j, c).start()
        for j in range(len(index)):
            copy(j, 1 - c).wait_recv()
        for j in range(len(index)):
            copy(j, c).wait_send()

    return pl.pallas_call(
        body, name="share_halves", in_specs=[ANY] * n, out_specs=[ANY] * n,
        out_shape=[jax.ShapeDtypeStruct(f.shape, f.dtype) for f in fulls],
        input_output_aliases={a: a for a in range(n)},
        scratch_shapes=[pltpu.SemaphoreType.DMA((len(index),)), pltpu.SemaphoreType.DMA((len(index),))],
        compiler_params=pltpu.CompilerParams(has_side_effects=True),
    )(*fulls)


def _allgather_small(v):
    def body(v_ref, o_ref, send, recv, lsem):
        x, y, c = _place()
        me = 4 * x + 2 * y + c
        own = pltpu.make_async_copy(v_ref, o_ref.at[me], lsem)
        own.start()
        sends = []
        for k, (fx, fy, fc) in enumerate(_flips()):
            cp = pltpu.make_async_remote_copy(
                src_ref=v_ref, dst_ref=o_ref.at[me], send_sem=send.at[k], recv_sem=recv.at[k],
                device_id=(x ^ fx, y ^ fy, c ^ fc), device_id_type=MESH)
            cp.start()
            sends.append(cp)
        for k, (fx, fy, fc) in enumerate(_flips()):
            px, py, pc = x ^ fx, y ^ fy, c ^ fc
            pltpu.make_async_remote_copy(
                src_ref=v_ref, dst_ref=o_ref.at[4 * px + 2 * py + pc], send_sem=send.at[k], recv_sem=recv.at[k],
                device_id=(px, py, pc), device_id_type=MESH).wait_recv()
        for cp in sends:
            cp.wait_send()
        own.wait()

    return pl.pallas_call(
        body, name="allgather_small", in_specs=[ANY], out_specs=ANY,
        out_shape=jax.ShapeDtypeStruct((N_DEV,) + v.shape, v.dtype),
        scratch_shapes=[pltpu.SemaphoreType.DMA((7,)), pltpu.SemaphoreType.DMA((7,)), pltpu.SemaphoreType.DMA(())],
        compiler_params=pltpu.CompilerParams(has_side_effects=True),
    )(v)


def _mm_col_fwd(name, a, w4, l, cb, out_dtype, parts, epi_act=None, w4b=None, tm_pref=512):
    T, K = a.shape
    cs = w4.shape[3]
    N = N_SHARD * cs
    tm = _pick(T, tm_pref, 8)
    nps = cs // cb
    npp = (N // parts) // cb
    a_spec = _spec((tm, K), lambda j, i, k: (i, 0))
    b_spec = _spec((None, None, K, cb), lambda j, i, k: (j // nps, l, 0, j % nps))
    if parts == 1:
        o_spec = _spec((tm, cb), lambda j, i, k: (i, j))
        o_shape = (T, N)
    else:
        o_spec = _spec((None, tm, cb), lambda j, i, k: (j // npp, i, j % npp))
        o_shape = (parts, T, N // parts)
    b_ops = [(w4, b_spec)] if w4b is None else [(w4, b_spec), (w4b, b_spec)]
    dots = [(0, 0, 0)] if w4b is None else [(0, 0, 0), (0, 1, 1)]
    out_dtypes = (out_dtype,) if w4b is None else (F32, F32, out_dtype)
    return _mm(name, (N // cb, T // tm, 1), [(a, a_spec)], b_ops, dots, NN, [(tm, cb)] * len(b_ops),
               [(o_shape, dt, o_spec) for dt in out_dtypes], epi=epi_act)


def _mm_row_fwd(name, a, w4, l, res):
    T, K = a.shape
    rs, N = w4.shape[2], w4.shape[3]
    tm = _pick(T, 512, 8)
    tn = _pick(N, 1024, 128)
    a_spec = _spec((tm, rs), lambda j, i, k: (i, k))
    b_spec = _spec((None, None, rs, tn), lambda j, i, k: (k, l, 0, j))
    o_spec = _spec((tm, tn), lambda j, i, k: (i, j))
    return _mm(name, (N // tn, T // tm, N_SHARD), [(a, a_spec)], [(w4, b_spec)], [(0, 0, 0)], NN, [(tm, tn)],
               [((T, N), F32, o_spec)], epi=lambda accs, ex: [accs[0] + ex[0]], extras=[(res, o_spec)])


def _mm_row_bwd_data(name, dy, w4, l, out_dtype=BF16, epi=None, extras=(), n_out=1, tm_pref=512, after=()):
    T, N = dy.shape
    rs = w4.shape[2]
    K = N_SHARD * rs
    tm = _pick(T, tm_pref, 8)
    a_spec = _spec((tm, N), lambda j, i, k: (i, 0))
    b_spec = _spec((None, None, rs, N), lambda j, i, k: (j, l, 0, 0))
    o_spec = _spec((tm, rs), lambda j, i, k: (i, j))
    return _mm(name, (N_SHARD, T // tm, 1), [(dy, a_spec)], [(w4, b_spec)], [(0, 0, 0)], NT, [(tm, rs)],
               [((T, K), out_dtype, o_spec)] * n_out, epi=epi, extras=[(e, o_spec) for e in extras], after=after)


def _mm_row_bwd_weight(name, a, dy, rs):
    T, K = a.shape
    N = dy.shape[1]
    tk = _pick(T, 512, 8)
    tn = _pick(N, 1024, 128)
    a_spec = _spec((tk, rs), lambda i, j, k: (k, i))
    b_spec = _spec((tk, tn), lambda i, j, k: (k, j))
    o_spec = _spec((None, rs, tn), lambda i, j, k: (i, 0, j))
    return _mm(name, (N_SHARD, N // tn, T // tk), [(a, a_spec)], [(dy, b_spec)], [(0, 0, 0)], TN, [(rs, tn)],
               [((N_SHARD, rs, N), BF16, o_spec)])[0]


def _mm_col_bwd_data(name, dys, w4s, l, cb, parts, tm_pref=512):
    T = dys[0].shape[-2]
    K, cs = w4s[0].shape[2], w4s[0].shape[3]
    N = N_SHARD * cs
    tm = _pick(T, tm_pref, 8)
    nps = cs // cb
    npp = (N // parts) // cb
    if parts == 1:
        a_spec = _spec((tm, cb), lambda i, j, k: (i, k))
    else:
        a_spec = _spec((None, tm, cb), lambda i, j, k: (k // npp, i, k % npp))
    b_spec = _spec((None, None, K, cb), lambda i, j, k: (k // nps, l, 0, k % nps))
    o_spec = _spec((tm, K), lambda i, j, k: (i, 0))
    return _mm(name, (T // tm, 1, N // cb), [(d, a_spec) for d in dys], [(w, b_spec) for w in w4s],
               [(p, p, 0) for p in range(len(dys))], NT, [(tm, K)], [((T, K), F32, o_spec)])[0]


def _mm_col_bwd_weight(name, a, dys, cs, cb, parts, tm_pref=512):
    T, K = a.shape
    N = N_SHARD * cs
    tk = _pick(T, 512, 8)
    tm = _pick(K, tm_pref, 128)
    nps = cs // cb
    npp = (N // parts) // cb
    a_spec = _spec((tk, tm), lambda i, j, k: (k, i))
    if parts == 1:
        b_spec = _spec((tk, cb), lambda i, j, k: (k, j))
    else:
        b_spec = _spec((None, tk, cb), lambda i, j, k: (j // npp, k, j % npp))
    o_spec = _spec((None, tm, cb), lambda i, j, k: (j // nps, i, j % nps))
    nd = len(dys)
    return _mm(name, (K // tm, N // cb, T // tk), [(a, a_spec)], [(d, b_spec) for d in dys],
               [(0, p, p) for p in range(nd)], TN, [(tm, cb)] * nd, [((N_SHARD, K, cs), BF16, o_spec)] * nd)


def _swiglu(accs, _):
    g, up = accs
    return [g, up, g * jax.nn.sigmoid(g) * up]


def _swiglu_bwd(accs, ex):
    da = accs[0]
    g = ex[0].astype(F32)
    up = ex[1].astype(F32)
    s = jax.nn.sigmoid(g)
    return [da * up * (s * (1.0 + g * (1.0 - s))), da * (g * s)]


def _ffn_fwd(x1, g_ffn, wg4, wu4, wd4, l):
    h2 = _rmsnorm_fwd(x1, g_ffn, BF16)
    fs = wg4.shape[3]
    gate, up, act = _mm_col_fwd("ffn_up", h2, wg4, l, fs, BF16, 1, epi_act=_swiglu, w4b=wu4, tm_pref=256)
    x2 = _mm_row_fwd("ffn_down", act, wd4, l, x1)[0]
    return x2, (x1, h2, gate, up, act)


def _ffn_bwd(dx2, dx2b, saved, g_ffn, wg4, wu4, wd4, l, after):
    x1, h2, gate, up, act = saved
    fs = wg4.shape[3]
    dgate, dup = _mm_row_bwd_data("ffn_down_bwd_data", dx2b, wd4, l, epi=_swiglu_bwd, extras=(gate, up), n_out=2,
                                  tm_pref=256, after=after)
    g_down = _mm_row_bwd_weight("ffn_down_bwd_weight", act, dx2b, fs)
    dh2 = _mm_col_bwd_data("ffn_up_bwd_data", [dgate, dup], [wg4, wu4], l, fs, 1, tm_pref=256)
    g_gate, g_up = _mm_col_bwd_weight("ffn_up_bwd_weight", h2, [dgate, dup], fs, fs, 1)
    dx1, dx1b, dg = _rmsnorm_bwd(x1, g_ffn, dh2, dx2)
    return dx1, dx1b, dg, g_gate, g_up, g_down


def _sb_fwd(x, g_mix, gq, gk, wqkv4, wo4, l):
    D = x.shape[1]
    h = _rmsnorm_fwd(x, g_mix, BF16)
    qkv = _mm_col_fwd("qkv_proj", h, wqkv4, l, D // 4, F32, 3)[0]
    qs, kn, vb = _headnorm_fwd(qkv, gq, gk)
    o, tot = _attn_fwd(qs, kn, vb)
    x1 = _mm_row_fwd("attn_out", o, wo4, l, x)[0]
    return x1, (x, h, qkv, qs, kn, vb, o, tot)


def _sb_bwd(dx1, dx1b, saved, g_mix, gq, gk, wqkv4, wo4, l):
    x, h, qkv, qs, kn, vb, o, tot = saved
    D = x.shape[1]
    do = _mm_row_bwd_data("attn_out_bwd_data", dx1b, wo4, l)[0]
    g_wo = _mm_row_bwd_weight("attn_out_bwd_weight", o, dx1b, wo4.shape[2])
    dqs, dkn, dv = _attn_bwd(qs, kn, vb, tot, do)
    dqkv, dgq, dgk = _headnorm_bwd(qkv, gq, gk, dqs, dkn, dv)
    dh = _mm_col_bwd_data("qkv_bwd_data", [dqkv], [wqkv4], l, D // 4, 3)
    g_wqkv = _mm_col_bwd_weight("qkv_bwd_weight", h, [dqkv], wqkv4.shape[3], D // 4, 3, tm_pref=2048)[0]
    dx, dxb, dg = _rmsnorm_bwd(x, g_mix, dh, dx1)
    return dx, dxb, dg, dgq, dgk, g_wqkv, g_wo


def _pool_mix_fwd(x, g_mix, wp4, scale):
    T, D = x.shape
    C = D // N_POOL_GROUPS
    rq = C // N_SHARD
    h = _rmsnorm_fwd(x, g_mix, F32)
    p = _pool_fwd(h)
    tm = _pick(T, 512, 8)
    a_spec = _spec((tm, rq), lambda g, i, k: (i, g * N_SHARD + k))
    b_spec = _spec((None, None, rq, C), lambda g, i, k: (k, 0, g, 0))
    o_spec = _spec((tm, C), lambda g, i, k: (i, g))
    s_spec = _spec((1, C), lambda g, i, k: (0, g))
    x1, ypre = _mm("pool_mix", (N_POOL_GROUPS, T // tm, N_SHARD), [(p, a_spec)], [(wp4, b_spec)], [(0, 0, 0)], NN,
                   [(tm, C)], [((T, D), F32, o_spec), ((T, D), BF16, o_spec)],
                   epi=lambda accs, ex: [ex[0] + accs[0] * ex[1], accs[0]], extras=[(x, o_spec), (scale, s_spec)])
    return x1, (x, h, p, ypre)


def _pool_mix_bwd(dx1, dx1b, saved, g_mix, wp4, scale):
    x, h, p, ypre = saved
    T, D = x.shape
    C = D // N_POOL_GROUPS
    rq = C // N_SHARD
    dys, dscale = _pool_scale_bwd(dx1, ypre, scale)
    tm = _pick(T, 512, 8)
    dp = _mm("pool_mix_bwd_data", (N_POOL_GROUPS * N_SHARD, T // tm, 1),
             [(dys, _spec((tm, C), lambda j, i, k: (i, j // N_SHARD)))],
             [(wp4, _spec((None, None, rq, C), lambda j, i, k: (j % N_SHARD, 0, j // N_SHARD, 0)))],
             [(0, 0, 0)], NT, [(tm, rq)], [((T, D), F32, _spec((tm, rq), lambda j, i, k: (i, j)))])[0]
    tk = _pick(T, 512, 8)
    g_wp = _mm("pool_mix_bwd_weight", (N_POOL_GROUPS * N_SHARD, 1, T // tk),
               [(p, _spec((tk, rq), lambda j, n, k: (k, j)))],
               [(dys, _spec((tk, C), lambda j, n, k: (k, j // N_SHARD)))],
               [(0, 0, 0)], TN, [(rq, C)],
               [((N_SHARD, N_POOL_GROUPS * rq, C), BF16,
                 _spec((None, rq, C), lambda j, n, k: (j % N_SHARD, j // N_SHARD, 0)))])[0]
    dh = _pool_bwd(dp)
    dx, dxb, dg = _rmsnorm_bwd(x, g_mix, dh, dx1)
    return dx, dxb, dg, dscale, g_wp


def _conv_mix_fwd(x, g_mix, win4, taps4, wout4):
    D = x.shape[1]
    h = _rmsnorm_fwd(x, g_mix, BF16)
    bcx = _mm_col_fwd("conv_in", h, win4, 0, D // 4, F32, 3)[0]
    q = _conv_fwd(bcx, taps4)
    x1 = _mm_row_fwd("conv_out", q, wout4, 0, x)[0]
    return x1, (x, h, bcx, q)


def _conv_mix_bwd(dx1, dx1b, saved, g_mix, win4, taps4, wout4):
    x, h, bcx, q = saved
    D = x.shape[1]
    dq = _mm_row_bwd_data("conv_out_bwd_data", dx1b, wout4, 0, out_dtype=F32)[0]
    g_wout = _mm_row_bwd_weight("conv_out_bwd_weight", q, dx1b, wout4.shape[2])
    dbcx, dtaps = _conv_bwd(dq, bcx, taps4)
    dh = _mm_col_bwd_data("conv_in_bwd_data", [dbcx], [win4], 0, D // 4, 3)
    g_win = _mm_col_bwd_weight("conv_in_bwd_weight", h, [dbcx], win4.shape[3], D // 4, 3, tm_pref=2048)[0]
    dx, dxb, dg = _rmsnorm_bwd(x, g_mix, dh, dx1)
    return dx, dxb, dg, dtaps, g_wout, g_win


def _rows2d(w):
    return w.reshape(-1, w.shape[-1])


def _pad_rows(v, rows):
    return jnp.pad(v, ((0, rows - v.shape[0]), (0, 0)))


def kernel(x, norm_mix_g, norm_ffn_g, sb_w_qkv, sb_g_q, sb_g_k, sb_w_o, pool_w, pool_scale, conv_w_in, conv_w, conv_w_out, ffn_w_gate, ffn_w_up, ffn_w_down, loss_target, m_norm_mix_g, m_norm_ffn_g, m_sb_w_qkv, m_sb_g_q, m_sb_g_k, m_sb_w_o, m_pool_w, m_pool_scale, m_conv_w_in, m_conv_w, m_conv_w_out, m_ffn_w_gate, m_ffn_w_up, m_ffn_w_down, v_norm_mix_g, v_norm_ffn_g, v_sb_w_qkv, v_sb_g_q, v_sb_g_k, v_sb_w_o, v_pool_w, v_pool_scale, v_conv_w_in, v_conv_w, v_conv_w_out, v_ffn_w_gate, v_ffn_w_up, v_ffn_w_down):
    T, D = x.shape[1], x.shape[2]
    depth = norm_mix_g.shape[0]
    big = dict(sb_w_qkv=(sb_w_qkv, m_sb_w_qkv, v_sb_w_qkv), sb_w_o=(sb_w_o, m_sb_w_o, v_sb_w_o),
               pool_w=(pool_w, m_pool_w, v_pool_w), conv_w_in=(conv_w_in, m_conv_w_in, v_conv_w_in),
               conv_w_out=(conv_w_out, m_conv_w_out, v_conv_w_out), ffn_w_gate=(ffn_w_gate, m_ffn_w_gate, v_ffn_w_gate),
               ffn_w_up=(ffn_w_up, m_ffn_w_up, v_ffn_w_up), ffn_w_down=(ffn_w_down, m_ffn_w_down, v_ffn_w_down))
    names = list(big)

    cs_conv = conv_w.shape[2]
    taps_local = _pad_rows(conv_w[0], 16)
    gathered = _allgather_shards([_cast_into_slot(_rows2d(big[n][0]), BF16) for n in names]
                                 + [_cast_into_slot(taps_local, F32)])
    w4 = {}
    for n, g4 in zip(names, gathered[:-1]):
        layers = big[n][0].shape[0]
        w4[n] = g4.reshape(N_SHARD, layers, g4.shape[1] // layers, g4.shape[2])
    taps4 = gathered[-1]

    xs = x.reshape(T, D)
    saved = []
    for i in range(depth):
        kind, j = i % 3, i // 3
        g_mix = norm_mix_g[i:i + 1]
        if kind == 0:
            xs, sv = _sb_fwd(xs, g_mix, sb_g_q[j:j + 1], sb_g_k[j:j + 1], w4["sb_w_qkv"], w4["sb_w_o"], j)
        elif kind == 1:
            xs, sv = _pool_mix_fwd(xs, g_mix, w4["pool_w"], pool_scale[j:j + 1])
        else:
            xs, sv = _conv_mix_fwd(xs, g_mix, w4["conv_w_in"], taps4, w4["conv_w_out"])
        xs, sf = _ffn_fwd(xs, norm_ffn_g[i:i + 1], w4["ffn_w_gate"], w4["ffn_w_up"], w4["ffn_w_down"], i)
        saved.append((sv, sf))

    dxs, dxb, err2 = _loss_head(xs, loss_target.reshape(T, D))
    loss = lax.psum(0.5 * jnp.sum(err2) / D, ("x", "y", "c"))
    started = []
    small = {}
    token = ()
    for i in reversed(range(depth)):
        kind, j = i % 3, i // 3
        sv, sf = saved[i]
        dxs, dxb, dg, g_gate, g_up, g_down = _ffn_bwd(dxs, dxb, sf, norm_ffn_g[i:i + 1], w4["ffn_w_gate"],
                                                      w4["ffn_w_up"], w4["ffn_w_down"], i, token)
        small[("norm_ffn_g", i)] = dg
        grads = [g_gate, g_up, g_down]
        layout = [("ffn_w_gate", i), ("ffn_w_up", i), ("ffn_w_down", i)]
        g_mix = norm_mix_g[i:i + 1]
        if kind == 0:
            dxs, dxb, dg, dgq, dgk, g_wqkv, g_wo = _sb_bwd(dxs, dxb, sv, g_mix, sb_g_q[j:j + 1], sb_g_k[j:j + 1],
                                                          w4["sb_w_qkv"], w4["sb_w_o"], j)
            small[("sb_g_q", j)], small[("sb_g_k", j)] = dgq, dgk
            grads += [g_wqkv, g_wo]
            layout += [("sb_w_qkv", j), ("sb_w_o", j)]
        elif kind == 1:
            dxs, dxb, dg, dscale, g_wp = _pool_mix_bwd(dxs, dxb, sv, g_mix, w4["pool_w"], pool_scale[j:j + 1])
            small[("pool_scale", j)] = dscale
            grads += [g_wp]
            layout += [("pool_w", j)]
        else:
            dxs, dxb, dg, dtaps, g_wout, g_win = _conv_mix_bwd(dxs, dxb, sv, g_mix, w4["conv_w_in"], taps4,
                                                              w4["conv_w_out"])
            small[("conv_w", j)] = dtaps
            grads += [g_win, g_wout]
            layout += [("conv_w_in", j), ("conv_w_out", j)]
        small[("norm_mix_g", i)] = dg
        handle, tok = _exchange_start(f"exchange_start_{i}", grads)
        started.append((i, handle, layout))
        token = (tok,)
    grad_x = dxs.reshape(x.shape)

    full = {}
    for i, handle, layout in started:
        grads, zones = _exchange_wait(f"exchange_wait_{i}", handle, dg)
        for g, z, (t, l) in zip(grads, zones, layout):
            full[t] = _sum_into(z, g, l, big[t][0].shape[0], full.get(t))
    tensors = sorted(full)
    g_big = dict(zip(tensors, _share_halves([full[t] for t in tensors])))

    n_sb = sb_g_q.shape[0]
    gqk = jnp.concatenate([small[(n, j)] for j in range(n_sb) for n in ("sb_g_q", "sb_g_k")], axis=1)
    dtaps = small[("conv_w", 0)]
    taps_full = jnp.concatenate([dtaps[s, :3] for s in range(N_SHARD)], axis=1)
    pack = jnp.concatenate(
        [small[("norm_mix_g", i)] for i in range(depth)] + [small[("norm_ffn_g", i)] for i in range(depth)]
        + [small[("pool_scale", 0)], jnp.pad(gqk, ((0, 0), (0, D - gqk.shape[1]))), taps_full], axis=0)
    pack = _pad_rows(pack, SMALL_ROWS)
    g_small = _sum_devices(_allgather_small(pack))
    mine = 2 * lax.axis_index("x") + lax.axis_index("y")
    g_taps = lax.dynamic_slice(g_small, (2 * depth + 2, mine * cs_conv), (3, cs_conv))

    def pack_small(norm_mix, norm_ffn, scale, gq, gk, taps):
        qk = jnp.concatenate([v[j:j + 1] for j in range(n_sb) for v in (gq, gk)], axis=1)
        rows = jnp.concatenate([norm_mix, norm_ffn, scale, jnp.pad(qk, ((0, 0), (0, D - qk.shape[1]))),
                                jnp.pad(taps[0], ((0, 0), (0, D - cs_conv)))], axis=0)
        return _pad_rows(rows, SMALL_ROWS)

    g_pack = jnp.concatenate([g_small[:2 * depth + 2], jnp.pad(g_taps, ((0, 0), (0, D - cs_conv))),
                              jnp.zeros((SMALL_ROWS - 2 * depth - 5, D), F32)], axis=0)
    w_pack = pack_small(norm_mix_g, norm_ffn_g, pool_scale, sb_g_q, sb_g_k, conv_w)
    m_pack = pack_small(m_norm_mix_g, m_norm_ffn_g, m_pool_scale, m_sb_g_q, m_sb_g_k, m_conv_w)
    v_pack = pack_small(v_norm_mix_g, v_norm_ffn_g, v_pool_scale, v_sb_g_q, v_sb_g_k, v_conv_w)
    small_out = (g_pack,) + tuple(_adamw(w_pack, g_pack, m_pack, v_pack))

    def unpack_small(p):
        qk = p[2 * depth + 1]
        gq = jnp.stack([qk[(2 * j) * HEAD_DIM:(2 * j + 1) * HEAD_DIM] for j in range(n_sb)])
        gk = jnp.stack([qk[(2 * j + 1) * HEAD_DIM:(2 * j + 2) * HEAD_DIM] for j in range(n_sb)])
        return dict(norm_mix_g=p[:depth], norm_ffn_g=p[depth:2 * depth], pool_scale=p[2 * depth:2 * depth + 1],
                    sb_g_q=gq, sb_g_k=gk, conv_w=p[2 * depth + 2:2 * depth + 5, :cs_conv][None])

    results = [unpack_small(p) for p in small_out]
    for n in names:
        w, m, v = big[n]
        g = _rows2d(g_big[n])
        outs = (g,) + tuple(_adamw(_rows2d(w), g, _rows2d(m), _rows2d(v)))
        for r, o in zip(results, outs):
            r[n] = o.reshape(w.shape)

    order = ["norm_mix_g", "norm_ffn_g", "sb_w_qkv", "sb_g_q", "sb_g_k", "sb_w_o", "pool_w", "pool_scale",
             "conv_w_in", "conv_w", "conv_w_out", "ffn_w_gate", "ffn_w_up", "ffn_w_down"]
    return (loss, grad_x) + tuple(r[n] for r in results for n in order)
```

```python
import math

import jax
import jax.numpy as jnp
from jax import lax
from jax.experimental import pallas as pl
from jax.experimental.pallas import tpu as pltpu

F32 = jnp.float32
BF16 = jnp.bfloat16

HEAD_DIM = 128
N_POOL_GROUPS = 4
EPS = 1e-6
N_SHARD = 4
N_DEV = 8
VMEM_LIMIT_BYTES = 56 * 2**20
KEY_BLOCK = 128
ATTN_ROWS = 512
ATTN_GROUP_FWD = 4
ATTN_GROUP_BWD = 2
SMALL_ROWS = 16

ADAM_LR = 0.001
ADAM_B1 = 0.9
ADAM_B2 = 0.999
ADAM_EPS = 1e-08
ADAM_WD = 0.01
ADAM_STEP = 10

MESH = pl.DeviceIdType.MESH
ANY = pl.BlockSpec(memory_space=pl.ANY)


def _params(*sem):
    return pltpu.CompilerParams(dimension_semantics=sem, vmem_limit_bytes=VMEM_LIMIT_BYTES)


def _pick(n, pref, unit):
    t = (min(pref, n) // unit) * unit
    while n % t:
        t -= unit
    return t


NN = ((1,), (0,))
NT = ((1,), (1,))
TN = ((0,), (0,))


def _mm(name, grid, a_ops, b_ops, dots, dims, acc_shapes, outs, epi=None, extras=(), after=()):
    na, nb, ne, no, nacc = len(a_ops), len(b_ops), len(extras), len(outs), len(acc_shapes)
    nk = grid[2]
    n_in = na + nb + ne + len(after)

    def body(*refs):
        a_refs, b_refs = refs[:na], refs[na:na + nb]
        e_refs = refs[na + nb:na + nb + ne]
        o_refs = refs[n_in:n_in + no]
        acc_refs = refs[n_in + no:]

        def partial_sums():
            sums = [None] * nacc
            for ai, bi, ci in dots:
                d = lax.dot_general(a_refs[ai][...].astype(BF16), b_refs[bi][...].astype(BF16),
                                    (dims, ((), ())), preferred_element_type=F32)
                sums[ci] = d if sums[ci] is None else sums[ci] + d
            return sums

        def finish(accs):
            res = epi(accs, [e[...] for e in e_refs]) if epi is not None else accs
            for o, r in zip(o_refs, res):
                o[...] = r.astype(o.dtype)

        if nk == 1:
            finish(partial_sums())
            return
        k = pl.program_id(2)

        @pl.when(k == 0)
        def _():
            for acc, s in zip(acc_refs, partial_sums()):
                acc[...] = s

        @pl.when(k > 0)
        def _():
            for acc, s in zip(acc_refs, partial_sums()):
                acc[...] += s

        @pl.when(k == nk - 1)
        def _():
            finish([acc[...] for acc in acc_refs])

    ops = list(a_ops) + list(b_ops) + list(extras) + [(t, ANY) for t in after]
    return pl.pallas_call(
        body, name=name, grid=grid,
        in_specs=[s for _, s in ops],
        out_specs=[s for _, _, s in outs],
        out_shape=[jax.ShapeDtypeStruct(sh, dt) for sh, dt, _ in outs],
        scratch_shapes=[pltpu.VMEM(s, F32) for s in acc_shapes] if nk > 1 else [],
        compiler_params=_params("parallel", "parallel", "arbitrary"),
    )(*[a for a, _ in ops])


def _spec(block, index):
    return pl.BlockSpec(block, index)


def _rmsnorm_fwd(x, g, out_dtype):
    T, D = x.shape
    tm = _pick(T, 256, 8)

    def body(x_ref, g_ref, o_ref):
        xv = x_ref[...]
        r = lax.rsqrt(jnp.mean(xv * xv, axis=-1, keepdims=True) + EPS)
        o_ref[...] = (xv * r * g_ref[...]).astype(o_ref.dtype)

    row = _spec((tm, D), lambda i: (i, 0))
    return pl.pallas_call(
        body, name="rmsnorm_fwd", grid=(T // tm,),
        in_specs=[row, _spec((1, D), lambda i: (0, 0))], out_specs=row,
        out_shape=jax.ShapeDtypeStruct((T, D), out_dtype), compiler_params=_params("parallel"),
    )(x, g)


def _rmsnorm_bwd(x, g, dh, dres):
    T, D = x.shape
    tm = _pick(T, 256, 8)

    def body(x_ref, g_ref, dh_ref, dres_ref, dx_ref, dxb_ref, dg_ref):
        xv = x_ref[...]
        dhv = dh_ref[...].astype(F32)
        r = lax.rsqrt(jnp.mean(xv * xv, axis=-1, keepdims=True) + EPS)
        xh = xv * r
        dxh = dhv * g_ref[...]
        dx = r * (dxh - xh * jnp.mean(dxh * xh, axis=-1, keepdims=True)) + dres_ref[...]
        dx_ref[...] = dx
        dxb_ref[...] = dx.astype(BF16)
        part = jnp.sum(dhv * xh, axis=0, keepdims=True)

        @pl.when(pl.program_id(0) == 0)
        def _():
            dg_ref[...] = part

        @pl.when(pl.program_id(0) > 0)
        def _():
            dg_ref[...] += part

    row = _spec((tm, D), lambda i: (i, 0))
    vec = _spec((1, D), lambda i: (0, 0))
    return pl.pallas_call(
        body, name="rmsnorm_bwd", grid=(T // tm,),
        in_specs=[row, vec, row, row], out_specs=[row, row, vec],
        out_shape=[jax.ShapeDtypeStruct((T, D), F32), jax.ShapeDtypeStruct((T, D), BF16),
                   jax.ShapeDtypeStruct((1, D), F32)],
        compiler_params=_params("arbitrary"),
    )(x, g, dh, dres)


def _loss_head(y, target):
    T, D = y.shape
    tm = _pick(T, 256, 8)

    def body(y_ref, t_ref, dy_ref, dyb_ref, l_ref):
        err = y_ref[...] - t_ref[...]
        dy = err * (1.0 / D)
        dy_ref[...] = dy
        dyb_ref[...] = dy.astype(BF16)
        part = jnp.sum(err * err, axis=0, keepdims=True)

        @pl.when(pl.program_id(0) == 0)
        def _():
            l_ref[...] = part

        @pl.when(pl.program_id(0) > 0)
        def _():
            l_ref[...] += part

    row = _spec((tm, D), lambda i: (i, 0))
    vec = _spec((1, D), lambda i: (0, 0))
    return pl.pallas_call(
        body, name="loss_head", grid=(T // tm,),
        in_specs=[row, row], out_specs=[row, row, vec],
        out_shape=[jax.ShapeDtypeStruct((T, D), F32), jax.ShapeDtypeStruct((T, D), BF16),
                   jax.ShapeDtypeStruct((1, D), F32)],
        compiler_params=_params("arbitrary"),
    )(y, target)


def _headnorm_fwd(qkv, gq, gk):
    _, T, D = qkv.shape
    H = D // HEAD_DIM
    tm = _pick(T, 256, 8)
    scale = HEAD_DIM ** -0.5

    def body(q_ref, k_ref, v_ref, gq_ref, gk_ref, qs_ref, kn_ref, vb_ref):
        for h in range(H):
            sl = slice(h * HEAD_DIM, (h + 1) * HEAD_DIM)
            q = q_ref[:, sl]
            qs_ref[:, sl] = (q * lax.rsqrt(jnp.mean(q * q, axis=-1, keepdims=True) + EPS)
                             * (gq_ref[...] * scale)).astype(BF16)
            k = k_ref[:, sl]
            kn_ref[:, sl] = (k * lax.rsqrt(jnp.mean(k * k, axis=-1, keepdims=True) + EPS)
                             * gk_ref[...]).astype(BF16)
        vb_ref[...] = v_ref[...].astype(BF16)

    part = lambda p: _spec((None, tm, D), lambda i, p=p: (p, i, 0))
    row = _spec((tm, D), lambda i: (i, 0))
    vec = _spec((1, HEAD_DIM), lambda i: (0, 0))
    return pl.pallas_call(
        body, name="headnorm_fwd", grid=(T // tm,),
        in_specs=[part(0), part(1), part(2), vec, vec], out_specs=[row, row, row],
        out_shape=[jax.ShapeDtypeStruct((T, D), BF16)] * 3, compiler_params=_params("parallel"),
    )(qkv, qkv, qkv, gq, gk)


def _headnorm_bwd(qkv, gq, gk, dqs, dkn, dv):
    _, T, D = qkv.shape
    H = D // HEAD_DIM
    tm = _pick(T, 256, 8)
    scale = HEAD_DIM ** -0.5

    def body(q_ref, k_ref, gq_ref, gk_ref, dqs_ref, dkn_ref, dv_ref, dqkv_ref, dgq_ref, dgk_ref):
        dgq = jnp.zeros((1, HEAD_DIM), F32)
        dgk = jnp.zeros((1, HEAD_DIM), F32)
        for h in range(H):
            sl = slice(h * HEAD_DIM, (h + 1) * HEAD_DIM)
            for src, dsrc, g_ref, sc, p in ((q_ref, dqs_ref, gq_ref, scale, 0), (k_ref, dkn_ref, gk_ref, 1.0, 1)):
                v = src[:, sl]
                r = lax.rsqrt(jnp.mean(v * v, axis=-1, keepdims=True) + EPS)
                vh = v * r
                dn = dsrc[:, sl] * sc
                dvh = dn * g_ref[...]
                dqkv_ref[p, :, sl] = (r * (dvh - vh * jnp.mean(dvh * vh, axis=-1, keepdims=True))).astype(BF16)
                dg = jnp.sum(dn * vh, axis=0, keepdims=True)
                if p == 0:
                    dgq = dgq + dg
                else:
                    dgk = dgk + dg
        dqkv_ref[2] = dv_ref[...].astype(BF16)

        @pl.when(pl.program_id(0) == 0)
        def _():
            dgq_ref[...] = dgq
            dgk_ref[...] = dgk

        @pl.when(pl.program_id(0) > 0)
        def _():
            dgq_ref[...] += dgq
            dgk_ref[...] += dgk

    part = lambda p: _spec((None, tm, D), lambda i, p=p: (p, i, 0))
    row = _spec((tm, D), lambda i: (i, 0))
    vec = _spec((1, HEAD_DIM), lambda i: (0, 0))
    return pl.pallas_call(
        body, name="headnorm_bwd", grid=(T // tm,),
        in_specs=[part(0), part(1), vec, vec, row, row, row],
        out_specs=[_spec((3, tm, D), lambda i: (0, i, 0)), vec, vec],
        out_shape=[jax.ShapeDtypeStruct((3, T, D), BF16), jax.ShapeDtypeStruct((1, HEAD_DIM), F32),
                   jax.ShapeDtypeStruct((1, HEAD_DIM), F32)],
        compiler_params=_params("arbitrary"),
    )(qkv, qkv, gq, gk, dqs, dkn, dv)


def _sum_matrix(prefix):
    r = lax.broadcasted_iota(jnp.int32, (KEY_BLOCK, 2 * KEY_BLOCK), 0)
    c = lax.broadcasted_iota(jnp.int32, (KEY_BLOCK, 2 * KEY_BLOCK), 1)
    tri = (r <= c) if prefix else (r > c)
    return jnp.where(tri | (c >= KEY_BLOCK), 1.0, 0.0).astype(BF16)


def _block_sums(v, u):
    hi = v.astype(BF16)
    lo = (v - hi.astype(F32)).astype(BF16)
    s = (jnp.dot(hi, u, preferred_element_type=F32) + jnp.dot(lo, u, preferred_element_type=F32))
    return s[:, :KEY_BLOCK], s[:, KEY_BLOCK:]


def _log_terms(q, kj, mask):
    z = lax.dot_general(q, kj, (NT, ((), ())), preferred_element_type=F32)
    ls = jnp.minimum(z, 0.0) - jnp.log(1.0 + jnp.exp(-jnp.abs(z)))
    lk = ls - z
    if mask is not None:
        lk = jnp.where(mask, lk, 0.0)
    return ls, lk


def _causal_mask(row0, key0, tq):
    t = row0 + lax.broadcasted_iota(jnp.int32, (tq, KEY_BLOCK), 0)
    s = key0 + lax.broadcasted_iota(jnp.int32, (tq, KEY_BLOCK), 1)
    return s < t


def _attn_fwd(qs, kn, vb):
    T, D = qs.shape
    H = D // HEAD_DIM
    ATTN_GROUP = ATTN_GROUP_FWD
    tq = _pick(T, ATTN_ROWS, ATTN_GROUP * KEY_BLOCK)
    nd = tq // KEY_BLOCK

    def body(q_ref, k_ref, v_ref, u_ref, o_ref, tot_ref, acc_ref, run_ref):
        i = pl.program_id(1)
        q = q_ref[...]
        uv = u_ref[...]
        acc_ref[...] = jnp.zeros_like(acc_ref)
        run_ref[...] = jnp.zeros_like(run_ref)

        def group(first, masked):
            parts = []
            for n in range(ATTN_GROUP):
                k0 = pl.multiple_of((first - n) * KEY_BLOCK, KEY_BLOCK)
                mask = _causal_mask(i * tq, k0, tq) if masked else None
                ls, lk = _log_terms(q, k_ref[pl.ds(k0, KEY_BLOCK), :], mask)
                after, rows = _block_sums(lk, uv)
                parts.append((k0, mask, ls + after, rows))
            run = run_ref[...]
            acc = acc_ref[...]
            for k0, mask, base, rows in parts:
                a = jnp.exp(base + run)
                if masked:
                    a = jnp.where(mask, a, 0.0)
                acc = acc + jnp.dot(a.astype(BF16), v_ref[pl.ds(k0, KEY_BLOCK), :], preferred_element_type=F32)
                run = run + rows
            acc_ref[...] = acc
            run_ref[...] = run

        for d in range(nd // ATTN_GROUP):
            group(i * nd + nd - 1 - d * ATTN_GROUP, True)

        def below(n, c):
            group(i * nd - 1 - n * ATTN_GROUP, False)
            return c

        lax.fori_loop(0, i * (nd // ATTN_GROUP), below, 0)
        o_ref[...] = acc_ref[...].astype(o_ref.dtype)
        tot_ref[...] = run_ref[...]

    blk = _spec((tq, HEAD_DIM), lambda h, i: (i, h))
    col = _spec((T, HEAD_DIM), lambda h, i: (0, h))
    return pl.pallas_call(
        body, name="attn_fwd", grid=(H, T // tq),
        in_specs=[blk, col, col, _spec((KEY_BLOCK, 2 * KEY_BLOCK), lambda h, i: (0, 0))], out_specs=[blk, blk],
        out_shape=[jax.ShapeDtypeStruct((T, D), BF16), jax.ShapeDtypeStruct((T, D), F32)],
        scratch_shapes=[pltpu.VMEM((tq, HEAD_DIM), F32), pltpu.VMEM((tq, KEY_BLOCK), F32)],
        compiler_params=_params("parallel", "arbitrary"),
    )(qs, kn, vb, _sum_matrix(False))


def _attn_bwd(qs, kn, vb, tot, do):
    T, D = qs.shape
    H = D // HEAD_DIM
    ATTN_GROUP = ATTN_GROUP_BWD
    tq = _pick(T, ATTN_ROWS, ATTN_GROUP * KEY_BLOCK)
    nd = tq // KEY_BLOCK

    def body(q_ref, k_ref, v_ref, tot_ref, do_ref, u_ref, dq_ref, dk_ref, dv_ref, run_ref, grun_ref):
        i = pl.program_id(1)

        @pl.when(i == 0)
        def _():
            dk_ref[...] = jnp.zeros_like(dk_ref)
            dv_ref[...] = jnp.zeros_like(dv_ref)

        q = q_ref[...]
        dov = do_ref[...]
        uv = u_ref[...]
        dq_ref[...] = jnp.zeros_like(dq_ref)
        run_ref[...] = jnp.zeros_like(run_ref)
        grun_ref[...] = jnp.zeros_like(grun_ref)

        def group(first, masked):
            parts = []
            for n in range(ATTN_GROUP):
                k0 = pl.multiple_of((first + n) * KEY_BLOCK, KEY_BLOCK)
                keys = pl.ds(k0, KEY_BLOCK)
                mask = _causal_mask(i * tq, k0, tq) if masked else None
                kj = k_ref[keys, :]
                ls, lk = _log_terms(q, kj, mask)
                upto, rows = _block_sums(lk, uv)
                da = lax.dot_general(dov, v_ref[keys, :], (NT, ((), ())), preferred_element_type=F32)
                parts.append((keys, mask, kj, ls, tot_ref[...] - upto, rows, da))
            run = run_ref[...]
            grun = grun_ref[...]
            dqa = dq_ref[...]
            for keys, mask, kj, ls, right, rows, da in parts:
                a = jnp.exp(ls + (right - run))
                if masked:
                    a = jnp.where(mask, a, 0.0)
                g = a * da
                gupto, grows = _block_sums(g, uv)
                dz = g - jnp.exp(ls) * (grun + gupto)
                if masked:
                    dz = jnp.where(mask, dz, 0.0)
                dzb = dz.astype(BF16)
                dqa = dqa + jnp.dot(dzb, kj, preferred_element_type=F32)
                dk_ref[keys, :] += lax.dot_general(dzb, q, (TN, ((), ())), preferred_element_type=F32)
                dv_ref[keys, :] += lax.dot_general(a.astype(BF16), dov, (TN, ((), ())), preferred_element_type=F32)
                run = run + rows
                grun = grun + grows
            run_ref[...] = run
            grun_ref[...] = grun
            dq_ref[...] = dqa

        def below(n, c):
            group(n * ATTN_GROUP, False)
            return c

        lax.fori_loop(0, i * (nd // ATTN_GROUP), below, 0)
        for d in range(nd // ATTN_GROUP):
            group(i * nd + d * ATTN_GROUP, True)

    blk = _spec((tq, HEAD_DIM), lambda h, i: (i, h))
    col = _spec((T, HEAD_DIM), lambda h, i: (0, h))
    return pl.pallas_call(
        body, name="attn_bwd", grid=(H, T // tq),
        in_specs=[blk, col, col, blk, blk, _spec((KEY_BLOCK, 2 * KEY_BLOCK), lambda h, i: (0, 0))],
        out_specs=[blk, col, col],
        out_shape=[jax.ShapeDtypeStruct((T, D), F32)] * 3,
        scratch_shapes=[pltpu.VMEM((tq, KEY_BLOCK), F32), pltpu.VMEM((tq, KEY_BLOCK), F32)],
        compiler_params=_params("parallel", "arbitrary"),
    )(qs, kn, vb, tot, do, _sum_matrix(True))


def _shift_down(v, n):
    t = lax.broadcasted_iota(jnp.int32, v.shape, 0)
    return jnp.where(t >= n, pltpu.roll(v, n, 0), 0.0)


def _shift_up(v, n):
    rows = v.shape[0]
    t = lax.broadcasted_iota(jnp.int32, v.shape, 0)
    return jnp.where(t < rows - n, pltpu.roll(v, rows - n, 0), 0.0)


def _pool_window(j, cw, D):
    group = (j * cw) // (D // N_POOL_GROUPS)
    return jnp.left_shift(2, group)


def _pool_count(shape, w):
    t = lax.broadcasted_iota(jnp.int32, shape, 0)
    return jnp.minimum(t + 1, w).astype(F32)


def _pool_fwd(h):
    T, D = h.shape
    cw = min(256, D // N_POOL_GROUPS)

    def body(h_ref, p_ref):
        w = _pool_window(pl.program_id(0), cw, D)
        hv = h_ref[...]
        s = hv
        for n in (1, 2, 4, 8):
            s = jnp.where(n < w, s + _shift_down(s, n), s)
        p_ref[...] = (s / _pool_count(hv.shape, w) - hv).astype(p_ref.dtype)

    slab = _spec((T, cw), lambda j: (0, j))
    return pl.pallas_call(
        body, name="pool_fwd", grid=(D // cw,), in_specs=[slab], out_specs=slab,
        out_shape=jax.ShapeDtypeStruct((T, D), BF16), compiler_params=_params("parallel"),
    )(h)


def _pool_bwd(dp):
    T, D = dp.shape
    cw = min(256, D // N_POOL_GROUPS)

    def body(dp_ref, dh_ref):
        w = _pool_window(pl.program_id(0), cw, D)
        dpv = dp_ref[...]
        s = dpv / _pool_count(dpv.shape, w)
        for n in (1, 2, 4, 8):
            s = jnp.where(n < w, s + _shift_up(s, n), s)
        dh_ref[...] = (s - dpv).astype(dh_ref.dtype)

    slab = _spec((T, cw), lambda j: (0, j))
    return pl.pallas_call(
        body, name="pool_bwd", grid=(D // cw,), in_specs=[slab], out_specs=slab,
        out_shape=jax.ShapeDtypeStruct((T, D), F32), compiler_params=_params("parallel"),
    )(dp)


def _pool_scale_bwd(dx, ypre, scale):
    T, D = dx.shape
    tm = _pick(T, 256, 8)

    def body(dx_ref, y_ref, s_ref, dys_ref, ds_ref):
        dxv = dx_ref[...]
        dys_ref[...] = (dxv * s_ref[...]).astype(BF16)
        part = jnp.sum(dxv * y_ref[...].astype(F32), axis=0, keepdims=True)

        @pl.when(pl.program_id(0) == 0)
        def _():
            ds_ref[...] = part

        @pl.when(pl.program_id(0) > 0)
        def _():
            ds_ref[...] += part

    row = _spec((tm, D), lambda i: (i, 0))
    vec = _spec((1, D), lambda i: (0, 0))
    return pl.pallas_call(
        body, name="pool_scale_bwd", grid=(T // tm,), in_specs=[row, row, vec], out_specs=[row, vec],
        out_shape=[jax.ShapeDtypeStruct((T, D), BF16), jax.ShapeDtypeStruct((1, D), F32)],
        compiler_params=_params("arbitrary"),
    )(dx, ypre, scale)


def _conv_specs(T, D, cw, cs):
    part = lambda p: _spec((None, T, cw), lambda j, p=p: (p, 0, j))
    taps = _spec((None, 8, cw), lambda j: (j // (cs // cw), 0, j % (cs // cw)))
    return part, taps


def _conv_fwd(bcx, taps4):
    _, T, D = bcx.shape
    cs = taps4.shape[2]
    cw = min(128, cs)
    part, taps = _conv_specs(T, D, cw, cs)

    def body(b_ref, c_ref, u_ref, w_ref, q_ref):
        g = c_ref[...].astype(F32) * u_ref[...].astype(F32)
        w = w_ref[...]
        y = w[2:3] * g + w[1:2] * _shift_down(g, 1) + w[0:1] * _shift_down(g, 2)
        q_ref[...] = (b_ref[...].astype(F32) * y).astype(q_ref.dtype)

    return pl.pallas_call(
        body, name="conv_fwd", grid=(D // cw,), in_specs=[part(0), part(1), part(2), taps],
        out_specs=_spec((T, cw), lambda j: (0, j)),
        out_shape=jax.ShapeDtypeStruct((T, D), BF16), compiler_params=_params("parallel"),
    )(bcx, bcx, bcx, taps4)


def _conv_bwd(dq, bcx, taps4):
    _, T, D = bcx.shape
    cs = taps4.shape[2]
    cw = min(128, cs)
    part, taps = _conv_specs(T, D, cw, cs)

    def body(dq_ref, b_ref, c_ref, u_ref, w_ref, d_ref, dw_ref):
        b = b_ref[...].astype(F32)
        c = c_ref[...].astype(F32)
        uu = u_ref[...].astype(F32)
        dqv = dq_ref[...].astype(F32)
        w = w_ref[...]
        g = c * uu
        g1 = _shift_down(g, 1)
        g2 = _shift_down(g, 2)
        d_ref[0] = (dqv * (w[2:3] * g + w[1:2] * g1 + w[0:1] * g2)).astype(BF16)
        dy = dqv * b
        dg = w[2:3] * dy + w[1:2] * _shift_up(dy, 1) + w[0:1] * _shift_up(dy, 2)
        d_ref[1] = (dg * uu).astype(BF16)
        d_ref[2] = (dg * c).astype(BF16)
        dw_ref[0:1, :] = jnp.sum(dy * g2, axis=0, keepdims=True)
        dw_ref[1:2, :] = jnp.sum(dy * g1, axis=0, keepdims=True)
        dw_ref[2:3, :] = jnp.sum(dy * g, axis=0, keepdims=True)
        dw_ref[3:8, :] = jnp.zeros((5, cw), F32)

    return pl.pallas_call(
        body, name="conv_bwd", grid=(D // cw,),
        in_specs=[_spec((T, cw), lambda j: (0, j)), part(0), part(1), part(2), taps],
        out_specs=[_spec((3, T, cw), lambda j: (0, 0, j)), taps],
        out_shape=[jax.ShapeDtypeStruct((3, T, D), BF16), jax.ShapeDtypeStruct((N_SHARD, 8, cs), F32)],
        compiler_params=_params("parallel"),
    )(dq, bcx, bcx, bcx, taps4)


def _quarter():
    return 2 * lax.axis_index("x") + lax.axis_index("y")


def _cast_into_slot(w, dtype, row0=0, rows=None):
    R, C = (w.shape[0] if rows is None else rows), w.shape[1]
    tr = _pick(math.gcd(R, row0), 512, 8)
    first = row0 // tr

    def body(w_ref, o_ref):
        o_ref[...] = w_ref[...].astype(o_ref.dtype)

    return pl.pallas_call(
        body, name="cast_into_slot", grid=(R // tr,),
        in_specs=[_spec((tr, C), lambda i: (first + i, 0))],
        out_specs=_spec((None, tr, C), lambda i: (_quarter(), i, 0)),
        out_shape=jax.ShapeDtypeStruct((N_SHARD, R, C), dtype), compiler_params=_params("parallel"),
    )(w)


def _sum_into(recv, own, l, L, prev):
    n, R2, C = recv.shape
    tr = _pick(R2, 256, 8)
    nb = R2 // tr

    def body(r_ref, o_ref, *rest):
        s = o_ref[...].astype(F32)
        for k in range(n):
            s = s + r_ref[k].astype(F32)
        rest[-1][...] = s

    in_specs = [_spec((n, tr, C), lambda i: (0, i, 0)),
                _spec((None, tr, C), lambda i: (_quarter(), lax.axis_index("c") * nb + i, 0))]
    args = [recv, own]
    if prev is not None:
        in_specs.append(ANY)
        args.append(prev)
    return pl.pallas_call(
        body, name="sum_into", grid=(nb,), in_specs=in_specs,
        out_specs=_spec((None, tr, C), lambda i: (l, lax.axis_index("c") * nb + i, 0)),
        out_shape=jax.ShapeDtypeStruct((L, 2 * R2, C), F32),
        input_output_aliases={} if prev is None else {2: 0},
        compiler_params=_params("parallel"),
    )(*args)


def _sum_devices(parts):
    n, R, C = parts.shape
    tr = _pick(R, 256, 8)

    def body(p_ref, o_ref):
        s = p_ref[0].astype(F32)
        for d in range(1, n):
            s = s + p_ref[d].astype(F32)
        o_ref[...] = s

    return pl.pallas_call(
        body, name="sum_devices", grid=(R // tr,),
        in_specs=[_spec((n, tr, C), lambda i: (0, i, 0))], out_specs=_spec((tr, C), lambda i: (i, 0)),
        out_shape=jax.ShapeDtypeStruct((R, C), F32), compiler_params=_params("parallel"),
    )(parts)


def _adamw(w, g, m, v):
    R, C = w.shape
    tr = _pick(R, 256, 8)

    def body(w_ref, g_ref, m_ref, v_ref, d_ref, nm_ref, nv_ref):
        gv = g_ref[...]
        m2 = ADAM_B1 * m_ref[...] + (1.0 - ADAM_B1) * gv
        v2 = ADAM_B2 * v_ref[...] + (1.0 - ADAM_B2) * (gv * gv)
        m_hat = m2 / (1.0 - ADAM_B1 ** ADAM_STEP)
        v_hat = v2 / (1.0 - ADAM_B2 ** ADAM_STEP)
        d_ref[...] = -ADAM_LR * (m_hat / (jnp.sqrt(v_hat) + ADAM_EPS) + ADAM_WD * w_ref[...])
        nm_ref[...] = m2
        nv_ref[...] = v2

    row = _spec((tr, C), lambda i: (i, 0))
    return pl.pallas_call(
        body, name="adamw", grid=(R // tr,), in_specs=[row] * 4, out_specs=[row] * 3,
        out_shape=[jax.ShapeDtypeStruct((R, C), F32)] * 3, compiler_params=_params("parallel"),
    )(w, g, m, v)


def _place():
    return lax.axis_index("x"), lax.axis_index("y"), lax.axis_index("c")


def _half(ref_rows, c):
    return pl.ds(c * (ref_rows // 2), ref_rows // 2)


def _allgather_shards(bufs):
    n = len(bufs)

    def body(*refs):
        outs = refs[n:2 * n]
        send, recv = refs[2 * n:]
        x, y, c = _place()
        chips = [(1 - x, y), (x, 1 - y), (1 - x, 1 - y)]

        def copy(a, k, quarter, core, to):
            part = outs[a].at[quarter, _half(outs[a].shape[1], core)]
            return pltpu.make_async_remote_copy(
                src_ref=part, dst_ref=part, send_sem=send.at[a, k], recv_sem=recv.at[a, k],
                device_id=to, device_id_type=MESH)

        first, passed = [], []
        for a in range(n):
            for k, (cx, cy) in enumerate(chips):
                cp = copy(a, k, 2 * x + y, c, (cx, cy, c))
                cp.start()
                first.append(cp)
        for a in range(n):
            for k, (cx, cy) in enumerate(chips):
                copy(a, k, 2 * cx + cy, c, (x, y, c)).wait_recv()
                cp = copy(a, 3 + k, 2 * cx + cy, c, (x, y, 1 - c))
                cp.start()
                passed.append(cp)
        for a in range(n):
            for k, (cx, cy) in enumerate(chips):
                copy(a, 3 + k, 2 * cx + cy, 1 - c, (x, y, 1 - c)).wait_recv()
        for cp in first + passed:
            cp.wait_send()

    return pl.pallas_call(
        body, name="allgather_shards", in_specs=[ANY] * n, out_specs=[ANY] * n,
        out_shape=[jax.ShapeDtypeStruct(b.shape, b.dtype) for b in bufs],
        input_output_aliases={a: a for a in range(n)},
        scratch_shapes=[pltpu.SemaphoreType.DMA((n, 6)), pltpu.SemaphoreType.DMA((n, 6))],
        compiler_params=pltpu.CompilerParams(has_side_effects=True),
    )(*bufs)


def _flips():
    return [(fx, fy, fc) for fx in (0, 1) for fy in (0, 1) for fc in (0, 1) if (fx, fy, fc) != (0, 0, 0)]


HBM = pl.BlockSpec(memory_space=pltpu.HBM)
SEM = pl.BlockSpec(memory_space=pltpu.SEMAPHORE)
DATAFLOW = pltpu.SideEffectType.DATAFLOW_SIDE_EFFECTING


def _gather_copies(bufs, send, recv):
    x, y, c = _place()
    pairs = []
    for a in range(len(bufs)):
        for k, (cx, cy) in enumerate([(1 - x, y), (x, 1 - y), (1 - x, 1 - y)]):
            def copy(quarter, a=a, k=k, cx=cx, cy=cy):
                part = bufs[a].at[quarter, _half(bufs[a].shape[1], c)]
                return pltpu.make_async_remote_copy(
                    src_ref=part, dst_ref=part, send_sem=send.at[3 * a + k], recv_sem=recv.at[3 * a + k],
                    device_id=(cx, cy, c), device_id_type=MESH)
            pairs.append((copy(2 * x + y), copy(2 * cx + cy)))
    return pairs


def _gather_start(name, bufs, after):
    n = len(bufs)

    def body(*refs):
        for out, _ in _gather_copies(refs[:n], refs[n + 1], refs[n + 2]):
            out.start()
        refs[-1][...] = jnp.zeros_like(refs[-1])

    outs = pl.pallas_call(
        body, name=name, in_specs=[HBM] * n + [ANY],
        out_specs=(SEM, SEM) + (HBM,) * n + (pl.BlockSpec(memory_space=pltpu.VMEM),),
        out_shape=(pltpu.SemaphoreType.DMA((3 * n,)), pltpu.SemaphoreType.DMA((3 * n,)))
        + tuple(pltpu.HBM(b.shape, b.dtype) for b in bufs) + (jax.ShapeDtypeStruct((8, 128), F32),),
        input_output_aliases={a: 2 + a for a in range(n)},
        compiler_params=pltpu.CompilerParams(has_side_effects=DATAFLOW),
    )(*[pltpu.with_memory_space_constraint(b, pltpu.HBM) for b in bufs], after)
    return outs[:-1], outs[-1]


def _gather_wait(name, started, after):
    n = len(started) - 2

    def body(*refs):
        for out, arriving in _gather_copies(refs[:n], refs[n], refs[n + 1]):
            out.wait_send()
            arriving.wait_recv()

    return pl.pallas_call(
        body, name=name, in_specs=[HBM] * n + [SEM, SEM, ANY], out_specs=(HBM,) * n,
        out_shape=tuple(pltpu.HBM(b.shape, b.dtype) for b in started[2:]),
        input_output_aliases={a: a for a in range(n)},
        compiler_params=pltpu.CompilerParams(has_side_effects=DATAFLOW),
    )(*started[2:], started[0], started[1], after)


def _forward_halves(bufs):
    n = len(bufs)

    def body(*refs):
        outs = refs[n:2 * n]
        send, recv = refs[2 * n:]
        x, y, c = _place()

        def copy(a, k, quarter, core):
            part = outs[a].at[quarter, _half(outs[a].shape[1], core)]
            return pltpu.make_async_remote_copy(
                src_ref=part, dst_ref=part, send_sem=send.at[a, k], recv_sem=recv.at[a, k],
                device_id=(x, y, 1 - c), device_id_type=MESH)

        others = [2 * (1 - x) + y, 2 * x + (1 - y), 2 * (1 - x) + (1 - y)]
        for a in range(n):
            for k, quarter in enumerate(others):
                copy(a, k, quarter, c).start()
        for a in range(n):
            for k, quarter in enumerate(others):
                copy(a, k, quarter, 1 - c).wait_recv()
        for a in range(n):
            for k, quarter in enumerate(others):
                copy(a, k, quarter, c).wait_send()

    return pl.pallas_call(
        body, name="forward_halves", in_specs=[ANY] * n, out_specs=[ANY] * n,
        out_shape=[jax.ShapeDtypeStruct(b.shape, b.dtype) for b in bufs],
        input_output_aliases={a: a for a in range(n)},
        scratch_shapes=[pltpu.SemaphoreType.DMA((n, 3)), pltpu.SemaphoreType.DMA((n, 3))],
        compiler_params=pltpu.CompilerParams(has_side_effects=True),
    )(*bufs)


def _exchange_copies(grads, zones, send, recv):
    x, y, c = _place()
    copies = []
    for a in range(len(grads)):
        rows = grads[a].shape[1]
        for k, (fx, fy, fc) in enumerate(_flips()):
            px, py, pc = x ^ fx, y ^ fy, c ^ fc
            j = a * (N_DEV - 1) + k
            copies.append(pltpu.make_async_remote_copy(
                src_ref=grads[a].at[2 * px + py, _half(rows, pc)], dst_ref=zones[a].at[k],
                send_sem=send.at[j], recv_sem=recv.at[j], device_id=(px, py, pc), device_id_type=MESH))
    return copies


def _exchange_start(name, grads):
    n = len(grads)
    zones = [lax.empty((N_DEV - 1, g.shape[1] // 2, g.shape[2]), g.dtype) for g in grads]

    def body(*refs):
        for cp in _exchange_copies(refs[:n], refs[n:2 * n], refs[2 * n], refs[2 * n + 1]):
            cp.start()
        refs[-1][...] = jnp.zeros_like(refs[-1])

    outs = pl.pallas_call(
        body, name=name, in_specs=[HBM] * (2 * n),
        out_specs=(SEM, SEM) + (HBM,) * (2 * n) + (pl.BlockSpec(memory_space=pltpu.VMEM),),
        out_shape=(pltpu.SemaphoreType.DMA((n * (N_DEV - 1),)), pltpu.SemaphoreType.DMA((n * (N_DEV - 1),)))
        + tuple(pltpu.HBM(v.shape, v.dtype) for v in list(grads) + zones) + (jax.ShapeDtypeStruct((8, 128), F32),),
        input_output_aliases={a: 2 + a for a in range(2 * n)},
        compiler_params=pltpu.CompilerParams(has_side_effects=DATAFLOW),
    )(*[pltpu.with_memory_space_constraint(v, pltpu.HBM) for v in list(grads) + zones])
    return outs[:-1], outs[-1]


def _exchange_wait(name, started, after):
    send, recv = started[0], started[1]
    n = (len(started) - 2) // 2

    def body(*refs):
        for cp in _exchange_copies(refs[:n], refs[n:2 * n], refs[2 * n], refs[2 * n + 1]):
            cp.wait_send()
            cp.wait_recv()

    outs = pl.pallas_call(
        body, name=name, in_specs=[HBM] * (2 * n) + [SEM, SEM, ANY], out_specs=(HBM,) * (2 * n),
        out_shape=tuple(pltpu.HBM(v.shape, v.dtype) for v in started[2:]),
        input_output_aliases={a: a for a in range(2 * n)},
        compiler_params=pltpu.CompilerParams(has_side_effects=DATAFLOW),
    )(*started[2:], send, recv, after)
    return outs[:n], outs[n:]


def _share_halves(fulls):
    n = len(fulls)
    index = [(a, l) for a in range(n) for l in range(fulls[a].shape[0])]

    def body(*refs):
        outs = refs[n:2 * n]
        send, recv = refs[2 * n:]
        x, y, c = _place()

        def copy(j, core):
            a, l = index[j]
            part = outs[a].at[l, _half(outs[a].shape[1], core)]
            return pltpu.make_async_remote_copy(
                src_ref=part, dst_ref=part, send_sem=send.at[j], recv_sem=recv.at[j],
                device_id=(x, y, 1 - c), device_id_type=MESH)

        for j in range(len(index)):
            copy(j, c).start()
        for j in range(len(index)):
            copy(j, 1 - c).wait_recv()
        for j in range(len(index)):
            copy(j, c).wait_send()

    return pl.pallas_call(
        body, name="share_halves", in_specs=[ANY] * n, out_specs=[ANY] * n,
        out_shape=[jax.ShapeDtypeStruct(f.shape, f.dtype) for f in fulls],
        input_output_aliases={a: a for a in range(n)},
        scratch_shapes=[pltpu.SemaphoreType.DMA((len(index),)), pltpu.SemaphoreType.DMA((len(index),))],
        compiler_params=pltpu.CompilerParams(has_side_effects=True),
    )(*fulls)


def _allgather_small(v):
    def body(v_ref, o_ref, send, recv, lsem):
        x, y, c = _place()
        me = 4 * x + 2 * y + c
        own = pltpu.make_async_copy(v_ref, o_ref.at[me], lsem)
        own.start()
        sends = []
        for k, (fx, fy, fc) in enumerate(_flips()):
            cp = pltpu.make_async_remote_copy(
                src_ref=v_ref, dst_ref=o_ref.at[me], send_sem=send.at[k], recv_sem=recv.at[k],
                device_id=(x ^ fx, y ^ fy, c ^ fc), device_id_type=MESH)
            cp.start()
            sends.append(cp)
        for k, (fx, fy, fc) in enumerate(_flips()):
            px, py, pc = x ^ fx, y ^ fy, c ^ fc
            pltpu.make_async_remote_copy(
                src_ref=v_ref, dst_ref=o_ref.at[4 * px + 2 * py + pc], send_sem=send.at[k], recv_sem=recv.at[k],
                device_id=(px, py, pc), device_id_type=MESH).wait_recv()
        for cp in sends:
            cp.wait_send()
        own.wait()

    return pl.pallas_call(
        body, name="allgather_small", in_specs=[ANY], out_specs=ANY,
        out_shape=jax.ShapeDtypeStruct((N_DEV,) + v.shape, v.dtype),
        scratch_shapes=[pltpu.SemaphoreType.DMA((7,)), pltpu.SemaphoreType.DMA((7,)), pltpu.SemaphoreType.DMA(())],
        compiler_params=pltpu.CompilerParams(has_side_effects=True),
    )(v)


def _mm_col_fwd(name, a, w4, l, cb, out_dtype, parts, epi_act=None, w4b=None, tm_pref=512, after=()):
    T, K = a.shape
    cs = w4.shape[3]
    N = N_SHARD * cs
    tm = _pick(T, tm_pref, 8)
    nps = cs // cb
    npp = (N // parts) // cb
    a_spec = _spec((tm, K), lambda j, i, k: (i, 0))
    b_spec = _spec((None, None, K, cb), lambda j, i, k: (j // nps, l, 0, j % nps))
    if parts == 1:
        o_spec = _spec((tm, cb), lambda j, i, k: (i, j))
        o_shape = (T, N)
    else:
        o_spec = _spec((None, tm, cb), lambda j, i, k: (j // npp, i, j % npp))
        o_shape = (parts, T, N // parts)
    b_ops = [(w4, b_spec)] if w4b is None else [(w4, b_spec), (w4b, b_spec)]
    dots = [(0, 0, 0)] if w4b is None else [(0, 0, 0), (0, 1, 1)]
    out_dtypes = (out_dtype,) if w4b is None else (F32, F32, out_dtype)
    return _mm(name, (N // cb, T // tm, 1), [(a, a_spec)], b_ops, dots, NN, [(tm, cb)] * len(b_ops),
               [(o_shape, dt, o_spec) for dt in out_dtypes], epi=epi_act, after=after)


def _mm_row_fwd(name, a, w4, l, res):
    T, K = a.shape
    rs, N = w4.shape[2], w4.shape[3]
    tm = _pick(T, 512, 8)
    tn = _pick(N, 1024, 128)
    a_spec = _spec((tm, rs), lambda j, i, k: (i, k))
    b_spec = _spec((None, None, rs, tn), lambda j, i, k: (k, l, 0, j))
    o_spec = _spec((tm, tn), lambda j, i, k: (i, j))
    return _mm(name, (N // tn, T // tm, N_SHARD), [(a, a_spec)], [(w4, b_spec)], [(0, 0, 0)], NN, [(tm, tn)],
               [((T, N), F32, o_spec)], epi=lambda accs, ex: [accs[0] + ex[0]], extras=[(res, o_spec)])


def _mm_row_bwd_data(name, dy, w4, l, out_dtype=BF16, epi=None, extras=(), n_out=1, tm_pref=512, after=()):
    T, N = dy.shape
    rs = w4.shape[2]
    K = N_SHARD * rs
    tm = _pick(T, tm_pref, 8)
    a_spec = _spec((tm, N), lambda j, i, k: (i, 0))
    b_spec = _spec((None, None, rs, N), lambda j, i, k: (j, l, 0, 0))
    o_spec = _spec((tm, rs), lambda j, i, k: (i, j))
    return _mm(name, (N_SHARD, T // tm, 1), [(dy, a_spec)], [(w4, b_spec)], [(0, 0, 0)], NT, [(tm, rs)],
               [((T, K), out_dtype, o_spec)] * n_out, epi=epi, extras=[(e, o_spec) for e in extras], after=after)


def _mm_row_bwd_weight(name, a, dy, rs):
    T, K = a.shape
    N = dy.shape[1]
    tk = _pick(T, 512, 8)
    tn = _pick(N, 1024, 128)
    a_spec = _spec((tk, rs), lambda i, j, k: (k, i))
    b_spec = _spec((tk, tn), lambda i, j, k: (k, j))
    o_spec = _spec((None, rs, tn), lambda i, j, k: (i, 0, j))
    return _mm(name, (N_SHARD, N // tn, T // tk), [(a, a_spec)], [(dy, b_spec)], [(0, 0, 0)], TN, [(rs, tn)],
               [((N_SHARD, rs, N), BF16, o_spec)])[0]


def _mm_col_bwd_data(name, dys, w4s, l, cb, parts, tm_pref=512):
    T = dys[0].shape[-2]
    K, cs = w4s[0].shape[2], w4s[0].shape[3]
    N = N_SHARD * cs
    tm = _pick(T, tm_pref, 8)
    nps = cs // cb
    npp = (N // parts) // cb
    if parts == 1:
        a_spec = _spec((tm, cb), lambda i, j, k: (i, k))
    else:
        a_spec = _spec((None, tm, cb), lambda i, j, k: (k // npp, i, k % npp))
    b_spec = _spec((None, None, K, cb), lambda i, j, k: (k // nps, l, 0, k % nps))
    o_spec = _spec((tm, K), lambda i, j, k: (i, 0))
    return _mm(name, (T // tm, 1, N // cb), [(d, a_spec) for d in dys], [(w, b_spec) for w in w4s],
               [(p, p, 0) for p in range(len(dys))], NT, [(tm, K)], [((T, K), F32, o_spec)])[0]


def _mm_col_bwd_weight(name, a, dys, cs, cb, parts, tm_pref=512):
    T, K = a.shape
    N = N_SHARD * cs
    tk = _pick(T, 512, 8)
    tm = _pick(K, tm_pref, 128)
    nps = cs // cb
    npp = (N // parts) // cb
    a_spec = _spec((tk, tm), lambda i, j, k: (k, i))
    if parts == 1:
        b_spec = _spec((tk, cb), lambda i, j, k: (k, j))
    else:
        b_spec = _spec((None, tk, cb), lambda i, j, k: (j // npp, k, j % npp))
    o_spec = _spec((None, tm, cb), lambda i, j, k: (j // nps, i, j % nps))
    nd = len(dys)
    return _mm(name, (K // tm, N // cb, T // tk), [(a, a_spec)], [(d, b_spec) for d in dys],
               [(0, p, p) for p in range(nd)], TN, [(tm, cb)] * nd, [((N_SHARD, K, cs), BF16, o_spec)] * nd)


def _swiglu(accs, _):
    g, up = accs
    return [g, up, g * jax.nn.sigmoid(g) * up]


def _swiglu_bwd(accs, ex):
    da = accs[0]
    g = ex[0].astype(F32)
    up = ex[1].astype(F32)
    s = jax.nn.sigmoid(g)
    return [da * up * (s * (1.0 + g * (1.0 - s))), da * (g * s)]


def _ffn_fwd(x1, g_ffn, wg4, wu4, wd4, l):
    h2 = _rmsnorm_fwd(x1, g_ffn, BF16)
    fs = wg4.shape[3]
    gate, up, act = _mm_col_fwd("ffn_up", h2, wg4, l, fs, BF16, 1, epi_act=_swiglu, w4b=wu4, tm_pref=256)
    x2 = _mm_row_fwd("ffn_down", act, wd4, l, x1)[0]
    return x2, (x1, h2, gate, up, act)


def _ffn_bwd(dx2, dx2b, saved, g_ffn, wg4, wu4, wd4, l, after):
    x1, h2, gate, up, act = saved
    fs = wg4.shape[3]
    dgate, dup = _mm_row_bwd_data("ffn_down_bwd_data", dx2b, wd4, l, epi=_swiglu_bwd, extras=(gate, up), n_out=2,
                                  tm_pref=256, after=after)
    g_down = _mm_row_bwd_weight("ffn_down_bwd_weight", act, dx2b, fs)
    dh2 = _mm_col_bwd_data("ffn_up_bwd_data", [dgate, dup], [wg4, wu4], l, fs, 1, tm_pref=256)
    g_gate, g_up = _mm_col_bwd_weight("ffn_up_bwd_weight", h2, [dgate, dup], fs, fs, 1)
    dx1, dx1b, dg = _rmsnorm_bwd(x1, g_ffn, dh2, dx2)
    return dx1, dx1b, dg, g_gate, g_up, g_down


def _sb_fwd(x, g_mix, gq, gk, wqkv4, wo4, l, after=()):
    D = x.shape[1]
    h = _rmsnorm_fwd(x, g_mix, BF16)
    qkv = _mm_col_fwd("qkv_proj", h, wqkv4, l, D // 4, F32, 3, after=after)[0]
    qs, kn, vb = _headnorm_fwd(qkv, gq, gk)
    o, tot = _attn_fwd(qs, kn, vb)
    x1 = _mm_row_fwd("attn_out", o, wo4, l, x)[0]
    return x1, (x, h, qkv, qs, kn, vb, o, tot)


def _sb_bwd(dx1, dx1b, saved, g_mix, gq, gk, wqkv4, wo4, l, after=()):
    x, h, qkv, qs, kn, vb, o, tot = saved
    D = x.shape[1]
    do = _mm_row_bwd_data("attn_out_bwd_data", dx1b, wo4, l, after=after)[0]
    g_wo = _mm_row_bwd_weight("attn_out_bwd_weight", o, dx1b, wo4.shape[2])
    dqs, dkn, dv = _attn_bwd(qs, kn, vb, tot, do)
    dqkv, dgq, dgk = _headnorm_bwd(qkv, gq, gk, dqs, dkn, dv)
    dh = _mm_col_bwd_data("qkv_bwd_data", [dqkv], [wqkv4], l, D // 4, 3)
    g_wqkv = _mm_col_bwd_weight("qkv_bwd_weight", h, [dqkv], wqkv4.shape[3], D // 4, 3, tm_pref=2048)[0]
    dx, dxb, dg = _rmsnorm_bwd(x, g_mix, dh, dx1)
    return dx, dxb, dg, dgq, dgk, g_wqkv, g_wo


def _pool_mix_fwd(x, g_mix, wp4, scale):
    T, D = x.shape
    C = D // N_POOL_GROUPS
    rq = C // N_SHARD
    h = _rmsnorm_fwd(x, g_mix, F32)
    p = _pool_fwd(h)
    tm = _pick(T, 512, 8)
    a_spec = _spec((tm, rq), lambda g, i, k: (i, g * N_SHARD + k))
    b_spec = _spec((None, None, rq, C), lambda g, i, k: (k, 0, g, 0))
    o_spec = _spec((tm, C), lambda g, i, k: (i, g))
    s_spec = _spec((1, C), lambda g, i, k: (0, g))
    x1, ypre = _mm("pool_mix", (N_POOL_GROUPS, T // tm, N_SHARD), [(p, a_spec)], [(wp4, b_spec)], [(0, 0, 0)], NN,
                   [(tm, C)], [((T, D), F32, o_spec), ((T, D), BF16, o_spec)],
                   epi=lambda accs, ex: [ex[0] + accs[0] * ex[1], accs[0]], extras=[(x, o_spec), (scale, s_spec)])
    return x1, (x, h, p, ypre)


def _pool_mix_bwd(dx1, dx1b, saved, g_mix, wp4, scale, after=()):
    x, h, p, ypre = saved
    T, D = x.shape
    C = D // N_POOL_GROUPS
    rq = C // N_SHARD
    dys, dscale = _pool_scale_bwd(dx1, ypre, scale)
    tm = _pick(T, 512, 8)
    dp = _mm("pool_mix_bwd_data", (N_POOL_GROUPS * N_SHARD, T // tm, 1),
             [(dys, _spec((tm, C), lambda j, i, k: (i, j // N_SHARD)))],
             [(wp4, _spec((None, None, rq, C), lambda j, i, k: (j % N_SHARD, 0, j // N_SHARD, 0)))],
             [(0, 0, 0)], NT, [(tm, rq)], [((T, D), F32, _spec((tm, rq), lambda j, i, k: (i, j)))], after=after)[0]
    tk = _pick(T, 512, 8)
    g_wp = _mm("pool_mix_bwd_weight", (N_POOL_GROUPS * N_SHARD, 1, T // tk),
               [(p, _spec((tk, rq), lambda j, n, k: (k, j)))],
               [(dys, _spec((tk, C), lambda j, n, k: (k, j // N_SHARD)))],
               [(0, 0, 0)], TN, [(rq, C)],
               [((N_SHARD, N_POOL_GROUPS * rq, C), BF16,
                 _spec((None, rq, C), lambda j, n, k: (j % N_SHARD, j // N_SHARD, 0)))])[0]
    dh = _pool_bwd(dp)
    dx, dxb, dg = _rmsnorm_bwd(x, g_mix, dh, dx1)
    return dx, dxb, dg, dscale, g_wp


def _conv_mix_fwd(x, g_mix, win4, taps4, wout4):
    D = x.shape[1]
    h = _rmsnorm_fwd(x, g_mix, BF16)
    bcx = _mm_col_fwd("conv_in", h, win4, 0, D // 4, F32, 3)[0]
    q = _conv_fwd(bcx, taps4)
    x1 = _mm_row_fwd("conv_out", q, wout4, 0, x)[0]
    return x1, (x, h, bcx, q)


def _conv_mix_bwd(dx1, dx1b, saved, g_mix, win4, taps4, wout4, after=()):
    x, h, bcx, q = saved
    D = x.shape[1]
    dq = _mm_row_bwd_data("conv_out_bwd_data", dx1b, wout4, 0, out_dtype=F32, after=after)[0]
    g_wout = _mm_row_bwd_weight("conv_out_bwd_weight", q, dx1b, wout4.shape[2])
    dbcx, dtaps = _conv_bwd(dq, bcx, taps4)
    dh = _mm_col_bwd_data("conv_in_bwd_data", [dbcx], [win4], 0, D // 4, 3)
    g_win = _mm_col_bwd_weight("conv_in_bwd_weight", h, [dbcx], win4.shape[3], D // 4, 3, tm_pref=2048)[0]
    dx, dxb, dg = _rmsnorm_bwd(x, g_mix, dh, dx1)
    return dx, dxb, dg, dtaps, g_wout, g_win


def _rows2d(w):
    return w.reshape(-1, w.shape[-1])


def _pad_rows(v, rows):
    return jnp.pad(v, ((0, rows - v.shape[0]), (0, 0)))


def kernel(x, norm_mix_g, norm_ffn_g, sb_w_qkv, sb_g_q, sb_g_k, sb_w_o, pool_w, pool_scale, conv_w_in, conv_w, conv_w_out, ffn_w_gate, ffn_w_up, ffn_w_down, loss_target, m_norm_mix_g, m_norm_ffn_g, m_sb_w_qkv, m_sb_g_q, m_sb_g_k, m_sb_w_o, m_pool_w, m_pool_scale, m_conv_w_in, m_conv_w, m_conv_w_out, m_ffn_w_gate, m_ffn_w_up, m_ffn_w_down, v_norm_mix_g, v_norm_ffn_g, v_sb_w_qkv, v_sb_g_q, v_sb_g_k, v_sb_w_o, v_pool_w, v_pool_scale, v_conv_w_in, v_conv_w, v_conv_w_out, v_ffn_w_gate, v_ffn_w_up, v_ffn_w_down):
    T, D = x.shape[1], x.shape[2]
    depth = norm_mix_g.shape[0]
    big = dict(sb_w_qkv=(sb_w_qkv, m_sb_w_qkv, v_sb_w_qkv), sb_w_o=(sb_w_o, m_sb_w_o, v_sb_w_o),
               pool_w=(pool_w, m_pool_w, v_pool_w), conv_w_in=(conv_w_in, m_conv_w_in, v_conv_w_in),
               conv_w_out=(conv_w_out, m_conv_w_out, v_conv_w_out), ffn_w_gate=(ffn_w_gate, m_ffn_w_gate, v_ffn_w_gate),
               ffn_w_up=(ffn_w_up, m_ffn_w_up, v_ffn_w_up), ffn_w_down=(ffn_w_down, m_ffn_w_down, v_ffn_w_down))
    names = list(big)

    cs_conv = conv_w.shape[2]
    taps_local = _pad_rows(conv_w[0], 16)
    first_names = ["sb_w_qkv", "sb_w_o", "ffn_w_gate", "ffn_w_up", "ffn_w_down"]
    per_layer = {n: _rows2d(big[n][0]).shape[0] // big[n][0].shape[0] for n in names}
    first = _allgather_shards([_cast_into_slot(_rows2d(big[n][0]), BF16, 0, per_layer[n]) for n in first_names])
    rest_bufs = []
    for n in names:
        skip = per_layer[n] if n in first_names else 0
        rest_bufs.append(_cast_into_slot(_rows2d(big[n][0]), BF16, skip, _rows2d(big[n][0]).shape[0] - skip))
    rest_bufs.append(_cast_into_slot(taps_local, F32))
    gathering, token = _gather_start("gather_start", rest_bufs, first[0])

    def as_layers(g4, layers):
        return g4.reshape(N_SHARD, layers, g4.shape[1] // layers, g4.shape[2])

    w_first = {n: as_layers(g4, 1) for n, g4 in zip(first_names, first)}
    w_rest, taps4 = {}, None

    def weights(layer, *tensors):
        if layer == 0 and tensors[0] in first_names:
            return [w_first[t] for t in tensors] + [0]
        return [w_rest[t] for t in tensors] + [layer - 1 if tensors[0] in first_names else layer]

    xs = x.reshape(T, D)
    saved = []
    for i in range(depth):
        kind, j = i % 3, i // 3
        g_mix = norm_mix_g[i:i + 1]
        if i == 1:
            gathered = _forward_halves(_gather_wait("gather_wait", gathering, xs))
            for n, g4 in zip(names, gathered[:-1]):
                w_rest[n] = as_layers(g4, big[n][0].shape[0] - (1 if n in first_names else 0))
            taps4 = gathered[-1]
        if kind == 0:
            xs, sv = _sb_fwd(xs, g_mix, sb_g_q[j:j + 1], sb_g_k[j:j + 1], *weights(j, "sb_w_qkv", "sb_w_o"),
                             after=(token,) if i == 0 else ())
        elif kind == 1:
            xs, sv = _pool_mix_fwd(xs, g_mix, w_rest["pool_w"], pool_scale[j:j + 1])
        else:
            xs, sv = _conv_mix_fwd(xs, g_mix, w_rest["conv_w_in"], taps4, w_rest["conv_w_out"])
        xs, sf = _ffn_fwd(xs, norm_ffn_g[i:i + 1], *weights(i, "ffn_w_gate", "ffn_w_up", "ffn_w_down"))
        saved.append((sv, sf))

    dxs, dxb, err2 = _loss_head(xs, loss_target.reshape(T, D))
    loss = lax.psum(0.5 * jnp.sum(err2) / D, ("x", "y", "c"))
    started = []
    small = {}
    token = ()

    def exchange(name, grads, layout):
        handle, tok = _exchange_start("exchange_start_" + name, grads)
        started.append((name, handle, layout))
        return (tok,)

    for i in reversed(range(depth)):
        kind, j = i % 3, i // 3
        sv, sf = saved[i]
        dxs, dxb, dg, g_gate, g_up, g_down = _ffn_bwd(dxs, dxb, sf, norm_ffn_g[i:i + 1],
                                                      *weights(i, "ffn_w_gate", "ffn_w_up", "ffn_w_down"), token)
        small[("norm_ffn_g", i)] = dg
        token = exchange(f"ffn_{i}", [g_gate, g_up, g_down], [("ffn_w_gate", i), ("ffn_w_up", i), ("ffn_w_down", i)])
        g_mix = norm_mix_g[i:i + 1]
        if kind == 0:
            dxs, dxb, dg, dgq, dgk, g_wqkv, g_wo = _sb_bwd(dxs, dxb, sv, g_mix, sb_g_q[j:j + 1], sb_g_k[j:j + 1],
                                                          *weights(j, "sb_w_qkv", "sb_w_o"), after=token)
            small[("sb_g_q", j)], small[("sb_g_k", j)] = dgq, dgk
            token = exchange(f"mix_{i}", [g_wqkv, g_wo], [("sb_w_qkv", j), ("sb_w_o", j)])
        elif kind == 1:
            dxs, dxb, dg, dscale, g_wp = _pool_mix_bwd(dxs, dxb, sv, g_mix, w_rest["pool_w"], pool_scale[j:j + 1],
                                                       after=token)
            small[("pool_scale", j)] = dscale
            token = exchange(f"mix_{i}", [g_wp], [("pool_w", j)])
        else:
            dxs, dxb, dg, dtaps, g_wout, g_win = _conv_mix_bwd(dxs, dxb, sv, g_mix, w_rest["conv_w_in"], taps4,
                                                              w_rest["conv_w_out"], after=token)
            small[("conv_w", j)] = dtaps
            token = exchange(f"mix_{i}", [g_win, g_wout], [("conv_w_in", j), ("conv_w_out", j)])
        small[("norm_mix_g", i)] = dg
    grad_x = dxs.reshape(x.shape)

    full = {}
    done = dg
    for name, handle, layout in started:
        grads, zones = _exchange_wait("exchange_wait_" + name, handle, done)
        for g, z, (t, l) in zip(grads, zones, layout):
            full[t] = done = _sum_into(z, g, l, big[t][0].shape[0], full.get(t))
    tensors = sorted(full)
    g_big = dict(zip(tensors, _share_halves([full[t] for t in tensors])))

    n_sb = sb_g_q.shape[0]
    gqk = jnp.concatenate([small[(n, j)] for j in range(n_sb) for n in ("sb_g_q", "sb_g_k")], axis=1)
    dtaps = small[("conv_w", 0)]
    taps_full = jnp.concatenate([dtaps[s, :3] for s in range(N_SHARD)], axis=1)
    pack = jnp.concatenate(
        [small[("norm_mix_g", i)] for i in range(depth)] + [small[("norm_ffn_g", i)] for i in range(depth)]
        + [small[("pool_scale", 0)], jnp.pad(gqk, ((0, 0), (0, D - gqk.shape[1]))), taps_full], axis=0)
    pack = _pad_rows(pack, SMALL_ROWS)
    g_small = _sum_devices(_allgather_small(pack))
    mine = 2 * lax.axis_index("x") + lax.axis_index("y")
    g_taps = lax.dynamic_slice(g_small, (2 * depth + 2, mine * cs_conv), (3, cs_conv))

    def pack_small(norm_mix, norm_ffn, scale, gq, gk, taps):
        qk = jnp.concatenate([v[j:j + 1] for j in range(n_sb) for v in (gq, gk)], axis=1)
        rows = jnp.concatenate([norm_mix, norm_ffn, scale, jnp.pad(qk, ((0, 0), (0, D - qk.shape[1]))),
                                jnp.pad(taps[0], ((0, 0), (0, D - cs_conv)))], axis=0)
        return _pad_rows(rows, SMALL_ROWS)

    g_pack = jnp.concatenate([g_small[:2 * depth + 2], jnp.pad(g_taps, ((0, 0), (0, D - cs_conv))),
                              jnp.zeros((SMALL_ROWS - 2 * depth - 5, D), F32)], axis=0)
    w_pack = pack_small(norm_mix_g, norm_ffn_g, pool_scale, sb_g_q, sb_g_k, conv_w)
    m_pack = pack_small(m_norm_mix_g, m_norm_ffn_g, m_pool_scale, m_sb_g_q, m_sb_g_k, m_conv_w)
    v_pack = pack_small(v_norm_mix_g, v_norm_ffn_g, v_pool_scale, v_sb_g_q, v_sb_g_k, v_conv_w)
    small_out = (g_pack,) + tuple(_adamw(w_pack, g_pack, m_pack, v_pack))

    def unpack_small(p):
        qk = p[2 * depth + 1]
        gq = jnp.stack([qk[(2 * j) * HEAD_DIM:(2 * j + 1) * HEAD_DIM] for j in range(n_sb)])
        gk = jnp.stack([qk[(2 * j + 1) * HEAD_DIM:(2 * j + 2) * HEAD_DIM] for j in range(n_sb)])
        return dict(norm_mix_g=p[:depth], norm_ffn_g=p[depth:2 * depth], pool_scale=p[2 * depth:2 * depth + 1],
                    sb_g_q=gq, sb_g_k=gk, conv_w=p[2 * depth + 2:2 * depth + 5, :cs_conv][None])

    results = [unpack_small(p) for p in small_out]
    for n in names:
        w, m, v = big[n]
        g = _rows2d(g_big[n])
        outs = (g,) + tuple(_adamw(_rows2d(w), g, _rows2d(m), _rows2d(v)))
        for r, o in zip(results, outs):
            r[n] = o.reshape(w.shape)

    order = ["norm_mix_g", "norm_ffn_g", "sb_w_qkv", "sb_g_q", "sb_g_k", "sb_w_o", "pool_w", "pool_scale",
             "conv_w_in", "conv_w", "conv_w_out", "ffn_w_gate", "ffn_w_up", "ffn_w_down"]
    return (loss, grad_x) + tuple(r[n] for r in results for n in order)
```

```python
import math

import jax
import jax.numpy as jnp
from jax import lax
from jax.experimental import pallas as pl
from jax.experimental.pallas import tpu as pltpu

F32 = jnp.float32
BF16 = jnp.bfloat16

HEAD_DIM = 128
N_POOL_GROUPS = 4
EPS = 1e-6
N_SHARD = 4
N_DEV = 8
VMEM_LIMIT_BYTES = 56 * 2**20
KEY_BLOCK = 128
ATTN_ROWS = 512
ATTN_GROUP_FWD = 4
ATTN_GROUP_BWD = 4
SMALL_ROWS = 16

ADAM_LR = 0.001
ADAM_B1 = 0.9
ADAM_B2 = 0.999
ADAM_EPS = 1e-08
ADAM_WD = 0.01
ADAM_STEP = 10

MESH = pl.DeviceIdType.MESH
ANY = pl.BlockSpec(memory_space=pl.ANY)


def _params(*sem):
    return pltpu.CompilerParams(dimension_semantics=sem, vmem_limit_bytes=VMEM_LIMIT_BYTES)


def _pick(n, pref, unit):
    t = (min(pref, n) // unit) * unit
    while n % t:
        t -= unit
    return t


NN = ((1,), (0,))
NT = ((1,), (1,))
TN = ((0,), (0,))


def _mm(name, grid, a_ops, b_ops, dots, dims, acc_shapes, outs, epi=None, extras=(), after=()):
    na, nb, ne, no, nacc = len(a_ops), len(b_ops), len(extras), len(outs), len(acc_shapes)
    nk = grid[2]
    n_in = na + nb + ne + len(after)

    def body(*refs):
        a_refs, b_refs = refs[:na], refs[na:na + nb]
        e_refs = refs[na + nb:na + nb + ne]
        o_refs = refs[n_in:n_in + no]
        acc_refs = refs[n_in + no:]

        def partial_sums():
            sums = [None] * nacc
            for ai, bi, ci in dots:
                d = lax.dot_general(a_refs[ai][...].astype(BF16), b_refs[bi][...].astype(BF16),
                                    (dims, ((), ())), preferred_element_type=F32)
                sums[ci] = d if sums[ci] is None else sums[ci] + d
            return sums

        def finish(accs):
            res = epi(accs, [e[...] for e in e_refs]) if epi is not None else accs
            for o, r in zip(o_refs, res):
                o[...] = r.astype(o.dtype)

        if nk == 1:
            finish(partial_sums())
            return
        k = pl.program_id(2)

        @pl.when(k == 0)
        def _():
            for acc, s in zip(acc_refs, partial_sums()):
                acc[...] = s

        @pl.when(k > 0)
        def _():
            for acc, s in zip(acc_refs, partial_sums()):
                acc[...] += s

        @pl.when(k == nk - 1)
        def _():
            finish([acc[...] for acc in acc_refs])

    ops = list(a_ops) + list(b_ops) + list(extras) + [(t, ANY) for t in after]
    return pl.pallas_call(
        body, name=name, grid=grid,
        in_specs=[s for _, s in ops],
        out_specs=[s for _, _, s in outs],
        out_shape=[jax.ShapeDtypeStruct(sh, dt) for sh, dt, _ in outs],
        scratch_shapes=[pltpu.VMEM(s, F32) for s in acc_shapes] if nk > 1 else [],
        compiler_params=_params("parallel", "parallel", "arbitrary"),
    )(*[a for a, _ in ops])


def _spec(block, index):
    return pl.BlockSpec(block, index)


def _rmsnorm_fwd(x, g, out_dtype):
    T, D = x.shape
    tm = _pick(T, 256, 8)

    def body(x_ref, g_ref, o_ref):
        xv = x_ref[...]
        r = lax.rsqrt(jnp.mean(xv * xv, axis=-1, keepdims=True) + EPS)
        o_ref[...] = (xv * r * g_ref[...]).astype(o_ref.dtype)

    row = _spec((tm, D), lambda i: (i, 0))
    return pl.pallas_call(
        body, name="rmsnorm_fwd", grid=(T // tm,),
        in_specs=[row, _spec((1, D), lambda i: (0, 0))], out_specs=row,
        out_shape=jax.ShapeDtypeStruct((T, D), out_dtype), compiler_params=_params("parallel"),
    )(x, g)


def _rmsnorm_bwd(x, g, dh, dres):
    T, D = x.shape
    tm = _pick(T, 256, 8)

    def body(x_ref, g_ref, dh_ref, dres_ref, dx_ref, dxb_ref, dg_ref):
        xv = x_ref[...]
        dhv = dh_ref[...].astype(F32)
        r = lax.rsqrt(jnp.mean(xv * xv, axis=-1, keepdims=True) + EPS)
        xh = xv * r
        dxh = dhv * g_ref[...]
        dx = r * (dxh - xh * jnp.mean(dxh * xh, axis=-1, keepdims=True)) + dres_ref[...]
        dx_ref[...] = dx
        dxb_ref[...] = dx.astype(BF16)
        part = jnp.sum(dhv * xh, axis=0, keepdims=True)

        @pl.when(pl.program_id(0) == 0)
        def _():
            dg_ref[...] = part

        @pl.when(pl.program_id(0) > 0)
        def _():
            dg_ref[...] += part

    row = _spec((tm, D), lambda i: (i, 0))
    vec = _spec((1, D), lambda i: (0, 0))
    return pl.pallas_call(
        body, name="rmsnorm_bwd", grid=(T // tm,),
        in_specs=[row, vec, row, row], out_specs=[row, row, vec],
        out_shape=[jax.ShapeDtypeStruct((T, D), F32), jax.ShapeDtypeStruct((T, D), BF16),
                   jax.ShapeDtypeStruct((1, D), F32)],
        compiler_params=_params("arbitrary"),
    )(x, g, dh, dres)


def _loss_head(y, target):
    T, D = y.shape
    tm = _pick(T, 256, 8)

    def body(y_ref, t_ref, dy_ref, dyb_ref, l_ref):
        err = y_ref[...] - t_ref[...]
        dy = err * (1.0 / D)
        dy_ref[...] = dy
        dyb_ref[...] = dy.astype(BF16)
        part = jnp.sum(err * err, axis=0, keepdims=True)

        @pl.when(pl.program_id(0) == 0)
        def _():
            l_ref[...] = part

        @pl.when(pl.program_id(0) > 0)
        def _():
            l_ref[...] += part

    row = _spec((tm, D), lambda i: (i, 0))
    vec = _spec((1, D), lambda i: (0, 0))
    return pl.pallas_call(
        body, name="loss_head", grid=(T // tm,),
        in_specs=[row, row], out_specs=[row, row, vec],
        out_shape=[jax.ShapeDtypeStruct((T, D), F32), jax.ShapeDtypeStruct((T, D), BF16),
                   jax.ShapeDtypeStruct((1, D), F32)],
        compiler_params=_params("arbitrary"),
    )(y, target)


def _headnorm_fwd(qkv, gq, gk):
    _, T, D = qkv.shape
    H = D // HEAD_DIM
    tm = _pick(T, 256, 8)
    scale = HEAD_DIM ** -0.5

    def body(q_ref, k_ref, v_ref, gq_ref, gk_ref, qs_ref, kn_ref, vb_ref):
        for h in range(H):
            sl = slice(h * HEAD_DIM, (h + 1) * HEAD_DIM)
            q = q_ref[:, sl]
            qs_ref[:, sl] = (q * lax.rsqrt(jnp.mean(q * q, axis=-1, keepdims=True) + EPS)
                             * (gq_ref[...] * scale)).astype(BF16)
            k = k_ref[:, sl]
            kn_ref[:, sl] = (k * lax.rsqrt(jnp.mean(k * k, axis=-1, keepdims=True) + EPS)
                             * gk_ref[...]).astype(BF16)
        vb_ref[...] = v_ref[...].astype(BF16)

    part = lambda p: _spec((None, tm, D), lambda i, p=p: (p, i, 0))
    row = _spec((tm, D), lambda i: (i, 0))
    vec = _spec((1, HEAD_DIM), lambda i: (0, 0))
    return pl.pallas_call(
        body, name="headnorm_fwd", grid=(T // tm,),
        in_specs=[part(0), part(1), part(2), vec, vec], out_specs=[row, row, row],
        out_shape=[jax.ShapeDtypeStruct((T, D), BF16)] * 3, compiler_params=_params("parallel"),
    )(qkv, qkv, qkv, gq, gk)


def _headnorm_bwd(qkv, gq, gk, dqs, dkn, dv):
    _, T, D = qkv.shape
    H = D // HEAD_DIM
    tm = _pick(T, 256, 8)
    scale = HEAD_DIM ** -0.5

    def body(q_ref, k_ref, gq_ref, gk_ref, dqs_ref, dkn_ref, dv_ref, dqkv_ref, dgq_ref, dgk_ref):
        dgq = jnp.zeros((1, HEAD_DIM), F32)
        dgk = jnp.zeros((1, HEAD_DIM), F32)
        for h in range(H):
            sl = slice(h * HEAD_DIM, (h + 1) * HEAD_DIM)
            for src, dsrc, g_ref, sc, p in ((q_ref, dqs_ref, gq_ref, scale, 0), (k_ref, dkn_ref, gk_ref, 1.0, 1)):
                v = src[:, sl]
                r = lax.rsqrt(jnp.mean(v * v, axis=-1, keepdims=True) + EPS)
                vh = v * r
                dn = dsrc[:, sl] * sc
                dvh = dn * g_ref[...]
                dqkv_ref[p, :, sl] = (r * (dvh - vh * jnp.mean(dvh * vh, axis=-1, keepdims=True))).astype(BF16)
                dg = jnp.sum(dn * vh, axis=0, keepdims=True)
                if p == 0:
                    dgq = dgq + dg
                else:
                    dgk = dgk + dg
        dqkv_ref[2] = dv_ref[...].astype(BF16)

        @pl.when(pl.program_id(0) == 0)
        def _():
            dgq_ref[...] = dgq
            dgk_ref[...] = dgk

        @pl.when(pl.program_id(0) > 0)
        def _():
            dgq_ref[...] += dgq
            dgk_ref[...] += dgk

    part = lambda p: _spec((None, tm, D), lambda i, p=p: (p, i, 0))
    row = _spec((tm, D), lambda i: (i, 0))
    vec = _spec((1, HEAD_DIM), lambda i: (0, 0))
    return pl.pallas_call(
        body, name="headnorm_bwd", grid=(T // tm,),
        in_specs=[part(0), part(1), vec, vec, row, row, row],
        out_specs=[_spec((3, tm, D), lambda i: (0, i, 0)), vec, vec],
        out_shape=[jax.ShapeDtypeStruct((3, T, D), BF16), jax.ShapeDtypeStruct((1, HEAD_DIM), F32),
                   jax.ShapeDtypeStruct((1, HEAD_DIM), F32)],
        compiler_params=_params("arbitrary"),
    )(qkv, qkv, gq, gk, dqs, dkn, dv)


def _sum_matrix(prefix):
    r = lax.broadcasted_iota(jnp.int32, (2 * KEY_BLOCK, 2 * KEY_BLOCK), 0) & (KEY_BLOCK - 1)
    c = lax.broadcasted_iota(jnp.int32, (2 * KEY_BLOCK, 2 * KEY_BLOCK), 1)
    tri = (r <= c) if prefix else (r > c)
    return jnp.where(tri | (c >= KEY_BLOCK), 1.0, 0.0).astype(BF16)


def _block_sums(v, u):
    hi = v.astype(BF16)
    lo = (v - hi.astype(F32)).astype(BF16)
    s = jnp.dot(jnp.concatenate([hi, lo], axis=1), u, preferred_element_type=F32)
    return s[:, :KEY_BLOCK], s[:, KEY_BLOCK:]


def _log_terms(z, mask):
    ls = jnp.minimum(z, 0.0) - jnp.log(1.0 + jnp.exp(-jnp.abs(z)))
    lk = ls - z
    if mask is not None:
        lk = jnp.where(mask, lk, 0.0)
    return ls, lk


def _causal_mask(row0, key0, tq):
    t = row0 + lax.broadcasted_iota(jnp.int32, (tq, KEY_BLOCK), 0)
    s = key0 + lax.broadcasted_iota(jnp.int32, (tq, KEY_BLOCK), 1)
    return s < t


def _attn_fwd(qs, kn, vb):
    T, D = qs.shape
    H = D // HEAD_DIM
    ATTN_GROUP = ATTN_GROUP_FWD
    tq = _pick(T, ATTN_ROWS, ATTN_GROUP * KEY_BLOCK)
    nd = tq // KEY_BLOCK

    def body(q_ref, k_ref, v_ref, u_ref, o_ref, tot_ref, acc_ref, run_ref):
        i = pl.program_id(1)
        q = q_ref[...]
        uv = u_ref[...]
        acc_ref[...] = jnp.zeros_like(acc_ref)
        run_ref[...] = jnp.zeros_like(run_ref)

        def group(first, masked):
            k_lo = pl.multiple_of((first - ATTN_GROUP + 1) * KEY_BLOCK, KEY_BLOCK)
            slab = pl.ds(k_lo, ATTN_GROUP * KEY_BLOCK)
            z = lax.dot_general(q, k_ref[slab, :], (NT, ((), ())), preferred_element_type=F32)
            parts = []
            for b in reversed(range(ATTN_GROUP)):
                mask = _causal_mask(i * tq, k_lo + b * KEY_BLOCK, tq) if masked else None
                ls, lk = _log_terms(z[:, b * KEY_BLOCK:(b + 1) * KEY_BLOCK], mask)
                after, rows = _block_sums(lk, uv)
                parts.append((b, mask, ls + after, rows))
            run = run_ref[...]
            a_parts = [None] * ATTN_GROUP
            for b, mask, base, rows in parts:
                a = jnp.exp(base + run)
                if masked:
                    a = jnp.where(mask, a, 0.0)
                a_parts[b] = a.astype(BF16)
                run = run + rows
            acc_ref[...] += jnp.dot(jnp.concatenate(a_parts, axis=1), v_ref[slab, :], preferred_element_type=F32)
            run_ref[...] = run

        for d in range(nd // ATTN_GROUP):
            group(i * nd + nd - 1 - d * ATTN_GROUP, True)

        def below(n, c):
            group(i * nd - 1 - n * ATTN_GROUP, False)
            return c

        lax.fori_loop(0, i * (nd // ATTN_GROUP), below, 0)
        o_ref[...] = acc_ref[...].astype(o_ref.dtype)
        tot_ref[...] = run_ref[...]

    blk = _spec((tq, HEAD_DIM), lambda h, i: (i, h))
    col = _spec((T, HEAD_DIM), lambda h, i: (0, h))
    return pl.pallas_call(
        body, name="attn_fwd", grid=(H, T // tq),
        in_specs=[blk, col, col, _spec((2 * KEY_BLOCK, 2 * KEY_BLOCK), lambda h, i: (0, 0))], out_specs=[blk, blk],
        out_shape=[jax.ShapeDtypeStruct((T, D), BF16), jax.ShapeDtypeStruct((T, D), F32)],
        scratch_shapes=[pltpu.VMEM((tq, HEAD_DIM), F32), pltpu.VMEM((tq, KEY_BLOCK), F32)],
        compiler_params=_params("parallel", "arbitrary"),
    )(qs, kn, vb, _sum_matrix(False))


def _attn_bwd(qs, kn, vb, tot, do):
    T, D = qs.shape
    H = D // HEAD_DIM
    ATTN_GROUP = ATTN_GROUP_BWD
    tq = _pick(T, ATTN_ROWS, ATTN_GROUP * KEY_BLOCK)
    nd = tq // KEY_BLOCK

    def body(q_ref, k_ref, v_ref, tot_ref, do_ref, u_ref, dq_ref, dk_ref, dv_ref, run_ref, grun_ref):
        i = pl.program_id(1)

        @pl.when(i == 0)
        def _():
            dk_ref[...] = jnp.zeros_like(dk_ref)
            dv_ref[...] = jnp.zeros_like(dv_ref)

        q = q_ref[...]
        dov = do_ref[...]
        uv = u_ref[...]
        dq_ref[...] = jnp.zeros_like(dq_ref)
        run_ref[...] = jnp.zeros_like(run_ref)
        grun_ref[...] = jnp.zeros_like(grun_ref)

        def group(first, masked):
            k_lo = pl.multiple_of(first * KEY_BLOCK, KEY_BLOCK)
            slab = pl.ds(k_lo, ATTN_GROUP * KEY_BLOCK)
            ks = k_ref[slab, :]
            z = lax.dot_general(q, ks, (NT, ((), ())), preferred_element_type=F32)
            da = lax.dot_general(dov, v_ref[slab, :], (NT, ((), ())), preferred_element_type=F32)
            parts = []
            for b in range(ATTN_GROUP):
                cols = slice(b * KEY_BLOCK, (b + 1) * KEY_BLOCK)
                mask = _causal_mask(i * tq, k_lo + b * KEY_BLOCK, tq) if masked else None
                ls, lk = _log_terms(z[:, cols], mask)
                upto, rows = _block_sums(lk, uv)
                parts.append((mask, ls, tot_ref[...] - upto, rows, da[:, cols]))
            run = run_ref[...]
            grun = grun_ref[...]
            a_parts, dz_parts = [], []
            for mask, ls, right, rows, dab in parts:
                a = jnp.exp(ls + (right - run))
                if masked:
                    a = jnp.where(mask, a, 0.0)
                g = a * dab
                gupto, grows = _block_sums(g, uv)
                dz = g - jnp.exp(ls) * (grun + gupto)
                if masked:
                    dz = jnp.where(mask, dz, 0.0)
                a_parts.append(a.astype(BF16))
                dz_parts.append(dz.astype(BF16))
                run = run + rows
                grun = grun + grows
            dzs = jnp.concatenate(dz_parts, axis=1)
            dq_ref[...] += jnp.dot(dzs, ks, preferred_element_type=F32)
            dk_ref[slab, :] += lax.dot_general(dzs, q, (TN, ((), ())), preferred_element_type=F32)
            dv_ref[slab, :] += lax.dot_general(jnp.concatenate(a_parts, axis=1), dov, (TN, ((), ())),
                                               preferred_element_type=F32)
            run_ref[...] = run
            grun_ref[...] = grun

        def below(n, c):
            group(n * ATTN_GROUP, False)
            return c

        lax.fori_loop(0, i * (nd // ATTN_GROUP), below, 0)
        for d in range(nd // ATTN_GROUP):
            group(i * nd + d * ATTN_GROUP, True)

    blk = _spec((tq, HEAD_DIM), lambda h, i: (i, h))
    col = _spec((T, HEAD_DIM), lambda h, i: (0, h))
    return pl.pallas_call(
        body, name="attn_bwd", grid=(H, T // tq),
        in_specs=[blk, col, col, blk, blk, _spec((2 * KEY_BLOCK, 2 * KEY_BLOCK), lambda h, i: (0, 0))],
        out_specs=[blk, col, col],
        out_shape=[jax.ShapeDtypeStruct((T, D), F32)] * 3,
        scratch_shapes=[pltpu.VMEM((tq, KEY_BLOCK), F32), pltpu.VMEM((tq, KEY_BLOCK), F32)],
        compiler_params=_params("parallel", "arbitrary"),
    )(qs, kn, vb, tot, do, _sum_matrix(True))


def _shift_down(v, n):
    t = lax.broadcasted_iota(jnp.int32, v.shape, 0)
    return jnp.where(t >= n, pltpu.roll(v, n, 0), 0.0)


def _shift_up(v, n):
    rows = v.shape[0]
    t = lax.broadcasted_iota(jnp.int32, v.shape, 0)
    return jnp.where(t < rows - n, pltpu.roll(v, rows - n, 0), 0.0)


def _pool_window(j, cw, D):
    group = (j * cw) // (D // N_POOL_GROUPS)
    return jnp.left_shift(2, group)


def _pool_count(shape, w):
    t = lax.broadcasted_iota(jnp.int32, shape, 0)
    return jnp.minimum(t + 1, w).astype(F32)


def _pool_fwd(h):
    T, D = h.shape
    cw = min(256, D // N_POOL_GROUPS)

    def body(h_ref, p_ref):
        w = _pool_window(pl.program_id(0), cw, D)
        hv = h_ref[...]
        s = hv
        for n in (1, 2, 4, 8):
            s = jnp.where(n < w, s + _shift_down(s, n), s)
        p_ref[...] = (s / _pool_count(hv.shape, w) - hv).astype(p_ref.dtype)

    slab = _spec((T, cw), lambda j: (0, j))
    return pl.pallas_call(
        body, name="pool_fwd", grid=(D // cw,), in_specs=[slab], out_specs=slab,
        out_shape=jax.ShapeDtypeStruct((T, D), BF16), compiler_params=_params("parallel"),
    )(h)


def _pool_bwd(dp):
    T, D = dp.shape
    cw = min(256, D // N_POOL_GROUPS)

    def body(dp_ref, dh_ref):
        w = _pool_window(pl.program_id(0), cw, D)
        dpv = dp_ref[...]
        s = dpv / _pool_count(dpv.shape, w)
        for n in (1, 2, 4, 8):
            s = jnp.where(n < w, s + _shift_up(s, n), s)
        dh_ref[...] = (s - dpv).astype(dh_ref.dtype)

    slab = _spec((T, cw), lambda j: (0, j))
    return pl.pallas_call(
        body, name="pool_bwd", grid=(D // cw,), in_specs=[slab], out_specs=slab,
        out_shape=jax.ShapeDtypeStruct((T, D), F32), compiler_params=_params("parallel"),
    )(dp)


def _pool_scale_bwd(dx, ypre, scale):
    T, D = dx.shape
    tm = _pick(T, 256, 8)

    def body(dx_ref, y_ref, s_ref, dys_ref, ds_ref):
        dxv = dx_ref[...]
        dys_ref[...] = (dxv * s_ref[...]).astype(BF16)
        part = jnp.sum(dxv * y_ref[...].astype(F32), axis=0, keepdims=True)

        @pl.when(pl.program_id(0) == 0)
        def _():
            ds_ref[...] = part

        @pl.when(pl.program_id(0) > 0)
        def _():
            ds_ref[...] += part

    row = _spec((tm, D), lambda i: (i, 0))
    vec = _spec((1, D), lambda i: (0, 0))
    return pl.pallas_call(
        body, name="pool_scale_bwd", grid=(T // tm,), in_specs=[row, row, vec], out_specs=[row, vec],
        out_shape=[jax.ShapeDtypeStruct((T, D), BF16), jax.ShapeDtypeStruct((1, D), F32)],
        compiler_params=_params("arbitrary"),
    )(dx, ypre, scale)


def _conv_specs(T, D, cw, cs):
    part = lambda p: _spec((None, T, cw), lambda j, p=p: (p, 0, j))
    taps = _spec((None, 8, cw), lambda j: (j // (cs // cw), 0, j % (cs // cw)))
    return part, taps


def _conv_fwd(bcx, taps4):
    _, T, D = bcx.shape
    cs = taps4.shape[2]
    cw = min(128, cs)
    part, taps = _conv_specs(T, D, cw, cs)

    def body(b_ref, c_ref, u_ref, w_ref, q_ref):
        g = c_ref[...].astype(F32) * u_ref[...].astype(F32)
        w = w_ref[...]
        y = w[2:3] * g + w[1:2] * _shift_down(g, 1) + w[0:1] * _shift_down(g, 2)
        q_ref[...] = (b_ref[...].astype(F32) * y).astype(q_ref.dtype)

    return pl.pallas_call(
        body, name="conv_fwd", grid=(D // cw,), in_specs=[part(0), part(1), part(2), taps],
        out_specs=_spec((T, cw), lambda j: (0, j)),
        out_shape=jax.ShapeDtypeStruct((T, D), BF16), compiler_params=_params("parallel"),
    )(bcx, bcx, bcx, taps4)


def _conv_bwd(dq, bcx, taps4):
    _, T, D = bcx.shape
    cs = taps4.shape[2]
    cw = min(128, cs)
    part, taps = _conv_specs(T, D, cw, cs)

    def body(dq_ref, b_ref, c_ref, u_ref, w_ref, d_ref, dw_ref):
        b = b_ref[...].astype(F32)
        c = c_ref[...].astype(F32)
        uu = u_ref[...].astype(F32)
        dqv = dq_ref[...].astype(F32)
        w = w_ref[...]
        g = c * uu
        g1 = _shift_down(g, 1)
        g2 = _shift_down(g, 2)
        d_ref[0] = (dqv * (w[2:3] * g + w[1:2] * g1 + w[0:1] * g2)).astype(BF16)
        dy = dqv * b
        dg = w[2:3] * dy + w[1:2] * _shift_up(dy, 1) + w[0:1] * _shift_up(dy, 2)
        d_ref[1] = (dg * uu).astype(BF16)
        d_ref[2] = (dg * c).astype(BF16)
        dw_ref[0:1, :] = jnp.sum(dy * g2, axis=0, keepdims=True)
        dw_ref[1:2, :] = jnp.sum(dy * g1, axis=0, keepdims=True)
        dw_ref[2:3, :] = jnp.sum(dy * g, axis=0, keepdims=True)
        dw_ref[3:8, :] = jnp.zeros((5, cw), F32)

    return pl.pallas_call(
        body, name="conv_bwd", grid=(D // cw,),
        in_specs=[_spec((T, cw), lambda j: (0, j)), part(0), part(1), part(2), taps],
        out_specs=[_spec((3, T, cw), lambda j: (0, 0, j)), taps],
        out_shape=[jax.ShapeDtypeStruct((3, T, D), BF16), jax.ShapeDtypeStruct((N_SHARD, 8, cs), F32)],
        compiler_params=_params("parallel"),
    )(dq, bcx, bcx, bcx, taps4)


def _quarter():
    return 2 * lax.axis_index("x") + lax.axis_index("y")


def _cast_into_slot(w, dtype, row0=0, rows=None):
    R, C = (w.shape[0] if rows is None else rows), w.shape[1]
    tr = _pick(math.gcd(R, row0), 512, 8)
    first = row0 // tr

    def body(w_ref, o_ref):
        o_ref[...] = w_ref[...].astype(o_ref.dtype)

    return pl.pallas_call(
        body, name="cast_into_slot", grid=(R // tr,),
        in_specs=[_spec((tr, C), lambda i: (first + i, 0))],
        out_specs=_spec((None, tr, C), lambda i: (_quarter(), i, 0)),
        out_shape=jax.ShapeDtypeStruct((N_SHARD, R, C), dtype), compiler_params=_params("parallel"),
    )(w)


def _sum_into(recv, own, l, L, prev):
    n, R2, C = recv.shape
    tr = _pick(R2, 256, 8)
    nb = R2 // tr

    def body(r_ref, o_ref, *rest):
        s = o_ref[...].astype(F32)
        for k in range(n):
            s = s + r_ref[k].astype(F32)
        rest[-1][...] = s

    in_specs = [_spec((n, tr, C), lambda i: (0, i, 0)),
                _spec((None, tr, C), lambda i: (_quarter(), lax.axis_index("c") * nb + i, 0))]
    args = [recv, own]
    if prev is not None:
        in_specs.append(ANY)
        args.append(prev)
    return pl.pallas_call(
        body, name="sum_into", grid=(nb,), in_specs=in_specs,
        out_specs=_spec((None, tr, C), lambda i: (l, lax.axis_index("c") * nb + i, 0)),
        out_shape=jax.ShapeDtypeStruct((L, 2 * R2, C), F32),
        input_output_aliases={} if prev is None else {2: 0},
        compiler_params=_params("parallel"),
    )(*args)


def _sum_devices(parts):
    n, R, C = parts.shape
    tr = _pick(R, 256, 8)

    def body(p_ref, o_ref):
        s = p_ref[0].astype(F32)
        for d in range(1, n):
            s = s + p_ref[d].astype(F32)
        o_ref[...] = s

    return pl.pallas_call(
        body, name="sum_devices", grid=(R // tr,),
        in_specs=[_spec((n, tr, C), lambda i: (0, i, 0))], out_specs=_spec((tr, C), lambda i: (i, 0)),
        out_shape=jax.ShapeDtypeStruct((R, C), F32), compiler_params=_params("parallel"),
    )(parts)


def _adamw(w, g, m, v):
    R, C = w.shape
    tr = _pick(R, 256, 8)

    def body(w_ref, g_ref, m_ref, v_ref, d_ref, nm_ref, nv_ref):
        gv = g_ref[...]
        m2 = ADAM_B1 * m_ref[...] + (1.0 - ADAM_B1) * gv
        v2 = ADAM_B2 * v_ref[...] + (1.0 - ADAM_B2) * (gv * gv)
        m_hat = m2 / (1.0 - ADAM_B1 ** ADAM_STEP)
        v_hat = v2 / (1.0 - ADAM_B2 ** ADAM_STEP)
        d_ref[...] = -ADAM_LR * (m_hat / (jnp.sqrt(v_hat) + ADAM_EPS) + ADAM_WD * w_ref[...])
        nm_ref[...] = m2
        nv_ref[...] = v2

    row = _spec((tr, C), lambda i: (i, 0))
    return pl.pallas_call(
        body, name="adamw", grid=(R // tr,), in_specs=[row] * 4, out_specs=[row] * 3,
        out_shape=[jax.ShapeDtypeStruct((R, C), F32)] * 3, compiler_params=_params("parallel"),
    )(w, g, m, v)


def _place():
    return lax.axis_index("x"), lax.axis_index("y"), lax.axis_index("c")


def _half(ref_rows, c):
    return pl.ds(c * (ref_rows // 2), ref_rows // 2)


def _allgather_shards(bufs):
    n = len(bufs)

    def body(*refs):
        outs = refs[n:2 * n]
        send, recv = refs[2 * n:]
        x, y, c = _place()
        chips = [(1 - x, y), (x, 1 - y), (1 - x, 1 - y)]

        def copy(a, k, quarter, core, to):
            part = outs[a].at[quarter, _half(outs[a].shape[1], core)]
            return pltpu.make_async_remote_copy(
                src_ref=part, dst_ref=part, send_sem=send.at[a, k], recv_sem=recv.at[a, k],
                device_id=to, device_id_type=MESH)

        first, passed = [], []
        for a in range(n):
            for k, (cx, cy) in enumerate(chips):
                cp = copy(a, k, 2 * x + y, c, (cx, cy, c))
                cp.start()
                first.append(cp)
        for a in range(n):
            for k, (cx, cy) in enumerate(chips):
                copy(a, k, 2 * cx + cy, c, (x, y, c)).wait_recv()
                cp = copy(a, 3 + k, 2 * cx + cy, c, (x, y, 1 - c))
                cp.start()
                passed.append(cp)
        for a in range(n):
            for k, (cx, cy) in enumerate(chips):
                copy(a, 3 + k, 2 * cx + cy, 1 - c, (x, y, 1 - c)).wait_recv()
        for cp in first + passed:
            cp.wait_send()

    return pl.pallas_call(
        body, name="allgather_shards", in_specs=[ANY] * n, out_specs=[ANY] * n,
        out_shape=[jax.ShapeDtypeStruct(b.shape, b.dtype) for b in bufs],
        input_output_aliases={a: a for a in range(n)},
        scratch_shapes=[pltpu.SemaphoreType.DMA((n, 6)), pltpu.SemaphoreType.DMA((n, 6))],
        compiler_params=pltpu.CompilerParams(has_side_effects=True),
    )(*bufs)


def _flips():
    return [(fx, fy, fc) for fx in (0, 1) for fy in (0, 1) for fc in (0, 1) if (fx, fy, fc) != (0, 0, 0)]


HBM = pl.BlockSpec(memory_space=pltpu.HBM)
SEM = pl.BlockSpec(memory_space=pltpu.SEMAPHORE)
DATAFLOW = pltpu.SideEffectType.DATAFLOW_SIDE_EFFECTING


def _gather_copies(bufs, send, recv):
    x, y, c = _place()
    pairs = []
    for a in range(len(bufs)):
        for k, (cx, cy) in enumerate([(1 - x, y), (x, 1 - y), (1 - x, 1 - y)]):
            def copy(quarter, a=a, k=k, cx=cx, cy=cy):
                part = bufs[a].at[quarter, _half(bufs[a].shape[1], c)]
                return pltpu.make_async_remote_copy(
                    src_ref=part, dst_ref=part, send_sem=send.at[3 * a + k], recv_sem=recv.at[3 * a + k],
                    device_id=(cx, cy, c), device_id_type=MESH)
            pairs.append((copy(2 * x + y), copy(2 * cx + cy)))
    return pairs


def _gather_start(name, bufs, after):
    n = len(bufs)

    def body(*refs):
        for out, _ in _gather_copies(refs[:n], refs[n + 1], refs[n + 2]):
            out.start()
        refs[-1][...] = jnp.zeros_like(refs[-1])

    outs = pl.pallas_call(
        body, name=name, in_specs=[HBM] * n + [ANY],
        out_specs=(SEM, SEM) + (HBM,) * n + (pl.BlockSpec(memory_space=pltpu.VMEM),),
        out_shape=(pltpu.SemaphoreType.DMA((3 * n,)), pltpu.SemaphoreType.DMA((3 * n,)))
        + tuple(pltpu.HBM(b.shape, b.dtype) for b in bufs) + (jax.ShapeDtypeStruct((8, 128), F32),),
        input_output_aliases={a: 2 + a for a in range(n)},
        compiler_params=pltpu.CompilerParams(has_side_effects=DATAFLOW),
    )(*[pltpu.with_memory_space_constraint(b, pltpu.HBM) for b in bufs], after)
    return outs[:-1], outs[-1]


def _gather_wait(name, started, after):
    n = len(started) - 2

    def body(*refs):
        for out, arriving in _gather_copies(refs[:n], refs[n], refs[n + 1]):
            out.wait_send()
            arriving.wait_recv()

    return pl.pallas_call(
        body, name=name, in_specs=[HBM] * n + [SEM, SEM, ANY], out_specs=(HBM,) * n,
        out_shape=tuple(pltpu.HBM(b.shape, b.dtype) for b in started[2:]),
        input_output_aliases={a: a for a in range(n)},
        compiler_params=pltpu.CompilerParams(has_side_effects=DATAFLOW),
    )(*started[2:], started[0], started[1], after)


def _forward_halves(bufs):
    n = len(bufs)

    def body(*refs):
        outs = refs[n:2 * n]
        send, recv = refs[2 * n:]
        x, y, c = _place()

        def copy(a, k, quarter, core):
            part = outs[a].at[quarter, _half(outs[a].shape[1], core)]
            return pltpu.make_async_remote_copy(
                src_ref=part, dst_ref=part, send_sem=send.at[a, k], recv_sem=recv.at[a, k],
                device_id=(x, y, 1 - c), device_id_type=MESH)

        others = [2 * (1 - x) + y, 2 * x + (1 - y), 2 * (1 - x) + (1 - y)]
        for a in range(n):
            for k, quarter in enumerate(others):
                copy(a, k, quarter, c).start()
        for a in range(n):
            for k, quarter in enumerate(others):
                copy(a, k, quarter, 1 - c).wait_recv()
        for a in range(n):
            for k, quarter in enumerate(others):
                copy(a, k, quarter, c).wait_send()

    return pl.pallas_call(
        body, name="forward_halves", in_specs=[ANY] * n, out_specs=[ANY] * n,
        out_shape=[jax.ShapeDtypeStruct(b.shape, b.dtype) for b in bufs],
        input_output_aliases={a: a for a in range(n)},
        scratch_shapes=[pltpu.SemaphoreType.DMA((n, 3)), pltpu.SemaphoreType.DMA((n, 3))],
        compiler_params=pltpu.CompilerParams(has_side_effects=True),
    )(*bufs)


def _exchange_copies(grads, zones, send, recv):
    x, y, c = _place()
    copies = []
    for a in range(len(grads)):
        rows = grads[a].shape[1]
        for k, (fx, fy, fc) in enumerate(_flips()):
            px, py, pc = x ^ fx, y ^ fy, c ^ fc
            j = a * (N_DEV - 1) + k
            copies.append(pltpu.make_async_remote_copy(
                src_ref=grads[a].at[2 * px + py, _half(rows, pc)], dst_ref=zones[a].at[k],
                send_sem=send.at[j], recv_sem=recv.at[j], device_id=(px, py, pc), device_id_type=MESH))
    return copies


def _exchange_start(name, grads):
    n = len(grads)
    zones = [lax.empty((N_DEV - 1, g.shape[1] // 2, g.shape[2]), g.dtype) for g in grads]

    def body(*refs):
        for cp in _exchange_copies(refs[:n], refs[n:2 * n], refs[2 * n], refs[2 * n + 1]):
            cp.start()
        refs[-1][...] = jnp.zeros_like(refs[-1])

    outs = pl.pallas_call(
        body, name=name, in_specs=[HBM] * (2 * n),
        out_specs=(SEM, SEM) + (HBM,) * (2 * n) + (pl.BlockSpec(memory_space=pltpu.VMEM),),
        out_shape=(pltpu.SemaphoreType.DMA((n * (N_DEV - 1),)), pltpu.SemaphoreType.DMA((n * (N_DEV - 1),)))
        + tuple(pltpu.HBM(v.shape, v.dtype) for v in list(grads) + zones) + (jax.ShapeDtypeStruct((8, 128), F32),),
        input_output_aliases={a: 2 + a for a in range(2 * n)},
        compiler_params=pltpu.CompilerParams(has_side_effects=DATAFLOW),
    )(*[pltpu.with_memory_space_constraint(v, pltpu.HBM) for v in list(grads) + zones])
    return outs[:-1], outs[-1]


def _exchange_wait(name, started, after):
    send, recv = started[0], started[1]
    n = (len(started) - 2) // 2

    def body(*refs):
        for cp in _exchange_copies(refs[:n], refs[n:2 * n], refs[2 * n], refs[2 * n + 1]):
            cp.wait_send()
            cp.wait_recv()

    outs = pl.pallas_call(
        body, name=name, in_specs=[HBM] * (2 * n) + [SEM, SEM, ANY], out_specs=(HBM,) * (2 * n),
        out_shape=tuple(pltpu.HBM(v.shape, v.dtype) for v in started[2:]),
        input_output_aliases={a: a for a in range(2 * n)},
        compiler_params=pltpu.CompilerParams(has_side_effects=DATAFLOW),
    )(*started[2:], send, recv, after)
    return outs[:n], outs[n:]


def _share_halves(fulls):
    n = len(fulls)
    index = [(a, l) for a in range(n) for l in range(fulls[a].shape[0])]

    def body(*refs):
        outs = refs[n:2 * n]
        send, recv = refs[2 * n:]
        x, y, c = _place()

        def copy(j, core):
            a, l = index[j]
            part = outs[a].at[l, _half(outs[a].shape[1], core)]
            return pltpu.make_async_remote_copy(
                src_ref=part, dst_ref=part, send_sem=send.at[j], recv_sem=recv.at[j],
                device_id=(x, y, 1 - c), device_id_type=MESH)

        for j in range(len(index)):
            copy(j, c).start()
        for j in range(len(index)):
            copy(j, 1 - c).wait_recv()
        for j in range(len(index)):
            copy(j, c).wait_send()

    return pl.pallas_call(
        body, name="share_halves", in_specs=[ANY] * n, out_specs=[ANY] * n,
        out_shape=[jax.ShapeDtypeStruct(f.shape, f.dtype) for f in fulls],
        input_output_aliases={a: a for a in range(n)},
        scratch_shapes=[pltpu.SemaphoreType.DMA((len(index),)), pltpu.SemaphoreType.DMA((len(index),))],
        compiler_params=pltpu.CompilerParams(has_side_effects=True),
    )(*fulls)


def _allgather_small(v):
    def body(v_ref, o_ref, send, recv, lsem):
        x, y, c = _place()
        me = 4 * x + 2 * y + c
        own = pltpu.make_async_copy(v_ref, o_ref.at[me], lsem)
        own.start()
        sends = []
        for k, (fx, fy, fc) in enumerate(_flips()):
            cp = pltpu.make_async_remote_copy(
                src_ref=v_ref, dst_ref=o_ref.at[me], send_sem=send.at[k], recv_sem=recv.at[k],
                device_id=(x ^ fx, y ^ fy, c ^ fc), device_id_type=MESH)
            cp.start()
            sends.append(cp)
        for k, (fx, fy, fc) in enumerate(_flips()):
            px, py, pc = x ^ fx, y ^ fy, c ^ fc
            pltpu.make_async_remote_copy(
                src_ref=v_ref, dst_ref=o_ref.at[4 * px + 2 * py + pc], send_sem=send.at[k], recv_sem=recv.at[k],
                device_id=(px, py, pc), device_id_type=MESH).wait_recv()
        for cp in sends:
            cp.wait_send()
        own.wait()

    return pl.pallas_call(
        body, name="allgather_small", in_specs=[ANY], out_specs=ANY,
        out_shape=jax.ShapeDtypeStruct((N_DEV,) + v.shape, v.dtype),
        scratch_shapes=[pltpu.SemaphoreType.DMA((7,)), pltpu.SemaphoreType.DMA((7,)), pltpu.SemaphoreType.DMA(())],
        compiler_params=pltpu.CompilerParams(has_side_effects=True),
    )(v)


def _mm_col_fwd(name, a, w4, l, cb, out_dtype, parts, epi_act=None, w4b=None, tm_pref=512, after=()):
    T, K = a.shape
    cs = w4.shape[3]
    N = N_SHARD * cs
    tm = _pick(T, tm_pref, 8)
    nps = cs // cb
    npp = (N // parts) // cb
    a_spec = _spec((tm, K), lambda j, i, k: (i, 0))
    b_spec = _spec((None, None, K, cb), lambda j, i, k: (j // nps, l, 0, j % nps))
    if parts == 1:
        o_spec = _spec((tm, cb), lambda j, i, k: (i, j))
        o_shape = (T, N)
    else:
        o_spec = _spec((None, tm, cb), lambda j, i, k: (j // npp, i, j % npp))
        o_shape = (parts, T, N // parts)
    b_ops = [(w4, b_spec)] if w4b is None else [(w4, b_spec), (w4b, b_spec)]
    dots = [(0, 0, 0)] if w4b is None else [(0, 0, 0), (0, 1, 1)]
    out_dtypes = (out_dtype,) if w4b is None else (F32, F32, out_dtype)
    return _mm(name, (N // cb, T // tm, 1), [(a, a_spec)], b_ops, dots, NN, [(tm, cb)] * len(b_ops),
               [(o_shape, dt, o_spec) for dt in out_dtypes], epi=epi_act, after=after)


def _mm_row_fwd(name, a, w4, l, res):
    T, K = a.shape
    rs, N = w4.shape[2], w4.shape[3]
    tm = _pick(T, 256, 8)
    tn = _pick(N, 1024, 128)
    a_ops = [(a, _spec((tm, rs), lambda j, i, k, s=s: (i, s))) for s in range(N_SHARD)]
    b_ops = [(w4, _spec((None, None, rs, tn), lambda j, i, k, s=s: (s, l, 0, j))) for s in range(N_SHARD)]
    o_spec = _spec((tm, tn), lambda j, i, k: (i, j))
    return _mm(name, (N // tn, T // tm, 1), a_ops, b_ops, [(s, s, 0) for s in range(N_SHARD)], NN, [(tm, tn)],
               [((T, N), F32, o_spec)], epi=lambda accs, ex: [accs[0] + ex[0]], extras=[(res, o_spec)])


def _mm_row_bwd_data(name, dy, w4, l, out_dtype=BF16, epi=None, extras=(), n_out=1, tm_pref=512, after=()):
    T, N = dy.shape
    rs = w4.shape[2]
    K = N_SHARD * rs
    tm = _pick(T, tm_pref, 8)
    a_spec = _spec((tm, N), lambda j, i, k: (i, 0))
    b_spec = _spec((None, None, rs, N), lambda j, i, k: (j, l, 0, 0))
    o_spec = _spec((tm, rs), lambda j, i, k: (i, j))
    return _mm(name, (N_SHARD, T // tm, 1), [(dy, a_spec)], [(w4, b_spec)], [(0, 0, 0)], NT, [(tm, rs)],
               [((T, K), out_dtype, o_spec)] * n_out, epi=epi, extras=[(e, o_spec) for e in extras], after=after)


def _mm_row_bwd_weight(name, a, dy, rs):
    T, K = a.shape
    N = dy.shape[1]
    tk = _pick(T, 512, 8)
    tn = _pick(N, 1024, 128)
    a_spec = _spec((tk, rs), lambda i, j, k: (k, i))
    b_spec = _spec((tk, tn), lambda i, j, k: (k, j))
    o_spec = _spec((None, rs, tn), lambda i, j, k: (i, 0, j))
    return _mm(name, (N_SHARD, N // tn, T // tk), [(a, a_spec)], [(dy, b_spec)], [(0, 0, 0)], TN, [(rs, tn)],
               [((N_SHARD, rs, N), BF16, o_spec)])[0]


def _mm_col_bwd_data(name, dys, w4s, l, cb, parts, tm_pref=512):
    T = dys[0].shape[-2]
    K, cs = w4s[0].shape[2], w4s[0].shape[3]
    N = N_SHARD * cs
    tm = _pick(T, tm_pref, 8)
    nps = cs // cb
    npp = (N // parts) // cb
    if parts == 1:
        a_spec = _spec((tm, cb), lambda i, j, k: (i, k))
    else:
        a_spec = _spec((None, tm, cb), lambda i, j, k: (k // npp, i, k % npp))
    b_spec = _spec((None, None, K, cb), lambda i, j, k: (k // nps, l, 0, k % nps))
    o_spec = _spec((tm, K), lambda i, j, k: (i, 0))
    return _mm(name, (T // tm, 1, N // cb), [(d, a_spec) for d in dys], [(w, b_spec) for w in w4s],
               [(p, p, 0) for p in range(len(dys))], NT, [(tm, K)], [((T, K), F32, o_spec)])[0]


def _mm_ffn_bwd_data(name, dys, w4s, l, tm_pref=256, tn_pref=512):
    T = dys[0].shape[0]
    K, cs = w4s[0].shape[2], w4s[0].shape[3]
    tm = _pick(T, tm_pref, 8)
    tn = _pick(K, tn_pref, 128)
    a_ops = [(d, _spec((tm, cs), lambda j, i, k, s=s: (i, s))) for d in dys for s in range(N_SHARD)]
    b_ops = [(w, _spec((None, None, tn, cs), lambda j, i, k, s=s: (s, l, j, 0))) for w in w4s for s in range(N_SHARD)]
    n = len(a_ops)
    return _mm(name, (K // tn, T // tm, 1), a_ops, b_ops, [(p, p, 0) for p in range(n)], NT, [(tm, tn)],
               [((T, K), F32, _spec((tm, tn), lambda j, i, k: (i, j)))])[0]


def _mm_col_bwd_weight(name, a, dys, cs, cb, parts, tm_pref=512, chunks=1):
    T, K = a.shape
    N = N_SHARD * cs
    tk = _pick(T // chunks, 512, 8)
    tm = _pick(K, tm_pref, 128)
    nps = cs // cb
    npp = (N // parts) // cb
    a_ops = [(a, _spec((tk, tm), lambda i, j, k, c=c: (k * chunks + c, i))) for c in range(chunks)]
    if parts == 1:
        b_ops = [(d, _spec((tk, cb), lambda i, j, k, c=c: (k * chunks + c, j))) for d in dys for c in range(chunks)]
    else:
        b_ops = [(d, _spec((None, tk, cb), lambda i, j, k, c=c: (j // npp, k * chunks + c, j % npp)))
                 for d in dys for c in range(chunks)]
    o_spec = _spec((None, tm, cb), lambda i, j, k: (j // nps, i, j % nps))
    nd = len(dys)
    return _mm(name, (K // tm, N // cb, T // (tk * chunks)), a_ops, b_ops,
               [(c, p * chunks + c, p) for p in range(nd) for c in range(chunks)], TN, [(tm, cb)] * nd,
               [((N_SHARD, K, cs), BF16, o_spec)] * nd)


def _swiglu(accs, _):
    g, up = accs
    return [g, up, g * jax.nn.sigmoid(g) * up]


def _swiglu_bwd(accs, ex):
    da = accs[0]
    g = ex[0].astype(F32)
    up = ex[1].astype(F32)
    s = jax.nn.sigmoid(g)
    return [da * up * (s * (1.0 + g * (1.0 - s))), da * (g * s)]


def _ffn_fwd(x1, g_ffn, wg4, wu4, wd4, l):
    h2 = _rmsnorm_fwd(x1, g_ffn, BF16)
    fs = wg4.shape[3]
    gate, up, act = _mm_col_fwd("ffn_up", h2, wg4, l, fs, BF16, 1, epi_act=_swiglu, w4b=wu4, tm_pref=256)
    x2 = _mm_row_fwd("ffn_down", act, wd4, l, x1)[0]
    return x2, (x1, h2, gate, up, act)


def _ffn_bwd(dx2, dx2b, saved, g_ffn, wg4, wu4, wd4, l, after):
    x1, h2, gate, up, act = saved
    fs = wg4.shape[3]
    dgate, dup = _mm_row_bwd_data("ffn_down_bwd_data", dx2b, wd4, l, epi=_swiglu_bwd, extras=(gate, up), n_out=2,
                                  tm_pref=256, after=after)
    g_down = _mm_row_bwd_weight("ffn_down_bwd_weight", act, dx2b, fs)
    dh2 = _mm_ffn_bwd_data("ffn_up_bwd_data", [dgate, dup], [wg4, wu4], l)
    g_gate, g_up = _mm_col_bwd_weight("ffn_up_bwd_weight", h2, [dgate, dup], fs, fs, 1, chunks=4)
    dx1, dx1b, dg = _rmsnorm_bwd(x1, g_ffn, dh2, dx2)
    return dx1, dx1b, dg, g_gate, g_up, g_down


def _sb_fwd(x, g_mix, gq, gk, wqkv4, wo4, l, after=()):
    D = x.shape[1]
    h = _rmsnorm_fwd(x, g_mix, BF16)
    qkv = _mm_col_fwd("qkv_proj", h, wqkv4, l, D // 4, F32, 3, after=after)[0]
    qs, kn, vb = _headnorm_fwd(qkv, gq, gk)
    o, tot = _attn_fwd(qs, kn, vb)
    x1 = _mm_row_fwd("attn_out", o, wo4, l, x)[0]
    return x1, (x, h, qkv, qs, kn, vb, o, tot)


def _sb_bwd(dx1, dx1b, saved, g_mix, gq, gk, wqkv4, wo4, l, after=()):
    x, h, qkv, qs, kn, vb, o, tot = saved
    D = x.shape[1]
    do = _mm_row_bwd_data("attn_out_bwd_data", dx1b, wo4, l, after=after)[0]
    g_wo = _mm_row_bwd_weight("attn_out_bwd_weight", o, dx1b, wo4.shape[2])
    dqs, dkn, dv = _attn_bwd(qs, kn, vb, tot, do)
    dqkv, dgq, dgk = _headnorm_bwd(qkv, gq, gk, dqs, dkn, dv)
    dh = _mm_col_bwd_data("qkv_bwd_data", [dqkv], [wqkv4], l, D // 4, 3)
    g_wqkv = _mm_col_bwd_weight("qkv_bwd_weight", h, [dqkv], wqkv4.shape[3], D // 4, 3, tm_pref=2048)[0]
    dx, dxb, dg = _rmsnorm_bwd(x, g_mix, dh, dx1)
    return dx, dxb, dg, dgq, dgk, g_wqkv, g_wo


def _pool_mix_fwd(x, g_mix, wp4, scale):
    T, D = x.shape
    C = D // N_POOL_GROUPS
    rq = C // N_SHARD
    h = _rmsnorm_fwd(x, g_mix, F32)
    p = _pool_fwd(h)
    tm = _pick(T, 512, 8)
    a_spec = _spec((tm, rq), lambda g, i, k: (i, g * N_SHARD + k))
    b_spec = _spec((None, None, rq, C), lambda g, i, k: (k, 0, g, 0))
    o_spec = _spec((tm, C), lambda g, i, k: (i, g))
    s_spec = _spec((1, C), lambda g, i, k: (0, g))
    x1, ypre = _mm("pool_mix", (N_POOL_GROUPS, T // tm, N_SHARD), [(p, a_spec)], [(wp4, b_spec)], [(0, 0, 0)], NN,
                   [(tm, C)], [((T, D), F32, o_spec), ((T, D), BF16, o_spec)],
                   epi=lambda accs, ex: [ex[0] + accs[0] * ex[1], accs[0]], extras=[(x, o_spec), (scale, s_spec)])
    return x1, (x, h, p, ypre)


def _pool_mix_bwd(dx1, dx1b, saved, g_mix, wp4, scale, after=()):
    x, h, p, ypre = saved
    T, D = x.shape
    C = D // N_POOL_GROUPS
    rq = C // N_SHARD
    dys, dscale = _pool_scale_bwd(dx1, ypre, scale)
    tm = _pick(T, 512, 8)
    dp = _mm("pool_mix_bwd_data", (N_POOL_GROUPS * N_SHARD, T // tm, 1),
             [(dys, _spec((tm, C), lambda j, i, k: (i, j // N_SHARD)))],
             [(wp4, _spec((None, None, rq, C), lambda j, i, k: (j % N_SHARD, 0, j // N_SHARD, 0)))],
             [(0, 0, 0)], NT, [(tm, rq)], [((T, D), F32, _spec((tm, rq), lambda j, i, k: (i, j)))], after=after)[0]
    tk = _pick(T, 512, 8)
    g_wp = _mm("pool_mix_bwd_weight", (N_POOL_GROUPS * N_SHARD, 1, T // tk),
               [(p, _spec((tk, rq), lambda j, n, k: (k, j)))],
               [(dys, _spec((tk, C), lambda j, n, k: (k, j // N_SHARD)))],
               [(0, 0, 0)], TN, [(rq, C)],
               [((N_SHARD, N_POOL_GROUPS * rq, C), BF16,
                 _spec((None, rq, C), lambda j, n, k: (j % N_SHARD, j // N_SHARD, 0)))])[0]
    dh = _pool_bwd(dp)
    dx, dxb, dg = _rmsnorm_bwd(x, g_mix, dh, dx1)
    return dx, dxb, dg, dscale, g_wp


def _conv_mix_fwd(x, g_mix, win4, taps4, wout4):
    D = x.shape[1]
    h = _rmsnorm_fwd(x, g_mix, BF16)
    bcx = _mm_col_fwd("conv_in", h, win4, 0, D // 4, F32, 3)[0]
    q = _conv_fwd(bcx, taps4)
    x1 = _mm_row_fwd("conv_out", q, wout4, 0, x)[0]
    return x1, (x, h, bcx, q)


def _conv_mix_bwd(dx1, dx1b, saved, g_mix, win4, taps4, wout4, after=()):
    x, h, bcx, q = saved
    D = x.shape[1]
    dq = _mm_row_bwd_data("conv_out_bwd_data", dx1b, wout4, 0, out_dtype=F32, after=after)[0]
    g_wout = _mm_row_bwd_weight("conv_out_bwd_weight", q, dx1b, wout4.shape[2])
    dbcx, dtaps = _conv_bwd(dq, bcx, taps4)
    dh = _mm_col_bwd_data("conv_in_bwd_data", [dbcx], [win4], 0, D // 4, 3)
    g_win = _mm_col_bwd_weight("conv_in_bwd_weight", h, [dbcx], win4.shape[3], D // 4, 3, tm_pref=2048)[0]
    dx, dxb, dg = _rmsnorm_bwd(x, g_mix, dh, dx1)
    return dx, dxb, dg, dtaps, g_wout, g_win


def _rows2d(w):
    return w.reshape(-1, w.shape[-1])


def _pad_rows(v, rows):
    return jnp.pad(v, ((0, rows - v.shape[0]), (0, 0)))


def kernel(x, norm_mix_g, norm_ffn_g, sb_w_qkv, sb_g_q, sb_g_k, sb_w_o, pool_w, pool_scale, conv_w_in, conv_w, conv_w_out, ffn_w_gate, ffn_w_up, ffn_w_down, loss_target, m_norm_mix_g, m_norm_ffn_g, m_sb_w_qkv, m_sb_g_q, m_sb_g_k, m_sb_w_o, m_pool_w, m_pool_scale, m_conv_w_in, m_conv_w, m_conv_w_out, m_ffn_w_gate, m_ffn_w_up, m_ffn_w_down, v_norm_mix_g, v_norm_ffn_g, v_sb_w_qkv, v_sb_g_q, v_sb_g_k, v_sb_w_o, v_pool_w, v_pool_scale, v_conv_w_in, v_conv_w, v_conv_w_out, v_ffn_w_gate, v_ffn_w_up, v_ffn_w_down):
    T, D = x.shape[1], x.shape[2]
    depth = norm_mix_g.shape[0]
    big = dict(sb_w_qkv=(sb_w_qkv, m_sb_w_qkv, v_sb_w_qkv), sb_w_o=(sb_w_o, m_sb_w_o, v_sb_w_o),
               pool_w=(pool_w, m_pool_w, v_pool_w), conv_w_in=(conv_w_in, m_conv_w_in, v_conv_w_in),
               conv_w_out=(conv_w_out, m_conv_w_out, v_conv_w_out), ffn_w_gate=(ffn_w_gate, m_ffn_w_gate, v_ffn_w_gate),
               ffn_w_up=(ffn_w_up, m_ffn_w_up, v_ffn_w_up), ffn_w_down=(ffn_w_down, m_ffn_w_down, v_ffn_w_down))
    names = list(big)

    cs_conv = conv_w.shape[2]
    taps_local = _pad_rows(conv_w[0], 16)
    first_names = ["sb_w_qkv", "sb_w_o", "ffn_w_gate", "ffn_w_up", "ffn_w_down"]
    per_layer = {n: _rows2d(big[n][0]).shape[0] // big[n][0].shape[0] for n in names}
    first = _allgather_shards([_cast_into_slot(_rows2d(big[n][0]), BF16, 0, per_layer[n]) for n in first_names])
    rest_bufs = []
    for n in names:
        skip = per_layer[n] if n in first_names else 0
        rest_bufs.append(_cast_into_slot(_rows2d(big[n][0]), BF16, skip, _rows2d(big[n][0]).shape[0] - skip))
    rest_bufs.append(_cast_into_slot(taps_local, F32))
    gathering, token = _gather_start("gather_start", rest_bufs, first[0])

    def as_layers(g4, layers):
        return g4.reshape(N_SHARD, layers, g4.shape[1] // layers, g4.shape[2])

    w_first = {n: as_layers(g4, 1) for n, g4 in zip(first_names, first)}
    w_rest, taps4 = {}, None

    def weights(layer, *tensors):
        if layer == 0 and tensors[0] in first_names:
            return [w_first[t] for t in tensors] + [0]
        return [w_rest[t] for t in tensors] + [layer - 1 if tensors[0] in first_names else layer]

    xs = x.reshape(T, D)
    saved = []
    for i in range(depth):
        kind, j = i % 3, i // 3
        g_mix = norm_mix_g[i:i + 1]
        if i == 1:
            gathered = _forward_halves(_gather_wait("gather_wait", gathering, xs))
            for n, g4 in zip(names, gathered[:-1]):
                w_rest[n] = as_layers(g4, big[n][0].shape[0] - (1 if n in first_names else 0))
            taps4 = gathered[-1]
        if kind == 0:
            xs, sv = _sb_fwd(xs, g_mix, sb_g_q[j:j + 1], sb_g_k[j:j + 1], *weights(j, "sb_w_qkv", "sb_w_o"),
                             after=(token,) if i == 0 else ())
        elif kind == 1:
            xs, sv = _pool_mix_fwd(xs, g_mix, w_rest["pool_w"], pool_scale[j:j + 1])
        else:
            xs, sv = _conv_mix_fwd(xs, g_mix, w_rest["conv_w_in"], taps4, w_rest["conv_w_out"])
        xs, sf = _ffn_fwd(xs, norm_ffn_g[i:i + 1], *weights(i, "ffn_w_gate", "ffn_w_up", "ffn_w_down"))
        saved.append((sv, sf))

    dxs, dxb, err2 = _loss_head(xs, loss_target.reshape(T, D))
    loss = lax.psum(0.5 * jnp.sum(err2) / D, ("x", "y", "c"))
    started = []
    small = {}
    token = ()

    def exchange(name, grads, layout):
        handle, tok = _exchange_start("exchange_start_" + name, grads)
        started.append((name, handle, layout))
        return (tok,)

    for i in reversed(range(depth)):
        kind, j = i % 3, i // 3
        sv, sf = saved[i]
        dxs, dxb, dg, g_gate, g_up, g_down = _ffn_bwd(dxs, dxb, sf, norm_ffn_g[i:i + 1],
                                                      *weights(i, "ffn_w_gate", "ffn_w_up", "ffn_w_down"), token)
        small[("norm_ffn_g", i)] = dg
        token = exchange(f"ffn_{i}", [g_gate, g_up, g_down], [("ffn_w_gate", i), ("ffn_w_up", i), ("ffn_w_down", i)])
        g_mix = norm_mix_g[i:i + 1]
        if kind == 0:
            dxs, dxb, dg, dgq, dgk, g_wqkv, g_wo = _sb_bwd(dxs, dxb, sv, g_mix, sb_g_q[j:j + 1], sb_g_k[j:j + 1],
                                                          *weights(j, "sb_w_qkv", "sb_w_o"), after=token)
            small[("sb_g_q", j)], small[("sb_g_k", j)] = dgq, dgk
            token = exchange(f"mix_{i}", [g_wqkv, g_wo], [("sb_w_qkv", j), ("sb_w_o", j)])
        elif kind == 1:
            dxs, dxb, dg, dscale, g_wp = _pool_mix_bwd(dxs, dxb, sv, g_mix, w_rest["pool_w"], pool_scale[j:j + 1],
                                                       after=token)
            small[("pool_scale", j)] = dscale
            token = exchange(f"mix_{i}", [g_wp], [("pool_w", j)])
        else:
            dxs, dxb, dg, dtaps, g_wout, g_win = _conv_mix_bwd(dxs, dxb, sv, g_mix, w_rest["conv_w_in"], taps4,
                                                              w_rest["conv_w_out"], after=token)
            small[("conv_w", j)] = dtaps
            token = exchange(f"mix_{i}", [g_win, g_wout], [("conv_w_in", j), ("conv_w_out", j)])
        small[("norm_mix_g", i)] = dg
    grad_x = dxs.reshape(x.shape)

    full = {}
    done = dg
    for name, handle, layout in started:
        grads, zones = _exchange_wait("exchange_wait_" + name, handle, done)
        for g, z, (t, l) in zip(grads, zones, layout):
            full[t] = done = _sum_into(z, g, l, big[t][0].shape[0], full.get(t))
    tensors = sorted(full)
    g_big = dict(zip(tensors, _share_halves([full[t] for t in tensors])))

    n_sb = sb_g_q.shape[0]
    gqk = jnp.concatenate([small[(n, j)] for j in range(n_sb) for n in ("sb_g_q", "sb_g_k")], axis=1)
    dtaps = small[("conv_w", 0)]
    taps_full = jnp.concatenate([dtaps[s, :3] for s in range(N_SHARD)], axis=1)
    pack = jnp.concatenate(
        [small[("norm_mix_g", i)] for i in range(depth)] + [small[("norm_ffn_g", i)] for i in range(depth)]
        + [small[("pool_scale", 0)], jnp.pad(gqk, ((0, 0), (0, D - gqk.shape[1]))), taps_full], axis=0)
    pack = _pad_rows(pack, SMALL_ROWS)
    g_small = _sum_devices(_allgather_small(pack))
    mine = 2 * lax.axis_index("x") + lax.axis_index("y")
    g_taps = lax.dynamic_slice(g_small, (2 * depth + 2, mine * cs_conv), (3, cs_conv))

    def pack_small(norm_mix, norm_ffn, scale, gq, gk, taps):
        qk = jnp.concatenate([v[j:j + 1] for j in range(n_sb) for v in (gq, gk)], axis=1)
        rows = jnp.concatenate([norm_mix, norm_ffn, scale, jnp.pad(qk, ((0, 0), (0, D - qk.shape[1]))),
                                jnp.pad(taps[0], ((0, 0), (0, D - cs_conv)))], axis=0)
        return _pad_rows(rows, SMALL_ROWS)

    g_pack = jnp.concatenate([g_small[:2 * depth + 2], jnp.pad(g_taps, ((0, 0), (0, D - cs_conv))),
                              jnp.zeros((SMALL_ROWS - 2 * depth - 5, D), F32)], axis=0)
    w_pack = pack_small(norm_mix_g, norm_ffn_g, pool_scale, sb_g_q, sb_g_k, conv_w)
    m_pack = pack_small(m_norm_mix_g, m_norm_ffn_g, m_pool_scale, m_sb_g_q, m_sb_g_k, m_conv_w)
    v_pack = pack_small(v_norm_mix_g, v_norm_ffn_g, v_pool_scale, v_sb_g_q, v_sb_g_k, v_conv_w)
    small_out = (g_pack,) + tuple(_adamw(w_pack, g_pack, m_pack, v_pack))

    def unpack_small(p):
        qk = p[2 * depth + 1]
        gq = jnp.stack([qk[(2 * j) * HEAD_DIM:(2 * j + 1) * HEAD_DIM] for j in range(n_sb)])
        gk = jnp.stack([qk[(2 * j + 1) * HEAD_DIM:(2 * j + 2) * HEAD_DIM] for j in range(n_sb)])
        return dict(norm_mix_g=p[:depth], norm_ffn_g=p[depth:2 * depth], pool_scale=p[2 * depth:2 * depth + 1],
                    sb_g_q=gq, sb_g_k=gk, conv_w=p[2 * depth + 2:2 * depth + 5, :cs_conv][None])

    results = [unpack_small(p) for p in small_out]
    for n in names:
        w, m, v = big[n]
        g = _rows2d(g_big[n])
        outs = (g,) + tuple(_adamw(_rows2d(w), g, _rows2d(m), _rows2d(v)))
        for r, o in zip(results, outs):
            r[n] = o.reshape(w.shape)

    order = ["norm_mix_g", "norm_ffn_g", "sb_w_qkv", "sb_g_q", "sb_g_k", "sb_w_o", "pool_w", "pool_scale",
             "conv_w_in", "conv_w", "conv_w_out", "ffn_w_gate", "ffn_w_up", "ffn_w_down"]
    return (loss, grad_x) + tuple(r[n] for r in results for n in order)
```

```python
import math

import jax
import jax.numpy as jnp
from jax import lax
from jax.experimental import pallas as pl
from jax.experimental.pallas import tpu as pltpu

F32 = jnp.float32
BF16 = jnp.bfloat16

HEAD_DIM = 128
N_POOL_GROUPS = 4
EPS = 1e-6
N_SHARD = 4
N_DEV = 8
VMEM_LIMIT_BYTES = 56 * 2**20
KEY_BLOCK = 128
ATTN_ROWS = 512
ATTN_GROUP_FWD = 4
ATTN_GROUP_BWD = 4
SMALL_ROWS = 16

ADAM_LR = 0.001
ADAM_B1 = 0.9
ADAM_B2 = 0.999
ADAM_EPS = 1e-08
ADAM_WD = 0.01
ADAM_STEP = 10

MESH = pl.DeviceIdType.MESH
ANY = pl.BlockSpec(memory_space=pl.ANY)


def _params(*sem):
    return pltpu.CompilerParams(dimension_semantics=sem, vmem_limit_bytes=VMEM_LIMIT_BYTES)


def _pick(n, pref, unit):
    t = (min(pref, n) // unit) * unit
    while n % t:
        t -= unit
    return t


NN = ((1,), (0,))
NT = ((1,), (1,))
TN = ((0,), (0,))


def _mm(name, grid, a_ops, b_ops, dots, dims, acc_shapes, outs, epi=None, extras=(), after=()):
    na, nb, ne, no, nacc = len(a_ops), len(b_ops), len(extras), len(outs), len(acc_shapes)
    nk = grid[2]
    n_in = na + nb + ne + len(after)

    def body(*refs):
        a_refs, b_refs = refs[:na], refs[na:na + nb]
        e_refs = refs[na + nb:na + nb + ne]
        o_refs = refs[n_in:n_in + no]
        acc_refs = refs[n_in + no:]

        def partial_sums():
            sums = [None] * nacc
            for ai, bi, ci in dots:
                d = lax.dot_general(a_refs[ai][...].astype(BF16), b_refs[bi][...].astype(BF16),
                                    (dims, ((), ())), preferred_element_type=F32)
                sums[ci] = d if sums[ci] is None else sums[ci] + d
            return sums

        def finish(accs):
            res = epi(accs, [e[...] for e in e_refs]) if epi is not None else accs
            for o, r in zip(o_refs, res):
                o[...] = r.astype(o.dtype)

        if nk == 1:
            finish(partial_sums())
            return
        k = pl.program_id(2)

        @pl.when(k == 0)
        def _():
            for acc, s in zip(acc_refs, partial_sums()):
                acc[...] = s

        @pl.when(k > 0)
        def _():
            for acc, s in zip(acc_refs, partial_sums()):
                acc[...] += s

        @pl.when(k == nk - 1)
        def _():
            finish([acc[...] for acc in acc_refs])

    ops = list(a_ops) + list(b_ops) + list(extras) + [(t, ANY) for t in after]
    return pl.pallas_call(
        body, name=name, grid=grid,
        in_specs=[s for _, s in ops],
        out_specs=[s for _, _, s in outs],
        out_shape=[jax.ShapeDtypeStruct(sh, dt) for sh, dt, _ in outs],
        scratch_shapes=[pltpu.VMEM(s, F32) for s in acc_shapes] if nk > 1 else [],
        compiler_params=_params("parallel", "parallel", "arbitrary"),
    )(*[a for a, _ in ops])


def _spec(block, index):
    return pl.BlockSpec(block, index)


def _rmsnorm_fwd(x, g, out_dtype):
    T, D = x.shape
    tm = _pick(T, 256, 8)

    def body(x_ref, g_ref, o_ref):
        xv = x_ref[...]
        r = lax.rsqrt(jnp.mean(xv * xv, axis=-1, keepdims=True) + EPS)
        o_ref[...] = (xv * r * g_ref[...]).astype(o_ref.dtype)

    row = _spec((tm, D), lambda i: (i, 0))
    return pl.pallas_call(
        body, name="rmsnorm_fwd", grid=(T // tm,),
        in_specs=[row, _spec((1, D), lambda i: (0, 0))], out_specs=row,
        out_shape=jax.ShapeDtypeStruct((T, D), out_dtype), compiler_params=_params("parallel"),
    )(x, g)


def _rmsnorm_bwd(x, g, dh, dres):
    T, D = x.shape
    tm = _pick(T, 256, 8)

    def body(x_ref, g_ref, dh_ref, dres_ref, dx_ref, dxb_ref, dg_ref):
        xv = x_ref[...]
        dhv = dh_ref[...].astype(F32)
        r = lax.rsqrt(jnp.mean(xv * xv, axis=-1, keepdims=True) + EPS)
        xh = xv * r
        dxh = dhv * g_ref[...]
        dx = r * (dxh - xh * jnp.mean(dxh * xh, axis=-1, keepdims=True)) + dres_ref[...]
        dx_ref[...] = dx
        dxb_ref[...] = dx.astype(BF16)
        part = jnp.sum(dhv * xh, axis=0, keepdims=True)

        @pl.when(pl.program_id(0) == 0)
        def _():
            dg_ref[...] = part

        @pl.when(pl.program_id(0) > 0)
        def _():
            dg_ref[...] += part

    row = _spec((tm, D), lambda i: (i, 0))
    vec = _spec((1, D), lambda i: (0, 0))
    return pl.pallas_call(
        body, name="rmsnorm_bwd", grid=(T // tm,),
        in_specs=[row, vec, row, row], out_specs=[row, row, vec],
        out_shape=[jax.ShapeDtypeStruct((T, D), F32), jax.ShapeDtypeStruct((T, D), BF16),
                   jax.ShapeDtypeStruct((1, D), F32)],
        compiler_params=_params("arbitrary"),
    )(x, g, dh, dres)


def _loss_head(y, target):
    T, D = y.shape
    tm = _pick(T, 256, 8)

    def body(y_ref, t_ref, dy_ref, dyb_ref, l_ref):
        err = y_ref[...] - t_ref[...]
        dy = err * (1.0 / D)
        dy_ref[...] = dy
        dyb_ref[...] = dy.astype(BF16)
        part = jnp.sum(err * err, axis=0, keepdims=True)

        @pl.when(pl.program_id(0) == 0)
        def _():
            l_ref[...] = part

        @pl.when(pl.program_id(0) > 0)
        def _():
            l_ref[...] += part

    row = _spec((tm, D), lambda i: (i, 0))
    vec = _spec((1, D), lambda i: (0, 0))
    return pl.pallas_call(
        body, name="loss_head", grid=(T // tm,),
        in_specs=[row, row], out_specs=[row, row, vec],
        out_shape=[jax.ShapeDtypeStruct((T, D), F32), jax.ShapeDtypeStruct((T, D), BF16),
                   jax.ShapeDtypeStruct((1, D), F32)],
        compiler_params=_params("arbitrary"),
    )(y, target)


def _headnorm_fwd(qkv, gq, gk):
    _, T, D = qkv.shape
    H = D // HEAD_DIM
    tm = _pick(T, 256, 8)
    scale = HEAD_DIM ** -0.5

    def body(q_ref, k_ref, v_ref, gq_ref, gk_ref, qs_ref, kn_ref, vb_ref):
        for h in range(H):
            sl = slice(h * HEAD_DIM, (h + 1) * HEAD_DIM)
            q = q_ref[:, sl]
            qs_ref[:, sl] = (q * lax.rsqrt(jnp.mean(q * q, axis=-1, keepdims=True) + EPS)
                             * (gq_ref[...] * scale)).astype(BF16)
            k = k_ref[:, sl]
            kn_ref[:, sl] = (k * lax.rsqrt(jnp.mean(k * k, axis=-1, keepdims=True) + EPS)
                             * gk_ref[...]).astype(BF16)
        vb_ref[...] = v_ref[...].astype(BF16)

    part = lambda p: _spec((None, tm, D), lambda i, p=p: (p, i, 0))
    row = _spec((tm, D), lambda i: (i, 0))
    vec = _spec((1, HEAD_DIM), lambda i: (0, 0))
    return pl.pallas_call(
        body, name="headnorm_fwd", grid=(T // tm,),
        in_specs=[part(0), part(1), part(2), vec, vec], out_specs=[row, row, row],
        out_shape=[jax.ShapeDtypeStruct((T, D), BF16)] * 3, compiler_params=_params("parallel"),
    )(qkv, qkv, qkv, gq, gk)


def _headnorm_bwd(qkv, gq, gk, dqs, dkn, dv):
    _, T, D = qkv.shape
    H = D // HEAD_DIM
    tm = _pick(T, 256, 8)
    scale = HEAD_DIM ** -0.5

    def body(q_ref, k_ref, gq_ref, gk_ref, dqs_ref, dkn_ref, dv_ref, dqkv_ref, dgq_ref, dgk_ref):
        dgq = jnp.zeros((1, HEAD_DIM), F32)
        dgk = jnp.zeros((1, HEAD_DIM), F32)
        for h in range(H):
            sl = slice(h * HEAD_DIM, (h + 1) * HEAD_DIM)
            for src, dsrc, g_ref, sc, p in ((q_ref, dqs_ref, gq_ref, scale, 0), (k_ref, dkn_ref, gk_ref, 1.0, 1)):
                v = src[:, sl]
                r = lax.rsqrt(jnp.mean(v * v, axis=-1, keepdims=True) + EPS)
                vh = v * r
                dn = dsrc[:, sl] * sc
                dvh = dn * g_ref[...]
                dqkv_ref[p, :, sl] = (r * (dvh - vh * jnp.mean(dvh * vh, axis=-1, keepdims=True))).astype(BF16)
                dg = jnp.sum(dn * vh, axis=0, keepdims=True)
                if p == 0:
                    dgq = dgq + dg
                else:
                    dgk = dgk + dg
        dqkv_ref[2] = dv_ref[...].astype(BF16)

        @pl.when(pl.program_id(0) == 0)
        def _():
            dgq_ref[...] = dgq
            dgk_ref[...] = dgk

        @pl.when(pl.program_id(0) > 0)
        def _():
            dgq_ref[...] += dgq
            dgk_ref[...] += dgk

    part = lambda p: _spec((None, tm, D), lambda i, p=p: (p, i, 0))
    row = _spec((tm, D), lambda i: (i, 0))
    vec = _spec((1, HEAD_DIM), lambda i: (0, 0))
    return pl.pallas_call(
        body, name="headnorm_bwd", grid=(T // tm,),
        in_specs=[part(0), part(1), vec, vec, row, row, row],
        out_specs=[_spec((3, tm, D), lambda i: (0, i, 0)), vec, vec],
        out_shape=[jax.ShapeDtypeStruct((3, T, D), BF16), jax.ShapeDtypeStruct((1, HEAD_DIM), F32),
                   jax.ShapeDtypeStruct((1, HEAD_DIM), F32)],
        compiler_params=_params("arbitrary"),
    )(qkv, qkv, gq, gk, dqs, dkn, dv)


def _sum_matrix(prefix):
    r = lax.broadcasted_iota(jnp.int32, (2 * KEY_BLOCK, 2 * KEY_BLOCK), 0) & (KEY_BLOCK - 1)
    c = lax.broadcasted_iota(jnp.int32, (2 * KEY_BLOCK, 2 * KEY_BLOCK), 1)
    tri = (r <= c) if prefix else (r > c)
    return jnp.where(tri | (c >= KEY_BLOCK), 1.0, 0.0).astype(BF16)


def _block_sums(v, u):
    hi = v.astype(BF16)
    lo = (v - hi.astype(F32)).astype(BF16)
    s = jnp.dot(jnp.concatenate([hi, lo], axis=1), u, preferred_element_type=F32)
    return s[:, :KEY_BLOCK], s[:, KEY_BLOCK:]


def _log_terms(z, mask):
    ls = jnp.minimum(z, 0.0) - jnp.log(1.0 + jnp.exp(-jnp.abs(z)))
    lk = ls - z
    if mask is not None:
        lk = jnp.where(mask, lk, 0.0)
    return ls, lk


def _causal_mask(row0, key0, tq):
    t = row0 + lax.broadcasted_iota(jnp.int32, (tq, KEY_BLOCK), 0)
    s = key0 + lax.broadcasted_iota(jnp.int32, (tq, KEY_BLOCK), 1)
    return s < t


def _attn_fwd(qs, kn, vb):
    T, D = qs.shape
    H = D // HEAD_DIM
    ATTN_GROUP = ATTN_GROUP_FWD
    tq = _pick(T, ATTN_ROWS, ATTN_GROUP * KEY_BLOCK)
    nd = tq // KEY_BLOCK

    def body(q_ref, k_ref, v_ref, u_ref, o_ref, tot_ref, acc_ref, run_ref):
        i = pl.program_id(1)
        q = q_ref[...]
        uv = u_ref[...]
        acc_ref[...] = jnp.zeros_like(acc_ref)
        run_ref[...] = jnp.zeros_like(run_ref)

        def group(first, masked):
            k_lo = pl.multiple_of((first - ATTN_GROUP + 1) * KEY_BLOCK, KEY_BLOCK)
            slab = pl.ds(k_lo, ATTN_GROUP * KEY_BLOCK)
            z = lax.dot_general(q, k_ref[slab, :], (NT, ((), ())), preferred_element_type=F32)
            parts = []
            for b in reversed(range(ATTN_GROUP)):
                mask = _causal_mask(i * tq, k_lo + b * KEY_BLOCK, tq) if masked else None
                ls, lk = _log_terms(z[:, b * KEY_BLOCK:(b + 1) * KEY_BLOCK], mask)
                after, rows = _block_sums(lk, uv)
                parts.append((b, mask, ls + after, rows))
            run = run_ref[...]
            a_parts = [None] * ATTN_GROUP
            for b, mask, base, rows in parts:
                a = jnp.exp(base + run)
                if masked:
                    a = jnp.where(mask, a, 0.0)
                a_parts[b] = a.astype(BF16)
                run = run + rows
            acc_ref[...] += jnp.dot(jnp.concatenate(a_parts, axis=1), v_ref[slab, :], preferred_element_type=F32)
            run_ref[...] = run

        for d in range(nd // ATTN_GROUP):
            group(i * nd + nd - 1 - d * ATTN_GROUP, True)

        def below(n, c):
            group(i * nd - 1 - n * ATTN_GROUP, False)
            return c

        lax.fori_loop(0, i * (nd // ATTN_GROUP), below, 0)
        o_ref[...] = acc_ref[...].astype(o_ref.dtype)
        tot_ref[...] = run_ref[...]

    blk = _spec((tq, HEAD_DIM), lambda h, i: (i, h))
    col = _spec((T, HEAD_DIM), lambda h, i: (0, h))
    return pl.pallas_call(
        body, name="attn_fwd", grid=(H, T // tq),
        in_specs=[blk, col, col, _spec((2 * KEY_BLOCK, 2 * KEY_BLOCK), lambda h, i: (0, 0))], out_specs=[blk, blk],
        out_shape=[jax.ShapeDtypeStruct((T, D), BF16), jax.ShapeDtypeStruct((T, D), F32)],
        scratch_shapes=[pltpu.VMEM((tq, HEAD_DIM), F32), pltpu.VMEM((tq, KEY_BLOCK), F32)],
        compiler_params=_params("parallel", "arbitrary"),
    )(qs, kn, vb, _sum_matrix(False))


def _attn_bwd(qs, kn, vb, tot, do):
    T, D = qs.shape
    H = D // HEAD_DIM
    ATTN_GROUP = ATTN_GROUP_BWD
    tq = _pick(T, ATTN_ROWS, ATTN_GROUP * KEY_BLOCK)
    nd = tq // KEY_BLOCK

    def body(q_ref, k_ref, v_ref, tot_ref, do_ref, u_ref, dq_ref, dk_ref, dv_ref, run_ref, grun_ref):
        i = pl.program_id(1)

        @pl.when(i == 0)
        def _():
            dk_ref[...] = jnp.zeros_like(dk_ref)
            dv_ref[...] = jnp.zeros_like(dv_ref)

        q = q_ref[...]
        dov = do_ref[...]
        uv = u_ref[...]
        dq_ref[...] = jnp.zeros_like(dq_ref)
        run_ref[...] = jnp.zeros_like(run_ref)
        grun_ref[...] = jnp.zeros_like(grun_ref)

        def group(first, masked):
            k_lo = pl.multiple_of(first * KEY_BLOCK, KEY_BLOCK)
            slab = pl.ds(k_lo, ATTN_GROUP * KEY_BLOCK)
            ks = k_ref[slab, :]
            z = lax.dot_general(q, ks, (NT, ((), ())), preferred_element_type=F32)
            da = lax.dot_general(dov, v_ref[slab, :], (NT, ((), ())), preferred_element_type=F32)
            parts = []
            for b in range(ATTN_GROUP):
                cols = slice(b * KEY_BLOCK, (b + 1) * KEY_BLOCK)
                mask = _causal_mask(i * tq, k_lo + b * KEY_BLOCK, tq) if masked else None
                ls, lk = _log_terms(z[:, cols], mask)
                upto, rows = _block_sums(lk, uv)
                parts.append((mask, ls, tot_ref[...] - upto, rows, da[:, cols]))
            run = run_ref[...]
            grun = grun_ref[...]
            a_parts, dz_parts = [], []
            for mask, ls, right, rows, dab in parts:
                a = jnp.exp(ls + (right - run))
                if masked:
                    a = jnp.where(mask, a, 0.0)
                g = a * dab
                gupto, grows = _block_sums(g, uv)
                dz = g - jnp.exp(ls) * (grun + gupto)
                if masked:
                    dz = jnp.where(mask, dz, 0.0)
                a_parts.append(a.astype(BF16))
                dz_parts.append(dz.astype(BF16))
                run = run + rows
                grun = grun + grows
            dzs = jnp.concatenate(dz_parts, axis=1)
            dq_ref[...] += jnp.dot(dzs, ks, preferred_element_type=F32)
            dk_ref[slab, :] += lax.dot_general(dzs, q, (TN, ((), ())), preferred_element_type=F32)
            dv_ref[slab, :] += lax.dot_general(jnp.concatenate(a_parts, axis=1), dov, (TN, ((), ())),
                                               preferred_element_type=F32)
            run_ref[...] = run
            grun_ref[...] = grun

        def below(n, c):
            group(n * ATTN_GROUP, False)
            return c

        lax.fori_loop(0, i * (nd // ATTN_GROUP), below, 0)
        for d in range(nd // ATTN_GROUP):
            group(i * nd + d * ATTN_GROUP, True)

    blk = _spec((tq, HEAD_DIM), lambda h, i: (i, h))
    col = _spec((T, HEAD_DIM), lambda h, i: (0, h))
    return pl.pallas_call(
        body, name="attn_bwd", grid=(H, T // tq),
        in_specs=[blk, col, col, blk, blk, _spec((2 * KEY_BLOCK, 2 * KEY_BLOCK), lambda h, i: (0, 0))],
        out_specs=[blk, col, col],
        out_shape=[jax.ShapeDtypeStruct((T, D), F32)] * 3,
        scratch_shapes=[pltpu.VMEM((tq, KEY_BLOCK), F32), pltpu.VMEM((tq, KEY_BLOCK), F32)],
        compiler_params=_params("parallel", "arbitrary"),
    )(qs, kn, vb, tot, do, _sum_matrix(True))


def _shift_down(v, n):
    t = lax.broadcasted_iota(jnp.int32, v.shape, 0)
    return jnp.where(t >= n, pltpu.roll(v, n, 0), 0.0)


def _shift_up(v, n):
    rows = v.shape[0]
    t = lax.broadcasted_iota(jnp.int32, v.shape, 0)
    return jnp.where(t < rows - n, pltpu.roll(v, rows - n, 0), 0.0)


def _pool_window(j, cw, D):
    group = (j * cw) // (D // N_POOL_GROUPS)
    return jnp.left_shift(2, group)


def _pool_count(shape, w):
    t = lax.broadcasted_iota(jnp.int32, shape, 0)
    return jnp.minimum(t + 1, w).astype(F32)


def _pool_fwd(h):
    T, D = h.shape
    cw = min(256, D // N_POOL_GROUPS)

    def body(h_ref, p_ref):
        w = _pool_window(pl.program_id(0), cw, D)
        hv = h_ref[...]
        s = hv
        for n in (1, 2, 4, 8):
            s = jnp.where(n < w, s + _shift_down(s, n), s)
        p_ref[...] = (s / _pool_count(hv.shape, w) - hv).astype(p_ref.dtype)

    slab = _spec((T, cw), lambda j: (0, j))
    return pl.pallas_call(
        body, name="pool_fwd", grid=(D // cw,), in_specs=[slab], out_specs=slab,
        out_shape=jax.ShapeDtypeStruct((T, D), BF16), compiler_params=_params("parallel"),
    )(h)


def _pool_bwd(dp):
    T, D = dp.shape
    cw = min(256, D // N_POOL_GROUPS)

    def body(dp_ref, dh_ref):
        w = _pool_window(pl.program_id(0), cw, D)
        dpv = dp_ref[...]
        s = dpv / _pool_count(dpv.shape, w)
        for n in (1, 2, 4, 8):
            s = jnp.where(n < w, s + _shift_up(s, n), s)
        dh_ref[...] = (s - dpv).astype(dh_ref.dtype)

    slab = _spec((T, cw), lambda j: (0, j))
    return pl.pallas_call(
        body, name="pool_bwd", grid=(D // cw,), in_specs=[slab], out_specs=slab,
        out_shape=jax.ShapeDtypeStruct((T, D), F32), compiler_params=_params("parallel"),
    )(dp)


def _pool_scale_bwd(dx, ypre, scale):
    T, D = dx.shape
    tm = _pick(T, 256, 8)

    def body(dx_ref, y_ref, s_ref, dys_ref, ds_ref):
        dxv = dx_ref[...]
        dys_ref[...] = (dxv * s_ref[...]).astype(BF16)
        part = jnp.sum(dxv * y_ref[...].astype(F32), axis=0, keepdims=True)

        @pl.when(pl.program_id(0) == 0)
        def _():
            ds_ref[...] = part

        @pl.when(pl.program_id(0) > 0)
        def _():
            ds_ref[...] += part

    row = _spec((tm, D), lambda i: (i, 0))
    vec = _spec((1, D), lambda i: (0, 0))
    return pl.pallas_call(
        body, name="pool_scale_bwd", grid=(T // tm,), in_specs=[row, row, vec], out_specs=[row, vec],
        out_shape=[jax.ShapeDtypeStruct((T, D), BF16), jax.ShapeDtypeStruct((1, D), F32)],
        compiler_params=_params("arbitrary"),
    )(dx, ypre, scale)


def _conv_specs(T, D, cw, cs):
    part = lambda p: _spec((None, T, cw), lambda j, p=p: (p, 0, j))
    taps = _spec((None, 8, cw), lambda j: (j // (cs // cw), 0, j % (cs // cw)))
    return part, taps


def _conv_fwd(bcx, taps4):
    _, T, D = bcx.shape
    cs = taps4.shape[2]
    cw = min(128, cs)
    part, taps = _conv_specs(T, D, cw, cs)

    def body(b_ref, c_ref, u_ref, w_ref, q_ref):
        g = c_ref[...].astype(F32) * u_ref[...].astype(F32)
        w = w_ref[...]
        y = w[2:3] * g + w[1:2] * _shift_down(g, 1) + w[0:1] * _shift_down(g, 2)
        q_ref[...] = (b_ref[...].astype(F32) * y).astype(q_ref.dtype)

    return pl.pallas_call(
        body, name="conv_fwd", grid=(D // cw,), in_specs=[part(0), part(1), part(2), taps],
        out_specs=_spec((T, cw), lambda j: (0, j)),
        out_shape=jax.ShapeDtypeStruct((T, D), BF16), compiler_params=_params("parallel"),
    )(bcx, bcx, bcx, taps4)


def _conv_bwd(dq, bcx, taps4):
    _, T, D = bcx.shape
    cs = taps4.shape[2]
    cw = min(128, cs)
    part, taps = _conv_specs(T, D, cw, cs)

    def body(dq_ref, b_ref, c_ref, u_ref, w_ref, d_ref, dw_ref):
        b = b_ref[...].astype(F32)
        c = c_ref[...].astype(F32)
        uu = u_ref[...].astype(F32)
        dqv = dq_ref[...].astype(F32)
        w = w_ref[...]
        g = c * uu
        g1 = _shift_down(g, 1)
        g2 = _shift_down(g, 2)
        d_ref[0] = (dqv * (w[2:3] * g + w[1:2] * g1 + w[0:1] * g2)).astype(BF16)
        dy = dqv * b
        dg = w[2:3] * dy + w[1:2] * _shift_up(dy, 1) + w[0:1] * _shift_up(dy, 2)
        d_ref[1] = (dg * uu).astype(BF16)
        d_ref[2] = (dg * c).astype(BF16)
        dw_ref[0:1, :] = jnp.sum(dy * g2, axis=0, keepdims=True)
        dw_ref[1:2, :] = jnp.sum(dy * g1, axis=0, keepdims=True)
        dw_ref[2:3, :] = jnp.sum(dy * g, axis=0, keepdims=True)
        dw_ref[3:8, :] = jnp.zeros((5, cw), F32)

    return pl.pallas_call(
        body, name="conv_bwd", grid=(D // cw,),
        in_specs=[_spec((T, cw), lambda j: (0, j)), part(0), part(1), part(2), taps],
        out_specs=[_spec((3, T, cw), lambda j: (0, 0, j)), taps],
        out_shape=[jax.ShapeDtypeStruct((3, T, D), BF16), jax.ShapeDtypeStruct((N_SHARD, 8, cs), F32)],
        compiler_params=_params("parallel"),
    )(dq, bcx, bcx, bcx, taps4)


def _quarter():
    return 2 * lax.axis_index("x") + lax.axis_index("y")


def _cast_into_slot(w, dtype, row0=0, rows=None):
    R, C = (w.shape[0] if rows is None else rows), w.shape[1]
    tr = _pick(math.gcd(R, row0), 512, 8)
    first = row0 // tr

    def body(w_ref, o_ref):
        o_ref[...] = w_ref[...].astype(o_ref.dtype)

    return pl.pallas_call(
        body, name="cast_into_slot", grid=(R // tr,),
        in_specs=[_spec((tr, C), lambda i: (first + i, 0))],
        out_specs=_spec((None, tr, C), lambda i: (_quarter(), i, 0)),
        out_shape=jax.ShapeDtypeStruct((N_SHARD, R, C), dtype), compiler_params=_params("parallel"),
    )(w)


def _sum_into(recv, own, l, L, prev):
    n, R2, C = recv.shape
    tr = _pick(R2, 256, 8)
    nb = R2 // tr

    def body(r_ref, o_ref, *rest):
        s = o_ref[...].astype(F32)
        for k in range(n):
            s = s + r_ref[k].astype(F32)
        rest[-1][...] = s

    in_specs = [_spec((n, tr, C), lambda i: (0, i, 0)),
                _spec((None, tr, C), lambda i: (_quarter(), lax.axis_index("c") * nb + i, 0))]
    args = [recv, own]
    if prev is not None:
        in_specs.append(ANY)
        args.append(prev)
    return pl.pallas_call(
        body, name="sum_into", grid=(nb,), in_specs=in_specs,
        out_specs=_spec((None, tr, C), lambda i: (l, lax.axis_index("c") * nb + i, 0)),
        out_shape=jax.ShapeDtypeStruct((L, 2 * R2, C), F32),
        input_output_aliases={} if prev is None else {2: 0},
        compiler_params=_params("parallel"),
    )(*args)


def _sum_devices(parts):
    n, R, C = parts.shape
    tr = _pick(R, 256, 8)

    def body(p_ref, o_ref):
        s = p_ref[0].astype(F32)
        for d in range(1, n):
            s = s + p_ref[d].astype(F32)
        o_ref[...] = s

    return pl.pallas_call(
        body, name="sum_devices", grid=(R // tr,),
        in_specs=[_spec((n, tr, C), lambda i: (0, i, 0))], out_specs=_spec((tr, C), lambda i: (i, 0)),
        out_shape=jax.ShapeDtypeStruct((R, C), F32), compiler_params=_params("parallel"),
    )(parts)


def _adamw(w, g, m, v):
    R, C = w.shape
    tr = _pick(R, 256, 8)

    def body(w_ref, g_ref, m_ref, v_ref, d_ref, nm_ref, nv_ref):
        gv = g_ref[...]
        m2 = ADAM_B1 * m_ref[...] + (1.0 - ADAM_B1) * gv
        v2 = ADAM_B2 * v_ref[...] + (1.0 - ADAM_B2) * (gv * gv)
        m_hat = m2 / (1.0 - ADAM_B1 ** ADAM_STEP)
        v_hat = v2 / (1.0 - ADAM_B2 ** ADAM_STEP)
        d_ref[...] = -ADAM_LR * (m_hat / (jnp.sqrt(v_hat) + ADAM_EPS) + ADAM_WD * w_ref[...])
        nm_ref[...] = m2
        nv_ref[...] = v2

    row = _spec((tr, C), lambda i: (i, 0))
    return pl.pallas_call(
        body, name="adamw", grid=(R // tr,), in_specs=[row] * 4, out_specs=[row] * 3,
        out_shape=[jax.ShapeDtypeStruct((R, C), F32)] * 3, compiler_params=_params("parallel"),
    )(w, g, m, v)


def _place():
    return lax.axis_index("x"), lax.axis_index("y"), lax.axis_index("c")


def _half(ref_rows, c):
    return pl.ds(c * (ref_rows // 2), ref_rows // 2)


def _allgather_shards(bufs):
    n = len(bufs)

    def body(*refs):
        outs = refs[n:2 * n]
        send, recv = refs[2 * n:]
        x, y, c = _place()
        chips = [(1 - x, y), (x, 1 - y), (1 - x, 1 - y)]

        def copy(a, k, quarter, core, to):
            part = outs[a].at[quarter, _half(outs[a].shape[1], core)]
            return pltpu.make_async_remote_copy(
                src_ref=part, dst_ref=part, send_sem=send.at[a, k], recv_sem=recv.at[a, k],
                device_id=to, device_id_type=MESH)

        first, passed = [], []
        for a in range(n):
            for k, (cx, cy) in enumerate(chips):
                cp = copy(a, k, 2 * x + y, c, (cx, cy, c))
                cp.start()
                first.append(cp)
        for a in range(n):
            for k, (cx, cy) in enumerate(chips):
                copy(a, k, 2 * cx + cy, c, (x, y, c)).wait_recv()
                cp = copy(a, 3 + k, 2 * cx + cy, c, (x, y, 1 - c))
                cp.start()
                passed.append(cp)
        for a in range(n):
            for k, (cx, cy) in enumerate(chips):
                copy(a, 3 + k, 2 * cx + cy, 1 - c, (x, y, 1 - c)).wait_recv()
        for cp in first + passed:
            cp.wait_send()

    return pl.pallas_call(
        body, name="allgather_shards", in_specs=[ANY] * n, out_specs=[ANY] * n,
        out_shape=[jax.ShapeDtypeStruct(b.shape, b.dtype) for b in bufs],
        input_output_aliases={a: a for a in range(n)},
        scratch_shapes=[pltpu.SemaphoreType.DMA((n, 6)), pltpu.SemaphoreType.DMA((n, 6))],
        compiler_params=pltpu.CompilerParams(has_side_effects=True),
    )(*bufs)


def _flips():
    return [(fx, fy, fc) for fx in (0, 1) for fy in (0, 1) for fc in (0, 1) if (fx, fy, fc) != (0, 0, 0)]


HBM = pl.BlockSpec(memory_space=pltpu.HBM)
SEM = pl.BlockSpec(memory_space=pltpu.SEMAPHORE)
DATAFLOW = pltpu.SideEffectType.DATAFLOW_SIDE_EFFECTING


def _gather_copies(bufs, send, recv):
    x, y, c = _place()
    pairs = []
    for a in range(len(bufs)):
        for k, (cx, cy) in enumerate([(1 - x, y), (x, 1 - y), (1 - x, 1 - y)]):
            def copy(quarter, a=a, k=k, cx=cx, cy=cy):
                part = bufs[a].at[quarter, _half(bufs[a].shape[1], c)]
                return pltpu.make_async_remote_copy(
                    src_ref=part, dst_ref=part, send_sem=send.at[3 * a + k], recv_sem=recv.at[3 * a + k],
                    device_id=(cx, cy, c), device_id_type=MESH)
            pairs.append((copy(2 * x + y), copy(2 * cx + cy)))
    return pairs


def _gather_start(name, bufs, after):
    n = len(bufs)

    def body(*refs):
        for out, _ in _gather_copies(refs[:n], refs[n + 1], refs[n + 2]):
            out.start()
        refs[-1][...] = jnp.zeros_like(refs[-1])

    outs = pl.pallas_call(
        body, name=name, in_specs=[HBM] * n + [ANY],
        out_specs=(SEM, SEM) + (HBM,) * n + (pl.BlockSpec(memory_space=pltpu.VMEM),),
        out_shape=(pltpu.SemaphoreType.DMA((3 * n,)), pltpu.SemaphoreType.DMA((3 * n,)))
        + tuple(pltpu.HBM(b.shape, b.dtype) for b in bufs) + (jax.ShapeDtypeStruct((8, 128), F32),),
        input_output_aliases={a: 2 + a for a in range(n)},
        compiler_params=pltpu.CompilerParams(has_side_effects=DATAFLOW),
    )(*[pltpu.with_memory_space_constraint(b, pltpu.HBM) for b in bufs], after)
    return outs[:-1], outs[-1]


def _gather_wait(name, started, after):
    n = len(started) - 2

    def body(*refs):
        for out, arriving in _gather_copies(refs[:n], refs[n], refs[n + 1]):
            out.wait_send()
            arriving.wait_recv()

    return pl.pallas_call(
        body, name=name, in_specs=[HBM] * n + [SEM, SEM, ANY], out_specs=(HBM,) * n,
        out_shape=tuple(pltpu.HBM(b.shape, b.dtype) for b in started[2:]),
        input_output_aliases={a: a for a in range(n)},
        compiler_params=pltpu.CompilerParams(has_side_effects=DATAFLOW),
    )(*started[2:], started[0], started[1], after)


def _forward_halves(bufs):
    n = len(bufs)

    def body(*refs):
        outs = refs[n:2 * n]
        send, recv = refs[2 * n:]
        x, y, c = _place()

        def copy(a, k, quarter, core):
            part = outs[a].at[quarter, _half(outs[a].shape[1], core)]
            return pltpu.make_async_remote_copy(
                src_ref=part, dst_ref=part, send_sem=send.at[a, k], recv_sem=recv.at[a, k],
                device_id=(x, y, 1 - c), device_id_type=MESH)

        others = [2 * (1 - x) + y, 2 * x + (1 - y), 2 * (1 - x) + (1 - y)]
        for a in range(n):
            for k, quarter in enumerate(others):
                copy(a, k, quarter, c).start()
        for a in range(n):
            for k, quarter in enumerate(others):
                copy(a, k, quarter, 1 - c).wait_recv()
        for a in range(n):
            for k, quarter in enumerate(others):
                copy(a, k, quarter, c).wait_send()

    return pl.pallas_call(
        body, name="forward_halves", in_specs=[ANY] * n, out_specs=[ANY] * n,
        out_shape=[jax.ShapeDtypeStruct(b.shape, b.dtype) for b in bufs],
        input_output_aliases={a: a for a in range(n)},
        scratch_shapes=[pltpu.SemaphoreType.DMA((n, 3)), pltpu.SemaphoreType.DMA((n, 3))],
        compiler_params=pltpu.CompilerParams(has_side_effects=True),
    )(*bufs)


def _exchange_copies(grads, zones, send, recv):
    x, y, c = _place()
    copies = []
    for a in range(len(grads)):
        rows = grads[a].shape[1]
        for k, (fx, fy, fc) in enumerate(_flips()):
            px, py, pc = x ^ fx, y ^ fy, c ^ fc
            j = a * (N_DEV - 1) + k
            copies.append(pltpu.make_async_remote_copy(
                src_ref=grads[a].at[2 * px + py, _half(rows, pc)], dst_ref=zones[a].at[k],
                send_sem=send.at[j], recv_sem=recv.at[j], device_id=(px, py, pc), device_id_type=MESH))
    return copies


def _exchange_start(name, grads):
    n = len(grads)
    zones = [lax.empty((N_DEV - 1, g.shape[1] // 2, g.shape[2]), g.dtype) for g in grads]

    def body(*refs):
        for cp in _exchange_copies(refs[:n], refs[n:2 * n], refs[2 * n], refs[2 * n + 1]):
            cp.start()
        refs[-1][...] = jnp.zeros_like(refs[-1])

    outs = pl.pallas_call(
        body, name=name, in_specs=[HBM] * (2 * n),
        out_specs=(SEM, SEM) + (HBM,) * (2 * n) + (pl.BlockSpec(memory_space=pltpu.VMEM),),
        out_shape=(pltpu.SemaphoreType.DMA((n * (N_DEV - 1),)), pltpu.SemaphoreType.DMA((n * (N_DEV - 1),)))
        + tuple(pltpu.HBM(v.shape, v.dtype) for v in list(grads) + zones) + (jax.ShapeDtypeStruct((8, 128), F32),),
        input_output_aliases={a: 2 + a for a in range(2 * n)},
        compiler_params=pltpu.CompilerParams(has_side_effects=DATAFLOW),
    )(*[pltpu.with_memory_space_constraint(v, pltpu.HBM) for v in list(grads) + zones])
    return outs[:-1], outs[-1]


def _exchange_wait(name, started, after):
    send, recv = started[0], started[1]
    n = (len(started) - 2) // 2

    def body(*refs):
        for cp in _exchange_copies(refs[:n], refs[n:2 * n], refs[2 * n], refs[2 * n + 1]):
            cp.wait_send()
            cp.wait_recv()

    outs = pl.pallas_call(
        body, name=name, in_specs=[HBM] * (2 * n) + [SEM, SEM, ANY], out_specs=(HBM,) * (2 * n),
        out_shape=tuple(pltpu.HBM(v.shape, v.dtype) for v in started[2:]),
        input_output_aliases={a: a for a in range(2 * n)},
        compiler_params=pltpu.CompilerParams(has_side_effects=DATAFLOW),
    )(*started[2:], send, recv, after)
    return outs[:n], outs[n:]


def _share_halves(fulls):
    n = len(fulls)
    index = [(a, l) for a in range(n) for l in range(fulls[a].shape[0])]

    def body(*refs):
        outs = refs[n:2 * n]
        send, recv = refs[2 * n:]
        x, y, c = _place()

        def copy(j, core):
            a, l = index[j]
            part = outs[a].at[l, _half(outs[a].shape[1], core)]
            return pltpu.make_async_remote_copy(
                src_ref=part, dst_ref=part, send_sem=send.at[j], recv_sem=recv.at[j],
                device_id=(x, y, 1 - c), device_id_type=MESH)

        for j in range(len(index)):
            copy(j, c).start()
        for j in range(len(index)):
            copy(j, 1 - c).wait_recv()
        for j in range(len(index)):
            copy(j, c).wait_send()

    return pl.pallas_call(
        body, name="share_halves", in_specs=[ANY] * n, out_specs=[ANY] * n,
        out_shape=[jax.ShapeDtypeStruct(f.shape, f.dtype) for f in fulls],
        input_output_aliases={a: a for a in range(n)},
        scratch_shapes=[pltpu.SemaphoreType.DMA((len(index),)), pltpu.SemaphoreType.DMA((len(index),))],
        compiler_params=pltpu.CompilerParams(has_side_effects=True),
    )(*fulls)


def _allgather_small(v):
    def body(v_ref, o_ref, send, recv, lsem):
        x, y, c = _place()
        me = 4 * x + 2 * y + c
        own = pltpu.make_async_copy(v_ref, o_ref.at[me], lsem)
        own.start()
        sends = []
        for k, (fx, fy, fc) in enumerate(_flips()):
            cp = pltpu.make_async_remote_copy(
                src_ref=v_ref, dst_ref=o_ref.at[me], send_sem=send.at[k], recv_sem=recv.at[k],
                device_id=(x ^ fx, y ^ fy, c ^ fc), device_id_type=MESH)
            cp.start()
            sends.append(cp)
        for k, (fx, fy, fc) in enumerate(_flips()):
            px, py, pc = x ^ fx, y ^ fy, c ^ fc
            pltpu.make_async_remote_copy(
                src_ref=v_ref, dst_ref=o_ref.at[4 * px + 2 * py + pc], send_sem=send.at[k], recv_sem=recv.at[k],
                device_id=(px, py, pc), device_id_type=MESH).wait_recv()
        for cp in sends:
            cp.wait_send()
        own.wait()

    return pl.pallas_call(
        body, name="allgather_small", in_specs=[ANY], out_specs=ANY,
        out_shape=jax.ShapeDtypeStruct((N_DEV,) + v.shape, v.dtype),
        scratch_shapes=[pltpu.SemaphoreType.DMA((7,)), pltpu.SemaphoreType.DMA((7,)), pltpu.SemaphoreType.DMA(())],
        compiler_params=pltpu.CompilerParams(has_side_effects=True),
    )(v)


def _mm_col_fwd(name, a, w4, l, cb, out_dtype, parts, epi_act=None, w4b=None, tm_pref=512, after=()):
    T, K = a.shape
    cs = w4.shape[3]
    N = N_SHARD * cs
    tm = _pick(T, tm_pref, 8)
    nps = cs // cb
    npp = (N // parts) // cb
    a_spec = _spec((tm, K), lambda j, i, k: (i, 0))
    b_spec = _spec((None, None, K, cb), lambda j, i, k: (j // nps, l, 0, j % nps))
    if parts == 1:
        o_spec = _spec((tm, cb), lambda j, i, k: (i, j))
        o_shape = (T, N)
    else:
        o_spec = _spec((None, tm, cb), lambda j, i, k: (j // npp, i, j % npp))
        o_shape = (parts, T, N // parts)
    b_ops = [(w4, b_spec)] if w4b is None else [(w4, b_spec), (w4b, b_spec)]
    dots = [(0, 0, 0)] if w4b is None else [(0, 0, 0), (0, 1, 1)]
    out_dtypes = (out_dtype,) if w4b is None else (F32, F32, out_dtype)
    return _mm(name, (N // cb, T // tm, 1), [(a, a_spec)], b_ops, dots, NN, [(tm, cb)] * len(b_ops),
               [(o_shape, dt, o_spec) for dt in out_dtypes], epi=epi_act, after=after)


def _mm_row_fwd(name, a, w4, l, res):
    T, K = a.shape
    rs, N = w4.shape[2], w4.shape[3]
    tm = _pick(T, 256, 8)
    tn = _pick(N, 1024, 128)
    a_ops = [(a, _spec((tm, rs), lambda j, i, k, s=s: (i, s))) for s in range(N_SHARD)]
    b_ops = [(w4, _spec((None, None, rs, tn), lambda j, i, k, s=s: (s, l, 0, j))) for s in range(N_SHARD)]
    o_spec = _spec((tm, tn), lambda j, i, k: (i, j))
    return _mm(name, (N // tn, T // tm, 1), a_ops, b_ops, [(s, s, 0) for s in range(N_SHARD)], NN, [(tm, tn)],
               [((T, N), F32, o_spec)], epi=lambda accs, ex: [accs[0] + ex[0]], extras=[(res, o_spec)])


def _mm_row_bwd_data(name, dy, w4, l, out_dtype=BF16, epi=None, extras=(), n_out=1, tm_pref=512, after=()):
    T, N = dy.shape
    rs = w4.shape[2]
    K = N_SHARD * rs
    tm = _pick(T, tm_pref, 8)
    a_spec = _spec((tm, N), lambda j, i, k: (i, 0))
    b_spec = _spec((None, None, rs, N), lambda j, i, k: (j, l, 0, 0))
    o_spec = _spec((tm, rs), lambda j, i, k: (i, j))
    return _mm(name, (N_SHARD, T // tm, 1), [(dy, a_spec)], [(w4, b_spec)], [(0, 0, 0)], NT, [(tm, rs)],
               [((T, K), out_dtype, o_spec)] * n_out, epi=epi, extras=[(e, o_spec) for e in extras], after=after)


def _mm_row_bwd_weight(name, a, dy, rs):
    T, K = a.shape
    N = dy.shape[1]
    tk = _pick(T, 512, 8)
    tn = _pick(N, 1024, 128)
    a_spec = _spec((tk, rs), lambda i, j, k: (k, i))
    b_spec = _spec((tk, tn), lambda i, j, k: (k, j))
    o_spec = _spec((None, rs, tn), lambda i, j, k: (i, 0, j))
    return _mm(name, (N_SHARD, N // tn, T // tk), [(a, a_spec)], [(dy, b_spec)], [(0, 0, 0)], TN, [(rs, tn)],
               [((N_SHARD, rs, N), BF16, o_spec)])[0]


def _mm_col_bwd_data(name, dys, w4s, l, cb, parts, tm_pref=512, after=()):
    T = dys[0].shape[-2]
    K, cs = w4s[0].shape[2], w4s[0].shape[3]
    N = N_SHARD * cs
    tm = _pick(T, tm_pref, 8)
    nps = cs // cb
    npp = (N // parts) // cb
    if parts == 1:
        a_spec = _spec((tm, cb), lambda i, j, k: (i, k))
    else:
        a_spec = _spec((None, tm, cb), lambda i, j, k: (k // npp, i, k % npp))
    b_spec = _spec((None, None, K, cb), lambda i, j, k: (k // nps, l, 0, k % nps))
    o_spec = _spec((tm, K), lambda i, j, k: (i, 0))
    return _mm(name, (T // tm, 1, N // cb), [(d, a_spec) for d in dys], [(w, b_spec) for w in w4s],
               [(p, p, 0) for p in range(len(dys))], NT, [(tm, K)], [((T, K), F32, o_spec)], after=after)[0]


def _mm_ffn_bwd_data(name, dys, w4s, l, tm_pref=256, tn_pref=512):
    T = dys[0].shape[0]
    K, cs = w4s[0].shape[2], w4s[0].shape[3]
    tm = _pick(T, tm_pref, 8)
    tn = _pick(K, tn_pref, 128)
    a_ops = [(d, _spec((tm, cs), lambda j, i, k, s=s: (i, s))) for d in dys for s in range(N_SHARD)]
    b_ops = [(w, _spec((None, None, tn, cs), lambda j, i, k, s=s: (s, l, j, 0))) for w in w4s for s in range(N_SHARD)]
    n = len(a_ops)
    return _mm(name, (K // tn, T // tm, 1), a_ops, b_ops, [(p, p, 0) for p in range(n)], NT, [(tm, tn)],
               [((T, K), F32, _spec((tm, tn), lambda j, i, k: (i, j)))])[0]


def _mm_col_bwd_weight(name, a, dys, cs, cb, parts, tm_pref=512, chunks=1):
    T, K = a.shape
    N = N_SHARD * cs
    tk = _pick(T // chunks, 512, 8)
    tm = _pick(K, tm_pref, 128)
    nps = cs // cb
    npp = (N // parts) // cb
    a_ops = [(a, _spec((tk, tm), lambda i, j, k, c=c: (k * chunks + c, i))) for c in range(chunks)]
    if parts == 1:
        b_ops = [(d, _spec((tk, cb), lambda i, j, k, c=c: (k * chunks + c, j))) for d in dys for c in range(chunks)]
    else:
        b_ops = [(d, _spec((None, tk, cb), lambda i, j, k, c=c: (j // npp, k * chunks + c, j % npp)))
                 for d in dys for c in range(chunks)]
    o_spec = _spec((None, tm, cb), lambda i, j, k: (j // nps, i, j % nps))
    nd = len(dys)
    return _mm(name, (K // tm, N // cb, T // (tk * chunks)), a_ops, b_ops,
               [(c, p * chunks + c, p) for p in range(nd) for c in range(chunks)], TN, [(tm, cb)] * nd,
               [((N_SHARD, K, cs), BF16, o_spec)] * nd)


def _swiglu(accs, _):
    g, up = accs
    return [g, up, g * jax.nn.sigmoid(g) * up]


def _swiglu_bwd(accs, ex):
    da = accs[0]
    g = ex[0].astype(F32)
    up = ex[1].astype(F32)
    s = jax.nn.sigmoid(g)
    return [da * up * (s * (1.0 + g * (1.0 - s))), da * (g * s)]


def _ffn_fwd(x1, g_ffn, wg4, wu4, wd4, l):
    h2 = _rmsnorm_fwd(x1, g_ffn, BF16)
    fs = wg4.shape[3]
    gate, up, act = _mm_col_fwd("ffn_up", h2, wg4, l, fs, BF16, 1, epi_act=_swiglu, w4b=wu4, tm_pref=256)
    x2 = _mm_row_fwd("ffn_down", act, wd4, l, x1)[0]
    return x2, (x1, h2, gate, up, act)


def _ffn_bwd(dx2, dx2b, saved, g_ffn, wg4, wu4, wd4, l, after):
    x1, h2, gate, up, act = saved
    fs = wg4.shape[3]
    dgate, dup = _mm_row_bwd_data("ffn_down_bwd_data", dx2b, wd4, l, epi=_swiglu_bwd, extras=(gate, up), n_out=2,
                                  tm_pref=256, after=after)
    g_down = _mm_row_bwd_weight("ffn_down_bwd_weight", act, dx2b, fs)
    dh2 = _mm_ffn_bwd_data("ffn_up_bwd_data", [dgate, dup], [wg4, wu4], l)
    g_gate, g_up = _mm_col_bwd_weight("ffn_up_bwd_weight", h2, [dgate, dup], fs, fs, 1, chunks=4)
    dx1, dx1b, dg = _rmsnorm_bwd(x1, g_ffn, dh2, dx2)
    return dx1, dx1b, dg, g_gate, g_up, g_down


def _sb_fwd(x, g_mix, gq, gk, wqkv4, wo4, l, after=()):
    D = x.shape[1]
    h = _rmsnorm_fwd(x, g_mix, BF16)
    qkv = _mm_col_fwd("qkv_proj", h, wqkv4, l, D // 4, F32, 3, after=after)[0]
    qs, kn, vb = _headnorm_fwd(qkv, gq, gk)
    o, tot = _attn_fwd(qs, kn, vb)
    x1 = _mm_row_fwd("attn_out", o, wo4, l, x)[0]
    return x1, (x, h, qkv, qs, kn, vb, o, tot)


def _sb_bwd(dx1, dx1b, saved, g_mix, gq, gk, wqkv4, wo4, l, after, exchange):
    x, h, qkv, qs, kn, vb, o, tot = saved
    D = x.shape[1]
    do = _mm_row_bwd_data("attn_out_bwd_data", dx1b, wo4, l, after=after)[0]
    g_wo = _mm_row_bwd_weight("attn_out_bwd_weight", o, dx1b, wo4.shape[2])
    dqs, dkn, dv = _attn_bwd(qs, kn, vb, tot, do)
    dqkv, dgq, dgk = _headnorm_bwd(qkv, gq, gk, dqs, dkn, dv)
    g_wqkv = _mm_col_bwd_weight("qkv_bwd_weight", h, [dqkv], wqkv4.shape[3], D // 4, 3, tm_pref=2048)[0]
    token = exchange([g_wqkv, g_wo])
    dh = _mm_col_bwd_data("qkv_bwd_data", [dqkv], [wqkv4], l, D // 4, 3, after=token)
    dx, dxb, dg = _rmsnorm_bwd(x, g_mix, dh, dx1)
    return dx, dxb, dg, dgq, dgk, token


def _pool_mix_fwd(x, g_mix, wp4, scale):
    T, D = x.shape
    C = D // N_POOL_GROUPS
    rq = C // N_SHARD
    h = _rmsnorm_fwd(x, g_mix, F32)
    p = _pool_fwd(h)
    tm = _pick(T, 512, 8)
    a_spec = _spec((tm, rq), lambda g, i, k: (i, g * N_SHARD + k))
    b_spec = _spec((None, None, rq, C), lambda g, i, k: (k, 0, g, 0))
    o_spec = _spec((tm, C), lambda g, i, k: (i, g))
    s_spec = _spec((1, C), lambda g, i, k: (0, g))
    x1, ypre = _mm("pool_mix", (N_POOL_GROUPS, T // tm, N_SHARD), [(p, a_spec)], [(wp4, b_spec)], [(0, 0, 0)], NN,
                   [(tm, C)], [((T, D), F32, o_spec), ((T, D), BF16, o_spec)],
                   epi=lambda accs, ex: [ex[0] + accs[0] * ex[1], accs[0]], extras=[(x, o_spec), (scale, s_spec)])
    return x1, (x, h, p, ypre)


def _pool_mix_bwd(dx1, dx1b, saved, g_mix, wp4, scale, after=()):
    x, h, p, ypre = saved
    T, D = x.shape
    C = D // N_POOL_GROUPS
    rq = C // N_SHARD
    dys, dscale = _pool_scale_bwd(dx1, ypre, scale)
    tm = _pick(T, 512, 8)
    dp = _mm("pool_mix_bwd_data", (N_POOL_GROUPS * N_SHARD, T // tm, 1),
             [(dys, _spec((tm, C), lambda j, i, k: (i, j // N_SHARD)))],
             [(wp4, _spec((None, None, rq, C), lambda j, i, k: (j % N_SHARD, 0, j // N_SHARD, 0)))],
             [(0, 0, 0)], NT, [(tm, rq)], [((T, D), F32, _spec((tm, rq), lambda j, i, k: (i, j)))], after=after)[0]
    tk = _pick(T, 512, 8)
    g_wp = _mm("pool_mix_bwd_weight", (N_POOL_GROUPS * N_SHARD, 1, T // tk),
               [(p, _spec((tk, rq), lambda j, n, k: (k, j)))],
               [(dys, _spec((tk, C), lambda j, n, k: (k, j // N_SHARD)))],
               [(0, 0, 0)], TN, [(rq, C)],
               [((N_SHARD, N_POOL_GROUPS * rq, C), BF16,
                 _spec((None, rq, C), lambda j, n, k: (j % N_SHARD, j // N_SHARD, 0)))])[0]
    dh = _pool_bwd(dp)
    dx, dxb, dg = _rmsnorm_bwd(x, g_mix, dh, dx1)
    return dx, dxb, dg, dscale, g_wp


def _conv_mix_fwd(x, g_mix, win4, taps4, wout4):
    D = x.shape[1]
    h = _rmsnorm_fwd(x, g_mix, BF16)
    bcx = _mm_col_fwd("conv_in", h, win4, 0, D // 4, F32, 3)[0]
    q = _conv_fwd(bcx, taps4)
    x1 = _mm_row_fwd("conv_out", q, wout4, 0, x)[0]
    return x1, (x, h, bcx, q)


def _conv_mix_bwd(dx1, dx1b, saved, g_mix, win4, taps4, wout4, after=()):
    x, h, bcx, q = saved
    D = x.shape[1]
    dq = _mm_row_bwd_data("conv_out_bwd_data", dx1b, wout4, 0, out_dtype=F32, after=after)[0]
    g_wout = _mm_row_bwd_weight("conv_out_bwd_weight", q, dx1b, wout4.shape[2])
    dbcx, dtaps = _conv_bwd(dq, bcx, taps4)
    dh = _mm_col_bwd_data("conv_in_bwd_data", [dbcx], [win4], 0, D // 4, 3)
    g_win = _mm_col_bwd_weight("conv_in_bwd_weight", h, [dbcx], win4.shape[3], D // 4, 3, tm_pref=2048)[0]
    dx, dxb, dg = _rmsnorm_bwd(x, g_mix, dh, dx1)
    return dx, dxb, dg, dtaps, g_wout, g_win


def _rows2d(w):
    return w.reshape(-1, w.shape[-1])


def _pad_rows(v, rows):
    return jnp.pad(v, ((0, rows - v.shape[0]), (0, 0)))


def kernel(x, norm_mix_g, norm_ffn_g, sb_w_qkv, sb_g_q, sb_g_k, sb_w_o, pool_w, pool_scale, conv_w_in, conv_w, conv_w_out, ffn_w_gate, ffn_w_up, ffn_w_down, loss_target, m_norm_mix_g, m_norm_ffn_g, m_sb_w_qkv, m_sb_g_q, m_sb_g_k, m_sb_w_o, m_pool_w, m_pool_scale, m_conv_w_in, m_conv_w, m_conv_w_out, m_ffn_w_gate, m_ffn_w_up, m_ffn_w_down, v_norm_mix_g, v_norm_ffn_g, v_sb_w_qkv, v_sb_g_q, v_sb_g_k, v_sb_w_o, v_pool_w, v_pool_scale, v_conv_w_in, v_conv_w, v_conv_w_out, v_ffn_w_gate, v_ffn_w_up, v_ffn_w_down):
    T, D = x.shape[1], x.shape[2]
    depth = norm_mix_g.shape[0]
    big = dict(sb_w_qkv=(sb_w_qkv, m_sb_w_qkv, v_sb_w_qkv), sb_w_o=(sb_w_o, m_sb_w_o, v_sb_w_o),
               pool_w=(pool_w, m_pool_w, v_pool_w), conv_w_in=(conv_w_in, m_conv_w_in, v_conv_w_in),
               conv_w_out=(conv_w_out, m_conv_w_out, v_conv_w_out), ffn_w_gate=(ffn_w_gate, m_ffn_w_gate, v_ffn_w_gate),
               ffn_w_up=(ffn_w_up, m_ffn_w_up, v_ffn_w_up), ffn_w_down=(ffn_w_down, m_ffn_w_down, v_ffn_w_down))
    names = list(big)

    cs_conv = conv_w.shape[2]
    taps_local = _pad_rows(conv_w[0], 16)
    mixer0_names = ["sb_w_qkv", "sb_w_o"]
    ffn0_names = ["ffn_w_gate", "ffn_w_up", "ffn_w_down"]
    first_names = mixer0_names + ffn0_names
    per_layer = {n: _rows2d(big[n][0]).shape[0] // big[n][0].shape[0] for n in names}

    def layer0(n):
        return _cast_into_slot(_rows2d(big[n][0]), BF16, 0, per_layer[n])

    first = _allgather_shards([layer0(n) for n in mixer0_names])
    gathering0, token0 = _gather_start("gather_start_ffn0", [layer0(n) for n in ffn0_names], first[0])
    rest_bufs = []
    for n in names:
        skip = per_layer[n] if n in first_names else 0
        rest_bufs.append(_cast_into_slot(_rows2d(big[n][0]), BF16, skip, _rows2d(big[n][0]).shape[0] - skip))
    rest_bufs.append(_cast_into_slot(taps_local, F32))
    gathering, token = _gather_start("gather_start", rest_bufs, token0)

    def as_layers(g4, layers):
        return g4.reshape(N_SHARD, layers, g4.shape[1] // layers, g4.shape[2])

    w_first = {n: as_layers(g4, 1) for n, g4 in zip(mixer0_names, first)}
    w_rest, taps4 = {}, None

    def weights(layer, *tensors):
        if layer == 0 and tensors[0] in first_names:
            return [w_first[t] for t in tensors] + [0]
        return [w_rest[t] for t in tensors] + [layer - 1 if tensors[0] in first_names else layer]

    xs = x.reshape(T, D)
    saved = []
    for i in range(depth):
        kind, j = i % 3, i // 3
        g_mix = norm_mix_g[i:i + 1]
        if i == 1:
            gathered = _forward_halves(_gather_wait("gather_wait", gathering, xs))
            for n, g4 in zip(names, gathered[:-1]):
                w_rest[n] = as_layers(g4, big[n][0].shape[0] - (1 if n in first_names else 0))
            taps4 = gathered[-1]
        if kind == 0:
            xs, sv = _sb_fwd(xs, g_mix, sb_g_q[j:j + 1], sb_g_k[j:j + 1], *weights(j, "sb_w_qkv", "sb_w_o"),
                             after=(token,) if i == 0 else ())
        elif kind == 1:
            xs, sv = _pool_mix_fwd(xs, g_mix, w_rest["pool_w"], pool_scale[j:j + 1])
        else:
            xs, sv = _conv_mix_fwd(xs, g_mix, w_rest["conv_w_in"], taps4, w_rest["conv_w_out"])
        if i == 0:
            for n, g4 in zip(ffn0_names, _forward_halves(_gather_wait("gather_wait_ffn0", gathering0, xs))):
                w_first[n] = as_layers(g4, 1)
        xs, sf = _ffn_fwd(xs, norm_ffn_g[i:i + 1], *weights(i, "ffn_w_gate", "ffn_w_up", "ffn_w_down"))
        saved.append((sv, sf))

    dxs, dxb, err2 = _loss_head(xs, loss_target.reshape(T, D))
    loss = lax.psum(0.5 * jnp.sum(err2) / D, ("x", "y", "c"))
    started = []
    small = {}
    token = ()

    def exchange(name, grads, layout):
        handle, tok = _exchange_start("exchange_start_" + name, grads)
        started.append((name, handle, layout))
        return (tok,)

    for i in reversed(range(depth)):
        kind, j = i % 3, i // 3
        sv, sf = saved[i]
        dxs, dxb, dg, g_gate, g_up, g_down = _ffn_bwd(dxs, dxb, sf, norm_ffn_g[i:i + 1],
                                                      *weights(i, "ffn_w_gate", "ffn_w_up", "ffn_w_down"), token)
        small[("norm_ffn_g", i)] = dg
        token = exchange(f"ffn_{i}", [g_gate, g_up, g_down], [("ffn_w_gate", i), ("ffn_w_up", i), ("ffn_w_down", i)])
        g_mix = norm_mix_g[i:i + 1]
        if kind == 0:
            dxs, dxb, dg, dgq, dgk, token = _sb_bwd(
                dxs, dxb, sv, g_mix, sb_g_q[j:j + 1], sb_g_k[j:j + 1], *weights(j, "sb_w_qkv", "sb_w_o"), token,
                lambda grads, i=i, j=j: exchange(f"mix_{i}", grads, [("sb_w_qkv", j), ("sb_w_o", j)]))
            small[("sb_g_q", j)], small[("sb_g_k", j)] = dgq, dgk
        elif kind == 1:
            dxs, dxb, dg, dscale, g_wp = _pool_mix_bwd(dxs, dxb, sv, g_mix, w_rest["pool_w"], pool_scale[j:j + 1],
                                                       after=token)
            small[("pool_scale", j)] = dscale
            token = exchange(f"mix_{i}", [g_wp], [("pool_w", j)])
        else:
            dxs, dxb, dg, dtaps, g_wout, g_win = _conv_mix_bwd(dxs, dxb, sv, g_mix, w_rest["conv_w_in"], taps4,
                                                              w_rest["conv_w_out"], after=token)
            small[("conv_w", j)] = dtaps
            token = exchange(f"mix_{i}", [g_win, g_wout], [("conv_w_in", j), ("conv_w_out", j)])
        small[("norm_mix_g", i)] = dg
    grad_x = dxs.reshape(x.shape)

    full = {}
    done = dg
    for name, handle, layout in started:
        grads, zones = _exchange_wait("exchange_wait_" + name, handle, done)
        for g, z, (t, l) in zip(grads, zones, layout):
            full[t] = done = _sum_into(z, g, l, big[t][0].shape[0], full.get(t))
    tensors = sorted(full)
    g_big = dict(zip(tensors, _share_halves([full[t] for t in tensors])))

    n_sb = sb_g_q.shape[0]
    gqk = jnp.concatenate([small[(n, j)] for j in range(n_sb) for n in ("sb_g_q", "sb_g_k")], axis=1)
    dtaps = small[("conv_w", 0)]
    taps_full = jnp.concatenate([dtaps[s, :3] for s in range(N_SHARD)], axis=1)
    pack = jnp.concatenate(
        [small[("norm_mix_g", i)] for i in range(depth)] + [small[("norm_ffn_g", i)] for i in range(depth)]
        + [small[("pool_scale", 0)], jnp.pad(gqk, ((0, 0), (0, D - gqk.shape[1]))), taps_full], axis=0)
    pack = _pad_rows(pack, SMALL_ROWS)
    g_small = _sum_devices(_allgather_small(pack))
    mine = 2 * lax.axis_index("x") + lax.axis_index("y")
    g_taps = lax.dynamic_slice(g_small, (2 * depth + 2, mine * cs_conv), (3, cs_conv))

    def pack_small(norm_mix, norm_ffn, scale, gq, gk, taps):
        qk = jnp.concatenate([v[j:j + 1] for j in range(n_sb) for v in (gq, gk)], axis=1)
        rows = jnp.concatenate([norm_mix, norm_ffn, scale, jnp.pad(qk, ((0, 0), (0, D - qk.shape[1]))),
                                jnp.pad(taps[0], ((0, 0), (0, D - cs_conv)))], axis=0)
        return _pad_rows(rows, SMALL_ROWS)

    g_pack = jnp.concatenate([g_small[:2 * depth + 2], jnp.pad(g_taps, ((0, 0), (0, D - cs_conv))),
                              jnp.zeros((SMALL_ROWS - 2 * depth - 5, D), F32)], axis=0)
    w_pack = pack_small(norm_mix_g, norm_ffn_g, pool_scale, sb_g_q, sb_g_k, conv_w)
    m_pack = pack_small(m_norm_mix_g, m_norm_ffn_g, m_pool_scale, m_sb_g_q, m_sb_g_k, m_conv_w)
    v_pack = pack_small(v_norm_mix_g, v_norm_ffn_g, v_pool_scale, v_sb_g_q, v_sb_g_k, v_conv_w)
    small_out = (g_pack,) + tuple(_adamw(w_pack, g_pack, m_pack, v_pack))

    def unpack_small(p):
        qk = p[2 * depth + 1]
        gq = jnp.stack([qk[(2 * j) * HEAD_DIM:(2 * j + 1) * HEAD_DIM] for j in range(n_sb)])
        gk = jnp.stack([qk[(2 * j + 1) * HEAD_DIM:(2 * j + 2) * HEAD_DIM] for j in range(n_sb)])
        return dict(norm_mix_g=p[:depth], norm_ffn_g=p[depth:2 * depth], pool_scale=p[2 * depth:2 * depth + 1],
                    sb_g_q=gq, sb_g_k=gk, conv_w=p[2 * depth + 2:2 * depth + 5, :cs_conv][None])

    results = [unpack_small(p) for p in small_out]
    for n in names:
        w, m, v = big[n]
        g = _rows2d(g_big[n])
        outs = (g,) + tuple(_adamw(_rows2d(w), g, _rows2d(m), _rows2d(v)))
        for r, o in zip(results, outs):
            r[n] = o.reshape(w.shape)

    order = ["norm_mix_g", "norm_ffn_g", "sb_w_qkv", "sb_g_q", "sb_g_k", "sb_w_o", "pool_w", "pool_scale",
             "conv_w_in", "conv_w", "conv_w_out", "ffn_w_gate", "ffn_w_up", "ffn_w_down"]
    return (loss, grad_x) + tuple(r[n] for r in results for n in order)
```

```python
import math

import jax
import jax.numpy as jnp
from jax import lax
from jax.experimental import pallas as pl
from jax.experimental.pallas import tpu as pltpu

F32 = jnp.float32
BF16 = jnp.bfloat16

HEAD_DIM = 128
N_POOL_GROUPS = 4
EPS = 1e-6
N_SHARD = 4
N_DEV = 8
VMEM_LIMIT_BYTES = 56 * 2**20
KEY_BLOCK = 128
ATTN_ROWS = 512
ATTN_GROUP_FWD = 4
ATTN_GROUP_BWD = 4
SMALL_ROWS = 16

ADAM_LR = 0.001
ADAM_B1 = 0.9
ADAM_B2 = 0.999
ADAM_EPS = 1e-08
ADAM_WD = 0.01
ADAM_STEP = 10

MESH = pl.DeviceIdType.MESH
ANY = pl.BlockSpec(memory_space=pl.ANY)


def _params(*sem):
    return pltpu.CompilerParams(dimension_semantics=sem, vmem_limit_bytes=VMEM_LIMIT_BYTES)


def _pick(n, pref, unit):
    t = (min(pref, n) // unit) * unit
    while n % t:
        t -= unit
    return t


NN = ((1,), (0,))
NT = ((1,), (1,))
TN = ((0,), (0,))


def _mm(name, grid, a_ops, b_ops, dots, dims, acc_shapes, outs, epi=None, extras=(), after=()):
    na, nb, ne, no, nacc = len(a_ops), len(b_ops), len(extras), len(outs), len(acc_shapes)
    nk = grid[2]
    n_in = na + nb + ne + len(after)

    def body(*refs):
        a_refs, b_refs = refs[:na], refs[na:na + nb]
        e_refs = refs[na + nb:na + nb + ne]
        o_refs = refs[n_in:n_in + no]
        acc_refs = refs[n_in + no:]

        def partial_sums():
            sums = [None] * nacc
            for ai, bi, ci in dots:
                d = lax.dot_general(a_refs[ai][...].astype(BF16), b_refs[bi][...].astype(BF16),
                                    (dims, ((), ())), preferred_element_type=F32)
                sums[ci] = d if sums[ci] is None else sums[ci] + d
            return sums

        def finish(accs):
            res = epi(accs, [e[...] for e in e_refs]) if epi is not None else accs
            for o, r in zip(o_refs, res):
                o[...] = r.astype(o.dtype)

        if nk == 1:
            finish(partial_sums())
            return
        k = pl.program_id(2)

        @pl.when(k == 0)
        def _():
            for acc, s in zip(acc_refs, partial_sums()):
                acc[...] = s

        @pl.when(k > 0)
        def _():
            for acc, s in zip(acc_refs, partial_sums()):
                acc[...] += s

        @pl.when(k == nk - 1)
        def _():
            finish([acc[...] for acc in acc_refs])

    ops = list(a_ops) + list(b_ops) + list(extras) + [(t, ANY) for t in after]
    return pl.pallas_call(
        body, name=name, grid=grid,
        in_specs=[s for _, s in ops],
        out_specs=[s for _, _, s in outs],
        out_shape=[jax.ShapeDtypeStruct(sh, dt) for sh, dt, _ in outs],
        scratch_shapes=[pltpu.VMEM(s, F32) for s in acc_shapes] if nk > 1 else [],
        compiler_params=_params("parallel", "parallel", "arbitrary"),
    )(*[a for a, _ in ops])


def _spec(block, index):
    return pl.BlockSpec(block, index)


def _rmsnorm_fwd(x, g, out_dtype):
    T, D = x.shape
    tm = _pick(T, 256, 8)

    def body(x_ref, g_ref, o_ref):
        xv = x_ref[...]
        r = lax.rsqrt(jnp.mean(xv * xv, axis=-1, keepdims=True) + EPS)
        o_ref[...] = (xv * r * g_ref[...]).astype(o_ref.dtype)

    row = _spec((tm, D), lambda i: (i, 0))
    return pl.pallas_call(
        body, name="rmsnorm_fwd", grid=(T // tm,),
        in_specs=[row, _spec((1, D), lambda i: (0, 0))], out_specs=row,
        out_shape=jax.ShapeDtypeStruct((T, D), out_dtype), compiler_params=_params("parallel"),
    )(x, g)


def _rmsnorm_bwd(x, g, dh, dres):
    T, D = x.shape
    tm = _pick(T, 256, 8)

    def body(x_ref, g_ref, dh_ref, dres_ref, dx_ref, dxb_ref, dg_ref):
        xv = x_ref[...]
        dhv = dh_ref[...].astype(F32)
        r = lax.rsqrt(jnp.mean(xv * xv, axis=-1, keepdims=True) + EPS)
        xh = xv * r
        dxh = dhv * g_ref[...]
        dx = r * (dxh - xh * jnp.mean(dxh * xh, axis=-1, keepdims=True)) + dres_ref[...]
        dx_ref[...] = dx
        dxb_ref[...] = dx.astype(BF16)
        part = jnp.sum(dhv * xh, axis=0, keepdims=True)

        @pl.when(pl.program_id(0) == 0)
        def _():
            dg_ref[...] = part

        @pl.when(pl.program_id(0) > 0)
        def _():
            dg_ref[...] += part

    row = _spec((tm, D), lambda i: (i, 0))
    vec = _spec((1, D), lambda i: (0, 0))
    return pl.pallas_call(
        body, name="rmsnorm_bwd", grid=(T // tm,),
        in_specs=[row, vec, row, row], out_specs=[row, row, vec],
        out_shape=[jax.ShapeDtypeStruct((T, D), F32), jax.ShapeDtypeStruct((T, D), BF16),
                   jax.ShapeDtypeStruct((1, D), F32)],
        compiler_params=_params("arbitrary"),
    )(x, g, dh, dres)


def _loss_head(y, target):
    T, D = y.shape
    tm = _pick(T, 256, 8)

    def body(y_ref, t_ref, dy_ref, dyb_ref, l_ref):
        err = y_ref[...] - t_ref[...]
        dy = err * (1.0 / D)
        dy_ref[...] = dy
        dyb_ref[...] = dy.astype(BF16)
        part = jnp.sum(err * err, axis=0, keepdims=True)

        @pl.when(pl.program_id(0) == 0)
        def _():
            l_ref[...] = part

        @pl.when(pl.program_id(0) > 0)
        def _():
            l_ref[...] += part

    row = _spec((tm, D), lambda i: (i, 0))
    vec = _spec((1, D), lambda i: (0, 0))
    return pl.pallas_call(
        body, name="loss_head", grid=(T // tm,),
        in_specs=[row, row], out_specs=[row, row, vec],
        out_shape=[jax.ShapeDtypeStruct((T, D), F32), jax.ShapeDtypeStruct((T, D), BF16),
                   jax.ShapeDtypeStruct((1, D), F32)],
        compiler_params=_params("arbitrary"),
    )(y, target)


def _headnorm_fwd(qkv, gq, gk):
    _, T, D = qkv.shape
    H = D // HEAD_DIM
    tm = _pick(T, 256, 8)
    scale = HEAD_DIM ** -0.5

    def body(q_ref, k_ref, v_ref, gq_ref, gk_ref, qs_ref, kn_ref, vb_ref):
        for h in range(H):
            sl = slice(h * HEAD_DIM, (h + 1) * HEAD_DIM)
            q = q_ref[:, sl]
            qs_ref[:, sl] = (q * lax.rsqrt(jnp.mean(q * q, axis=-1, keepdims=True) + EPS)
                             * (gq_ref[...] * scale)).astype(BF16)
            k = k_ref[:, sl]
            kn_ref[:, sl] = (k * lax.rsqrt(jnp.mean(k * k, axis=-1, keepdims=True) + EPS)
                             * gk_ref[...]).astype(BF16)
        vb_ref[...] = v_ref[...].astype(BF16)

    part = lambda p: _spec((None, tm, D), lambda i, p=p: (p, i, 0))
    row = _spec((tm, D), lambda i: (i, 0))
    vec = _spec((1, HEAD_DIM), lambda i: (0, 0))
    return pl.pallas_call(
        body, name="headnorm_fwd", grid=(T // tm,),
        in_specs=[part(0), part(1), part(2), vec, vec], out_specs=[row, row, row],
        out_shape=[jax.ShapeDtypeStruct((T, D), BF16)] * 3, compiler_params=_params("parallel"),
    )(qkv, qkv, qkv, gq, gk)


def _headnorm_bwd(qkv, gq, gk, dqs, dkn, dv):
    _, T, D = qkv.shape
    H = D // HEAD_DIM
    tm = _pick(T, 256, 8)
    scale = HEAD_DIM ** -0.5

    def body(q_ref, k_ref, gq_ref, gk_ref, dqs_ref, dkn_ref, dv_ref, dqkv_ref, dgq_ref, dgk_ref):
        dgq = jnp.zeros((1, HEAD_DIM), F32)
        dgk = jnp.zeros((1, HEAD_DIM), F32)
        for h in range(H):
            sl = slice(h * HEAD_DIM, (h + 1) * HEAD_DIM)
            for src, dsrc, g_ref, sc, p in ((q_ref, dqs_ref, gq_ref, scale, 0), (k_ref, dkn_ref, gk_ref, 1.0, 1)):
                v = src[:, sl]
                r = lax.rsqrt(jnp.mean(v * v, axis=-1, keepdims=True) + EPS)
                vh = v * r
                dn = dsrc[:, sl] * sc
                dvh = dn * g_ref[...]
                dqkv_ref[p, :, sl] = (r * (dvh - vh * jnp.mean(dvh * vh, axis=-1, keepdims=True))).astype(BF16)
                dg = jnp.sum(dn * vh, axis=0, keepdims=True)
                if p == 0:
                    dgq = dgq + dg
                else:
                    dgk = dgk + dg
        dqkv_ref[2] = dv_ref[...].astype(BF16)

        @pl.when(pl.program_id(0) == 0)
        def _():
            dgq_ref[...] = dgq
            dgk_ref[...] = dgk

        @pl.when(pl.program_id(0) > 0)
        def _():
            dgq_ref[...] += dgq
            dgk_ref[...] += dgk

    part = lambda p: _spec((None, tm, D), lambda i, p=p: (p, i, 0))
    row = _spec((tm, D), lambda i: (i, 0))
    vec = _spec((1, HEAD_DIM), lambda i: (0, 0))
    return pl.pallas_call(
        body, name="headnorm_bwd", grid=(T // tm,),
        in_specs=[part(0), part(1), vec, vec, row, row, row],
        out_specs=[_spec((3, tm, D), lambda i: (0, i, 0)), vec, vec],
        out_shape=[jax.ShapeDtypeStruct((3, T, D), BF16), jax.ShapeDtypeStruct((1, HEAD_DIM), F32),
                   jax.ShapeDtypeStruct((1, HEAD_DIM), F32)],
        compiler_params=_params("arbitrary"),
    )(qkv, qkv, gq, gk, dqs, dkn, dv)


def _sum_matrix(prefix):
    r = lax.broadcasted_iota(jnp.int32, (2 * KEY_BLOCK, 2 * KEY_BLOCK), 0) & (KEY_BLOCK - 1)
    c = lax.broadcasted_iota(jnp.int32, (2 * KEY_BLOCK, 2 * KEY_BLOCK), 1)
    tri = (r <= c) if prefix else (r > c)
    return jnp.where(tri | (c >= KEY_BLOCK), 1.0, 0.0).astype(BF16)


def _block_sums(v, u):
    hi = v.astype(BF16)
    lo = (v - hi.astype(F32)).astype(BF16)
    s = jnp.dot(jnp.concatenate([hi, lo], axis=1), u, preferred_element_type=F32)
    return s[:, :KEY_BLOCK], s[:, KEY_BLOCK:]


def _log_terms(z, mask):
    ls = jnp.minimum(z, 0.0) - jnp.log(1.0 + jnp.exp(-jnp.abs(z)))
    lk = ls - z
    if mask is not None:
        lk = jnp.where(mask, lk, 0.0)
    return ls, lk


def _causal_mask(row0, key0, tq):
    t = row0 + lax.broadcasted_iota(jnp.int32, (tq, KEY_BLOCK), 0)
    s = key0 + lax.broadcasted_iota(jnp.int32, (tq, KEY_BLOCK), 1)
    return s < t


def _attn_fwd(qs, kn, vb):
    T, D = qs.shape
    H = D // HEAD_DIM
    ATTN_GROUP = ATTN_GROUP_FWD
    tq = _pick(T, ATTN_ROWS, ATTN_GROUP * KEY_BLOCK)
    nd = tq // KEY_BLOCK

    def body(q_ref, k_ref, v_ref, u_ref, o_ref, tot_ref, acc_ref, run_ref):
        i = pl.program_id(1)
        q = q_ref[...]
        uv = u_ref[...]
        acc_ref[...] = jnp.zeros_like(acc_ref)
        run_ref[...] = jnp.zeros_like(run_ref)

        def group(first, masked):
            k_lo = pl.multiple_of((first - ATTN_GROUP + 1) * KEY_BLOCK, KEY_BLOCK)
            slab = pl.ds(k_lo, ATTN_GROUP * KEY_BLOCK)
            z = lax.dot_general(q, k_ref[slab, :], (NT, ((), ())), preferred_element_type=F32)
            parts = []
            for b in reversed(range(ATTN_GROUP)):
                mask = _causal_mask(i * tq, k_lo + b * KEY_BLOCK, tq) if masked else None
                ls, lk = _log_terms(z[:, b * KEY_BLOCK:(b + 1) * KEY_BLOCK], mask)
                after, rows = _block_sums(lk, uv)
                parts.append((b, mask, ls + after, rows))
            run = run_ref[...]
            a_parts = [None] * ATTN_GROUP
            for b, mask, base, rows in parts:
                a = jnp.exp(base + run)
                if masked:
                    a = jnp.where(mask, a, 0.0)
                a_parts[b] = a.astype(BF16)
                run = run + rows
            acc_ref[...] += jnp.dot(jnp.concatenate(a_parts, axis=1), v_ref[slab, :], preferred_element_type=F32)
            run_ref[...] = run

        for d in range(nd // ATTN_GROUP):
            group(i * nd + nd - 1 - d * ATTN_GROUP, True)

        def below(n, c):
            group(i * nd - 1 - n * ATTN_GROUP, False)
            return c

        lax.fori_loop(0, i * (nd // ATTN_GROUP), below, 0)
        o_ref[...] = acc_ref[...].astype(o_ref.dtype)
        tot_ref[...] = run_ref[...]

    blk = _spec((tq, HEAD_DIM), lambda h, i: (i, h))
    col = _spec((T, HEAD_DIM), lambda h, i: (0, h))
    return pl.pallas_call(
        body, name="attn_fwd", grid=(H, T // tq),
        in_specs=[blk, col, col, _spec((2 * KEY_BLOCK, 2 * KEY_BLOCK), lambda h, i: (0, 0))], out_specs=[blk, blk],
        out_shape=[jax.ShapeDtypeStruct((T, D), BF16), jax.ShapeDtypeStruct((T, D), F32)],
        scratch_shapes=[pltpu.VMEM((tq, HEAD_DIM), F32), pltpu.VMEM((tq, KEY_BLOCK), F32)],
        compiler_params=_params("parallel", "arbitrary"),
    )(qs, kn, vb, _sum_matrix(False))


def _attn_bwd(qs, kn, vb, tot, do):
    T, D = qs.shape
    H = D // HEAD_DIM
    ATTN_GROUP = ATTN_GROUP_BWD
    tq = _pick(T, ATTN_ROWS, ATTN_GROUP * KEY_BLOCK)
    nd = tq // KEY_BLOCK

    def body(q_ref, k_ref, v_ref, tot_ref, do_ref, u_ref, dq_ref, dk_ref, dv_ref, run_ref, grun_ref):
        i = pl.program_id(1)

        @pl.when(i == 0)
        def _():
            dk_ref[...] = jnp.zeros_like(dk_ref)
            dv_ref[...] = jnp.zeros_like(dv_ref)

        q = q_ref[...]
        dov = do_ref[...]
        uv = u_ref[...]
        dq_ref[...] = jnp.zeros_like(dq_ref)
        run_ref[...] = jnp.zeros_like(run_ref)
        grun_ref[...] = jnp.zeros_like(grun_ref)

        def group(first, masked):
            k_lo = pl.multiple_of(first * KEY_BLOCK, KEY_BLOCK)
            slab = pl.ds(k_lo, ATTN_GROUP * KEY_BLOCK)
            ks = k_ref[slab, :]
            z = lax.dot_general(q, ks, (NT, ((), ())), preferred_element_type=F32)
            da = lax.dot_general(dov, v_ref[slab, :], (NT, ((), ())), preferred_element_type=F32)
            parts = []
            for b in range(ATTN_GROUP):
                cols = slice(b * KEY_BLOCK, (b + 1) * KEY_BLOCK)
                mask = _causal_mask(i * tq, k_lo + b * KEY_BLOCK, tq) if masked else None
                ls, lk = _log_terms(z[:, cols], mask)
                upto, rows = _block_sums(lk, uv)
                parts.append((mask, ls, tot_ref[...] - upto, rows, da[:, cols]))
            run = run_ref[...]
            grun = grun_ref[...]
            a_parts, dz_parts = [], []
            for mask, ls, right, rows, dab in parts:
                a = jnp.exp(ls + (right - run))
                if masked:
                    a = jnp.where(mask, a, 0.0)
                g = a * dab
                gupto, grows = _block_sums(g, uv)
                dz = g - jnp.exp(ls) * (grun + gupto)
                if masked:
                    dz = jnp.where(mask, dz, 0.0)
                a_parts.append(a.astype(BF16))
                dz_parts.append(dz.astype(BF16))
                run = run + rows
                grun = grun + grows
            dzs = jnp.concatenate(dz_parts, axis=1)
            dq_ref[...] += jnp.dot(dzs, ks, preferred_element_type=F32)
            dk_ref[slab, :] += lax.dot_general(dzs, q, (TN, ((), ())), preferred_element_type=F32)
            dv_ref[slab, :] += lax.dot_general(jnp.concatenate(a_parts, axis=1), dov, (TN, ((), ())),
                                               preferred_element_type=F32)
            run_ref[...] = run
            grun_ref[...] = grun

        def below(n, c):
            group(n * ATTN_GROUP, False)
            return c

        lax.fori_loop(0, i * (nd // ATTN_GROUP), below, 0)
        for d in range(nd // ATTN_GROUP):
            group(i * nd + d * ATTN_GROUP, True)

    blk = _spec((tq, HEAD_DIM), lambda h, i: (i, h))
    col = _spec((T, HEAD_DIM), lambda h, i: (0, h))
    return pl.pallas_call(
        body, name="attn_bwd", grid=(H, T // tq),
        in_specs=[blk, col, col, blk, blk, _spec((2 * KEY_BLOCK, 2 * KEY_BLOCK), lambda h, i: (0, 0))],
        out_specs=[blk, col, col],
        out_shape=[jax.ShapeDtypeStruct((T, D), F32)] * 3,
        scratch_shapes=[pltpu.VMEM((tq, KEY_BLOCK), F32), pltpu.VMEM((tq, KEY_BLOCK), F32)],
        compiler_params=_params("parallel", "arbitrary"),
    )(qs, kn, vb, tot, do, _sum_matrix(True))


def _shift_down(v, n):
    t = lax.broadcasted_iota(jnp.int32, v.shape, 0)
    return jnp.where(t >= n, pltpu.roll(v, n, 0), 0.0)


def _shift_up(v, n):
    rows = v.shape[0]
    t = lax.broadcasted_iota(jnp.int32, v.shape, 0)
    return jnp.where(t < rows - n, pltpu.roll(v, rows - n, 0), 0.0)


def _pool_window(j, cw, D):
    group = (j * cw) // (D // N_POOL_GROUPS)
    return jnp.left_shift(2, group)


def _pool_count(shape, w):
    t = lax.broadcasted_iota(jnp.int32, shape, 0)
    return jnp.minimum(t + 1, w).astype(F32)


def _pool_fwd(h):
    T, D = h.shape
    cw = min(256, D // N_POOL_GROUPS)

    def body(h_ref, p_ref):
        w = _pool_window(pl.program_id(0), cw, D)
        hv = h_ref[...]
        s = hv
        for n in (1, 2, 4, 8):
            s = jnp.where(n < w, s + _shift_down(s, n), s)
        p_ref[...] = (s / _pool_count(hv.shape, w) - hv).astype(p_ref.dtype)

    slab = _spec((T, cw), lambda j: (0, j))
    return pl.pallas_call(
        body, name="pool_fwd", grid=(D // cw,), in_specs=[slab], out_specs=slab,
        out_shape=jax.ShapeDtypeStruct((T, D), BF16), compiler_params=_params("parallel"),
    )(h)


def _pool_bwd(dp):
    T, D = dp.shape
    cw = min(256, D // N_POOL_GROUPS)

    def body(dp_ref, dh_ref):
        w = _pool_window(pl.program_id(0), cw, D)
        dpv = dp_ref[...]
        s = dpv / _pool_count(dpv.shape, w)
        for n in (1, 2, 4, 8):
            s = jnp.where(n < w, s + _shift_up(s, n), s)
        dh_ref[...] = (s - dpv).astype(dh_ref.dtype)

    slab = _spec((T, cw), lambda j: (0, j))
    return pl.pallas_call(
        body, name="pool_bwd", grid=(D // cw,), in_specs=[slab], out_specs=slab,
        out_shape=jax.ShapeDtypeStruct((T, D), F32), compiler_params=_params("parallel"),
    )(dp)


def _pool_scale_bwd(dx, ypre, scale):
    T, D = dx.shape
    tm = _pick(T, 256, 8)

    def body(dx_ref, y_ref, s_ref, dys_ref, ds_ref):
        dxv = dx_ref[...]
        dys_ref[...] = (dxv * s_ref[...]).astype(BF16)
        part = jnp.sum(dxv * y_ref[...].astype(F32), axis=0, keepdims=True)

        @pl.when(pl.program_id(0) == 0)
        def _():
            ds_ref[...] = part

        @pl.when(pl.program_id(0) > 0)
        def _():
            ds_ref[...] += part

    row = _spec((tm, D), lambda i: (i, 0))
    vec = _spec((1, D), lambda i: (0, 0))
    return pl.pallas_call(
        body, name="pool_scale_bwd", grid=(T // tm,), in_specs=[row, row, vec], out_specs=[row, vec],
        out_shape=[jax.ShapeDtypeStruct((T, D), BF16), jax.ShapeDtypeStruct((1, D), F32)],
        compiler_params=_params("arbitrary"),
    )(dx, ypre, scale)


def _conv_specs(T, D, cw, cs):
    part = lambda p: _spec((None, T, cw), lambda j, p=p: (p, 0, j))
    taps = _spec((None, 8, cw), lambda j: (j // (cs // cw), 0, j % (cs // cw)))
    return part, taps


def _conv_fwd(bcx, taps4):
    _, T, D = bcx.shape
    cs = taps4.shape[2]
    cw = min(128, cs)
    part, taps = _conv_specs(T, D, cw, cs)

    def body(b_ref, c_ref, u_ref, w_ref, q_ref):
        g = c_ref[...].astype(F32) * u_ref[...].astype(F32)
        w = w_ref[...]
        y = w[2:3] * g + w[1:2] * _shift_down(g, 1) + w[0:1] * _shift_down(g, 2)
        q_ref[...] = (b_ref[...].astype(F32) * y).astype(q_ref.dtype)

    return pl.pallas_call(
        body, name="conv_fwd", grid=(D // cw,), in_specs=[part(0), part(1), part(2), taps],
        out_specs=_spec((T, cw), lambda j: (0, j)),
        out_shape=jax.ShapeDtypeStruct((T, D), BF16), compiler_params=_params("parallel"),
    )(bcx, bcx, bcx, taps4)


def _conv_bwd(dq, bcx, taps4):
    _, T, D = bcx.shape
    cs = taps4.shape[2]
    cw = min(128, cs)
    part, taps = _conv_specs(T, D, cw, cs)

    def body(dq_ref, b_ref, c_ref, u_ref, w_ref, d_ref, dw_ref):
        b = b_ref[...].astype(F32)
        c = c_ref[...].astype(F32)
        uu = u_ref[...].astype(F32)
        dqv = dq_ref[...].astype(F32)
        w = w_ref[...]
        g = c * uu
        g1 = _shift_down(g, 1)
        g2 = _shift_down(g, 2)
        d_ref[0] = (dqv * (w[2:3] * g + w[1:2] * g1 + w[0:1] * g2)).astype(BF16)
        dy = dqv * b
        dg = w[2:3] * dy + w[1:2] * _shift_up(dy, 1) + w[0:1] * _shift_up(dy, 2)
        d_ref[1] = (dg * uu).astype(BF16)
        d_ref[2] = (dg * c).astype(BF16)
        dw_ref[0:1, :] = jnp.sum(dy * g2, axis=0, keepdims=True)
        dw_ref[1:2, :] = jnp.sum(dy * g1, axis=0, keepdims=True)
        dw_ref[2:3, :] = jnp.sum(dy * g, axis=0, keepdims=True)
        dw_ref[3:8, :] = jnp.zeros((5, cw), F32)

    return pl.pallas_call(
        body, name="conv_bwd", grid=(D // cw,),
        in_specs=[_spec((T, cw), lambda j: (0, j)), part(0), part(1), part(2), taps],
        out_specs=[_spec((3, T, cw), lambda j: (0, 0, j)), taps],
        out_shape=[jax.ShapeDtypeStruct((3, T, D), BF16), jax.ShapeDtypeStruct((N_SHARD, 8, cs), F32)],
        compiler_params=_params("parallel"),
    )(dq, bcx, bcx, bcx, taps4)


def _quarter():
    return 2 * lax.axis_index("x") + lax.axis_index("y")


def _cast_into_slot(w, dtype, row0=0, rows=None):
    R, C = (w.shape[0] if rows is None else rows), w.shape[1]
    tr = _pick(math.gcd(R, row0), 512, 8)
    first = row0 // tr

    def body(w_ref, o_ref):
        o_ref[...] = w_ref[...].astype(o_ref.dtype)

    return pl.pallas_call(
        body, name="cast_into_slot", grid=(R // tr,),
        in_specs=[_spec((tr, C), lambda i: (first + i, 0))],
        out_specs=_spec((None, tr, C), lambda i: (_quarter(), i, 0)),
        out_shape=jax.ShapeDtypeStruct((N_SHARD, R, C), dtype), compiler_params=_params("parallel"),
    )(w)


def _sum_into(recv, own, l, L, prev):
    n, R2, C = recv.shape
    tr = _pick(R2, 256, 8)
    nb = R2 // tr

    def body(r_ref, o_ref, *rest):
        s = o_ref[...].astype(F32)
        for k in range(n):
            s = s + r_ref[k].astype(F32)
        rest[-1][...] = s

    in_specs = [_spec((n, tr, C), lambda i: (0, i, 0)),
                _spec((None, tr, C), lambda i: (_quarter(), lax.axis_index("c") * nb + i, 0))]
    args = [recv, own]
    if prev is not None:
        in_specs.append(ANY)
        args.append(prev)
    return pl.pallas_call(
        body, name="sum_into", grid=(nb,), in_specs=in_specs,
        out_specs=_spec((None, tr, C), lambda i: (l, lax.axis_index("c") * nb + i, 0)),
        out_shape=jax.ShapeDtypeStruct((L, 2 * R2, C), F32),
        input_output_aliases={} if prev is None else {2: 0},
        compiler_params=_params("parallel"),
    )(*args)


def _sum_devices(parts):
    n, R, C = parts.shape
    tr = _pick(R, 256, 8)

    def body(p_ref, o_ref):
        s = p_ref[0].astype(F32)
        for d in range(1, n):
            s = s + p_ref[d].astype(F32)
        o_ref[...] = s

    return pl.pallas_call(
        body, name="sum_devices", grid=(R // tr,),
        in_specs=[_spec((n, tr, C), lambda i: (0, i, 0))], out_specs=_spec((tr, C), lambda i: (i, 0)),
        out_shape=jax.ShapeDtypeStruct((R, C), F32), compiler_params=_params("parallel"),
    )(parts)


def _adamw(w, g, m, v):
    R, C = w.shape
    tr = _pick(R, 256, 8)

    def body(w_ref, g_ref, m_ref, v_ref, d_ref, nm_ref, nv_ref):
        gv = g_ref[...]
        m2 = ADAM_B1 * m_ref[...] + (1.0 - ADAM_B1) * gv
        v2 = ADAM_B2 * v_ref[...] + (1.0 - ADAM_B2) * (gv * gv)
        m_hat = m2 / (1.0 - ADAM_B1 ** ADAM_STEP)
        v_hat = v2 / (1.0 - ADAM_B2 ** ADAM_STEP)
        d_ref[...] = -ADAM_LR * (m_hat / (jnp.sqrt(v_hat) + ADAM_EPS) + ADAM_WD * w_ref[...])
        nm_ref[...] = m2
        nv_ref[...] = v2

    row = _spec((tr, C), lambda i: (i, 0))
    return pl.pallas_call(
        body, name="adamw", grid=(R // tr,), in_specs=[row] * 4, out_specs=[row] * 3,
        out_shape=[jax.ShapeDtypeStruct((R, C), F32)] * 3, compiler_params=_params("parallel"),
    )(w, g, m, v)


def _place():
    return lax.axis_index("x"), lax.axis_index("y"), lax.axis_index("c")


def _half(ref_rows, c):
    return pl.ds(c * (ref_rows // 2), ref_rows // 2)


def _allgather_shards(bufs):
    n = len(bufs)

    def body(*refs):
        outs = refs[n:2 * n]
        send, recv = refs[2 * n:]
        x, y, c = _place()
        chips = [(1 - x, y), (x, 1 - y), (1 - x, 1 - y)]

        def copy(a, k, quarter, core, to):
            part = outs[a].at[quarter, _half(outs[a].shape[1], core)]
            return pltpu.make_async_remote_copy(
                src_ref=part, dst_ref=part, send_sem=send.at[a, k], recv_sem=recv.at[a, k],
                device_id=to, device_id_type=MESH)

        first, passed = [], []
        for a in range(n):
            for k, (cx, cy) in enumerate(chips):
                cp = copy(a, k, 2 * x + y, c, (cx, cy, c))
                cp.start()
                first.append(cp)
        for a in range(n):
            for k, (cx, cy) in enumerate(chips):
                copy(a, k, 2 * cx + cy, c, (x, y, c)).wait_recv()
                cp = copy(a, 3 + k, 2 * cx + cy, c, (x, y, 1 - c))
                cp.start()
                passed.append(cp)
        for a in range(n):
            for k, (cx, cy) in enumerate(chips):
                copy(a, 3 + k, 2 * cx + cy, 1 - c, (x, y, 1 - c)).wait_recv()
        for cp in first + passed:
            cp.wait_send()

    return pl.pallas_call(
        body, name="allgather_shards", in_specs=[ANY] * n, out_specs=[ANY] * n,
        out_shape=[jax.ShapeDtypeStruct(b.shape, b.dtype) for b in bufs],
        input_output_aliases={a: a for a in range(n)},
        scratch_shapes=[pltpu.SemaphoreType.DMA((n, 6)), pltpu.SemaphoreType.DMA((n, 6))],
        compiler_params=pltpu.CompilerParams(has_side_effects=True),
    )(*bufs)


def _flips():
    return [(fx, fy, fc) for fx in (0, 1) for fy in (0, 1) for fc in (0, 1) if (fx, fy, fc) != (0, 0, 0)]


HBM = pl.BlockSpec(memory_space=pltpu.HBM)
SEM = pl.BlockSpec(memory_space=pltpu.SEMAPHORE)
DATAFLOW = pltpu.SideEffectType.DATAFLOW_SIDE_EFFECTING


def _gather_copies(bufs, send, recv):
    x, y, c = _place()
    pairs = []
    for a in range(len(bufs)):
        for k, (cx, cy) in enumerate([(1 - x, y), (x, 1 - y), (1 - x, 1 - y)]):
            def copy(quarter, a=a, k=k, cx=cx, cy=cy):
                part = bufs[a].at[quarter, _half(bufs[a].shape[1], c)]
                return pltpu.make_async_remote_copy(
                    src_ref=part, dst_ref=part, send_sem=send.at[3 * a + k], recv_sem=recv.at[3 * a + k],
                    device_id=(cx, cy, c), device_id_type=MESH)
            pairs.append((copy(2 * x + y), copy(2 * cx + cy)))
    return pairs


def _gather_start(name, bufs, after):
    n = len(bufs)

    def body(*refs):
        for out, _ in _gather_copies(refs[:n], refs[n + 1], refs[n + 2]):
            out.start()
        refs[-1][...] = jnp.zeros_like(refs[-1])

    outs = pl.pallas_call(
        body, name=name, in_specs=[HBM] * n + [ANY],
        out_specs=(SEM, SEM) + (HBM,) * n + (pl.BlockSpec(memory_space=pltpu.VMEM),),
        out_shape=(pltpu.SemaphoreType.DMA((3 * n,)), pltpu.SemaphoreType.DMA((3 * n,)))
        + tuple(pltpu.HBM(b.shape, b.dtype) for b in bufs) + (jax.ShapeDtypeStruct((8, 128), F32),),
        input_output_aliases={a: 2 + a for a in range(n)},
        compiler_params=pltpu.CompilerParams(has_side_effects=DATAFLOW),
    )(*[pltpu.with_memory_space_constraint(b, pltpu.HBM) for b in bufs], after)
    return outs[:-1], outs[-1]


def _gather_wait(name, started, after):
    n = len(started) - 2

    def body(*refs):
        for out, arriving in _gather_copies(refs[:n], refs[n], refs[n + 1]):
            out.wait_send()
            arriving.wait_recv()

    return pl.pallas_call(
        body, name=name, in_specs=[HBM] * n + [SEM, SEM, ANY], out_specs=(HBM,) * n,
        out_shape=tuple(pltpu.HBM(b.shape, b.dtype) for b in started[2:]),
        input_output_aliases={a: a for a in range(n)},
        compiler_params=pltpu.CompilerParams(has_side_effects=DATAFLOW),
    )(*started[2:], started[0], started[1], after)


def _forward_halves(bufs):
    n = len(bufs)

    def body(*refs):
        outs = refs[n:2 * n]
        send, recv = refs[2 * n:]
        x, y, c = _place()

        def copy(a, k, quarter, core):
            part = outs[a].at[quarter, _half(outs[a].shape[1], core)]
            return pltpu.make_async_remote_copy(
                src_ref=part, dst_ref=part, send_sem=send.at[a, k], recv_sem=recv.at[a, k],
                device_id=(x, y, 1 - c), device_id_type=MESH)

        others = [2 * (1 - x) + y, 2 * x + (1 - y), 2 * (1 - x) + (1 - y)]
        for a in range(n):
            for k, quarter in enumerate(others):
                copy(a, k, quarter, c).start()
        for a in range(n):
            for k, quarter in enumerate(others):
                copy(a, k, quarter, 1 - c).wait_recv()
        for a in range(n):
            for k, quarter in enumerate(others):
                copy(a, k, quarter, c).wait_send()

    return pl.pallas_call(
        body, name="forward_halves", in_specs=[ANY] * n, out_specs=[ANY] * n,
        out_shape=[jax.ShapeDtypeStruct(b.shape, b.dtype) for b in bufs],
        input_output_aliases={a: a for a in range(n)},
        scratch_shapes=[pltpu.SemaphoreType.DMA((n, 3)), pltpu.SemaphoreType.DMA((n, 3))],
        compiler_params=pltpu.CompilerParams(has_side_effects=True),
    )(*bufs)


def _exchange_copies(grads, zones, send, recv):
    x, y, c = _place()
    copies = []
    for a in range(len(grads)):
        rows = grads[a].shape[1]
        for k, (fx, fy, fc) in enumerate(_flips()):
            px, py, pc = x ^ fx, y ^ fy, c ^ fc
            j = a * (N_DEV - 1) + k
            copies.append(pltpu.make_async_remote_copy(
                src_ref=grads[a].at[2 * px + py, _half(rows, pc)], dst_ref=zones[a].at[k],
                send_sem=send.at[j], recv_sem=recv.at[j], device_id=(px, py, pc), device_id_type=MESH))
    return copies


def _exchange_start(name, grads):
    n = len(grads)
    zones = [lax.empty((N_DEV - 1, g.shape[1] // 2, g.shape[2]), g.dtype) for g in grads]

    def body(*refs):
        for cp in _exchange_copies(refs[:n], refs[n:2 * n], refs[2 * n], refs[2 * n + 1]):
            cp.start()
        refs[-1][...] = jnp.zeros_like(refs[-1])

    outs = pl.pallas_call(
        body, name=name, in_specs=[HBM] * (2 * n),
        out_specs=(SEM, SEM) + (HBM,) * (2 * n) + (pl.BlockSpec(memory_space=pltpu.VMEM),),
        out_shape=(pltpu.SemaphoreType.DMA((n * (N_DEV - 1),)), pltpu.SemaphoreType.DMA((n * (N_DEV - 1),)))
        + tuple(pltpu.HBM(v.shape, v.dtype) for v in list(grads) + zones) + (jax.ShapeDtypeStruct((8, 128), F32),),
        input_output_aliases={a: 2 + a for a in range(2 * n)},
        compiler_params=pltpu.CompilerParams(has_side_effects=DATAFLOW),
    )(*[pltpu.with_memory_space_constraint(v, pltpu.HBM) for v in list(grads) + zones])
    return outs[:-1], outs[-1]


def _exchange_wait(name, started, after):
    send, recv = started[0], started[1]
    n = (len(started) - 2) // 2

    def body(*refs):
        for cp in _exchange_copies(refs[:n], refs[n:2 * n], refs[2 * n], refs[2 * n + 1]):
            cp.wait_send()
            cp.wait_recv()

    outs = pl.pallas_call(
        body, name=name, in_specs=[HBM] * (2 * n) + [SEM, SEM, ANY], out_specs=(HBM,) * (2 * n),
        out_shape=tuple(pltpu.HBM(v.shape, v.dtype) for v in started[2:]),
        input_output_aliases={a: a for a in range(2 * n)},
        compiler_params=pltpu.CompilerParams(has_side_effects=DATAFLOW),
    )(*started[2:], send, recv, after)
    return outs[:n], outs[n:]


def _share_halves(fulls):
    n = len(fulls)
    index = [(a, l) for a in range(n) for l in range(fulls[a].shape[0])]

    def body(*refs):
        outs = refs[n:2 * n]
        send, recv = refs[2 * n:]
        x, y, c = _place()

        def copy(j, core):
            a, l = index[j]
            part = outs[a].at[l, _half(outs[a].shape[1], core)]
            return pltpu.make_async_remote_copy(
                src_ref=part, dst_ref=part, send_sem=send.at[j], recv_sem=recv.at[j],
                device_id=(x, y, 1 - c), device_id_type=MESH)

        for j in range(len(index)):
            copy(j, c).start()
        for j in range(len(index)):
            copy(j, 1 - c).wait_recv()
        for j in range(len(index)):
            copy(j, c).wait_send()

    return pl.pallas_call(
        body, name="share_halves", in_specs=[ANY] * n, out_specs=[ANY] * n,
        out_shape=[jax.ShapeDtypeStruct(f.shape, f.dtype) for f in fulls],
        input_output_aliases={a: a for a in range(n)},
        scratch_shapes=[pltpu.SemaphoreType.DMA((len(index),)), pltpu.SemaphoreType.DMA((len(index),))],
        compiler_params=pltpu.CompilerParams(has_side_effects=True),
    )(*fulls)


def _allgather_small(v):
    def body(v_ref, o_ref, send, recv, lsem):
        x, y, c = _place()
        me = 4 * x + 2 * y + c
        own = pltpu.make_async_copy(v_ref, o_ref.at[me], lsem)
        own.start()
        sends = []
        for k, (fx, fy, fc) in enumerate(_flips()):
            cp = pltpu.make_async_remote_copy(
                src_ref=v_ref, dst_ref=o_ref.at[me], send_sem=send.at[k], recv_sem=recv.at[k],
                device_id=(x ^ fx, y ^ fy, c ^ fc), device_id_type=MESH)
            cp.start()
            sends.append(cp)
        for k, (fx, fy, fc) in enumerate(_flips()):
            px, py, pc = x ^ fx, y ^ fy, c ^ fc
            pltpu.make_async_remote_copy(
                src_ref=v_ref, dst_ref=o_ref.at[4 * px + 2 * py + pc], send_sem=send.at[k], recv_sem=recv.at[k],
                device_id=(px, py, pc), device_id_type=MESH).wait_recv()
        for cp in sends:
            cp.wait_send()
        own.wait()

    return pl.pallas_call(
        body, name="allgather_small", in_specs=[ANY], out_specs=ANY,
        out_shape=jax.ShapeDtypeStruct((N_DEV,) + v.shape, v.dtype),
        scratch_shapes=[pltpu.SemaphoreType.DMA((7,)), pltpu.SemaphoreType.DMA((7,)), pltpu.SemaphoreType.DMA(())],
        compiler_params=pltpu.CompilerParams(has_side_effects=True),
    )(v)


def _mm_col_fwd(name, a, w4, l, cb, out_dtype, parts, epi_act=None, w4b=None, tm_pref=512, after=()):
    T, K = a.shape
    cs = w4.shape[3]
    N = N_SHARD * cs
    tm = _pick(T, tm_pref, 8)
    nps = cs // cb
    npp = (N // parts) // cb
    a_spec = _spec((tm, K), lambda j, i, k: (i, 0))
    b_spec = _spec((None, None, K, cb), lambda j, i, k: (j // nps, l, 0, j % nps))
    if parts == 1:
        o_spec = _spec((tm, cb), lambda j, i, k: (i, j))
        o_shape = (T, N)
    else:
        o_spec = _spec((None, tm, cb), lambda j, i, k: (j // npp, i, j % npp))
        o_shape = (parts, T, N // parts)
    b_ops = [(w4, b_spec)] if w4b is None else [(w4, b_spec), (w4b, b_spec)]
    dots = [(0, 0, 0)] if w4b is None else [(0, 0, 0), (0, 1, 1)]
    out_dtypes = (out_dtype,) if w4b is None else (F32, F32, out_dtype)
    return _mm(name, (N // cb, T // tm, 1), [(a, a_spec)], b_ops, dots, NN, [(tm, cb)] * len(b_ops),
               [(o_shape, dt, o_spec) for dt in out_dtypes], epi=epi_act, after=after)


def _mm_row_fwd(name, a, w4, l, res):
    T, K = a.shape
    rs, N = w4.shape[2], w4.shape[3]
    tm = _pick(T, 256, 8)
    tn = _pick(N, 1024, 128)
    a_ops = [(a, _spec((tm, rs), lambda j, i, k, s=s: (i, s))) for s in range(N_SHARD)]
    b_ops = [(w4, _spec((None, None, rs, tn), lambda j, i, k, s=s: (s, l, 0, j))) for s in range(N_SHARD)]
    o_spec = _spec((tm, tn), lambda j, i, k: (i, j))
    return _mm(name, (N // tn, T // tm, 1), a_ops, b_ops, [(s, s, 0) for s in range(N_SHARD)], NN, [(tm, tn)],
               [((T, N), F32, o_spec)], epi=lambda accs, ex: [accs[0] + ex[0]], extras=[(res, o_spec)])


def _mm_row_bwd_data(name, dy, w4, l, out_dtype=BF16, epi=None, extras=(), n_out=1, tm_pref=512, after=()):
    T, N = dy.shape
    rs = w4.shape[2]
    K = N_SHARD * rs
    tm = _pick(T, tm_pref, 8)
    a_spec = _spec((tm, N), lambda j, i, k: (i, 0))
    b_spec = _spec((None, None, rs, N), lambda j, i, k: (j, l, 0, 0))
    o_spec = _spec((tm, rs), lambda j, i, k: (i, j))
    return _mm(name, (N_SHARD, T // tm, 1), [(dy, a_spec)], [(w4, b_spec)], [(0, 0, 0)], NT, [(tm, rs)],
               [((T, K), out_dtype, o_spec)] * n_out, epi=epi, extras=[(e, o_spec) for e in extras], after=after)


def _mm_row_bwd_weight(name, a, dy, rs):
    T, K = a.shape
    N = dy.shape[1]
    tk = _pick(T, 512, 8)
    tn = _pick(N, 1024, 128)
    a_spec = _spec((tk, rs), lambda i, j, k: (k, i))
    b_spec = _spec((tk, tn), lambda i, j, k: (k, j))
    o_spec = _spec((None, rs, tn), lambda i, j, k: (i, 0, j))
    return _mm(name, (N_SHARD, N // tn, T // tk), [(a, a_spec)], [(dy, b_spec)], [(0, 0, 0)], TN, [(rs, tn)],
               [((N_SHARD, rs, N), BF16, o_spec)])[0]


def _mm_col_bwd_data(name, dys, w4s, l, cb, parts, tm_pref=512, after=()):
    T = dys[0].shape[-2]
    K, cs = w4s[0].shape[2], w4s[0].shape[3]
    N = N_SHARD * cs
    tm = _pick(T, tm_pref, 8)
    nps = cs // cb
    npp = (N // parts) // cb
    if parts == 1:
        a_spec = _spec((tm, cb), lambda i, j, k: (i, k))
    else:
        a_spec = _spec((None, tm, cb), lambda i, j, k: (k // npp, i, k % npp))
    b_spec = _spec((None, None, K, cb), lambda i, j, k: (k // nps, l, 0, k % nps))
    o_spec = _spec((tm, K), lambda i, j, k: (i, 0))
    return _mm(name, (T // tm, 1, N // cb), [(d, a_spec) for d in dys], [(w, b_spec) for w in w4s],
               [(p, p, 0) for p in range(len(dys))], NT, [(tm, K)], [((T, K), F32, o_spec)], after=after)[0]


def _mm_ffn_bwd_data(name, dys, w4s, l, tm_pref=256, tn_pref=512):
    T = dys[0].shape[0]
    K, cs = w4s[0].shape[2], w4s[0].shape[3]
    tm = _pick(T, tm_pref, 8)
    tn = _pick(K, tn_pref, 128)
    a_ops = [(d, _spec((tm, cs), lambda j, i, k, s=s: (i, s))) for d in dys for s in range(N_SHARD)]
    b_ops = [(w, _spec((None, None, tn, cs), lambda j, i, k, s=s: (s, l, j, 0))) for w in w4s for s in range(N_SHARD)]
    n = len(a_ops)
    return _mm(name, (K // tn, T // tm, 1), a_ops, b_ops, [(p, p, 0) for p in range(n)], NT, [(tm, tn)],
               [((T, K), F32, _spec((tm, tn), lambda j, i, k: (i, j)))])[0]


def _mm_col_bwd_weight(name, a, dys, cs, cb, parts, tm_pref=512, chunks=1):
    T, K = a.shape
    N = N_SHARD * cs
    tk = _pick(T // chunks, 512, 8)
    tm = _pick(K, tm_pref, 128)
    nps = cs // cb
    npp = (N // parts) // cb
    a_ops = [(a, _spec((tk, tm), lambda i, j, k, c=c: (k * chunks + c, i))) for c in range(chunks)]
    if parts == 1:
        b_ops = [(d, _spec((tk, cb), lambda i, j, k, c=c: (k * chunks + c, j))) for d in dys for c in range(chunks)]
    else:
        b_ops = [(d, _spec((None, tk, cb), lambda i, j, k, c=c: (j // npp, k * chunks + c, j % npp)))
                 for d in dys for c in range(chunks)]
    o_spec = _spec((None, tm, cb), lambda i, j, k: (j // nps, i, j % nps))
    nd = len(dys)
    return _mm(name, (K // tm, N // cb, T // (tk * chunks)), a_ops, b_ops,
               [(c, p * chunks + c, p) for p in range(nd) for c in range(chunks)], TN, [(tm, cb)] * nd,
               [((N_SHARD, K, cs), BF16, o_spec)] * nd)


def _swiglu(accs, _):
    g, up = accs
    return [g, up, g * jax.nn.sigmoid(g) * up]


def _swiglu_bwd(accs, ex):
    da = accs[0]
    g = ex[0].astype(F32)
    up = ex[1].astype(F32)
    s = jax.nn.sigmoid(g)
    return [da * up * (s * (1.0 + g * (1.0 - s))), da * (g * s)]


def _ffn_fwd(x1, g_ffn, wg4, wu4, wd4, l):
    h2 = _rmsnorm_fwd(x1, g_ffn, BF16)
    fs = wg4.shape[3]
    gate, up, act = _mm_col_fwd("ffn_up", h2, wg4, l, fs, BF16, 1, epi_act=_swiglu, w4b=wu4, tm_pref=256)
    x2 = _mm_row_fwd("ffn_down", act, wd4, l, x1)[0]
    return x2, (x1, h2, gate, up, act)


def _ffn_bwd(dx2, dx2b, saved, g_ffn, wg4, wu4, wd4, l, after):
    x1, h2, gate, up, act = saved
    fs = wg4.shape[3]
    dgate, dup = _mm_row_bwd_data("ffn_down_bwd_data", dx2b, wd4, l, epi=_swiglu_bwd, extras=(gate, up), n_out=2,
                                  tm_pref=256, after=after)
    g_down = _mm_row_bwd_weight("ffn_down_bwd_weight", act, dx2b, fs)
    dh2 = _mm_ffn_bwd_data("ffn_up_bwd_data", [dgate, dup], [wg4, wu4], l)
    g_gate, g_up = _mm_col_bwd_weight("ffn_up_bwd_weight", h2, [dgate, dup], fs, fs, 1, chunks=4)
    dx1, dx1b, dg = _rmsnorm_bwd(x1, g_ffn, dh2, dx2)
    return dx1, dx1b, dg, g_gate, g_up, g_down


def _sb_fwd(x, g_mix, gq, gk, wqkv4, wo4, l, after=()):
    D = x.shape[1]
    h = _rmsnorm_fwd(x, g_mix, BF16)
    qkv = _mm_col_fwd("qkv_proj", h, wqkv4, l, D // 4, F32, 3, after=after)[0]
    qs, kn, vb = _headnorm_fwd(qkv, gq, gk)
    o, tot = _attn_fwd(qs, kn, vb)
    x1 = _mm_row_fwd("attn_out", o, wo4, l, x)[0]
    return x1, (x, h, qkv, qs, kn, vb, o, tot)


def _sb_bwd(dx1, dx1b, saved, g_mix, gq, gk, wqkv4, wo4, l, after, exchange):
    x, h, qkv, qs, kn, vb, o, tot = saved
    D = x.shape[1]
    do = _mm_row_bwd_data("attn_out_bwd_data", dx1b, wo4, l, after=after)[0]
    g_wo = _mm_row_bwd_weight("attn_out_bwd_weight", o, dx1b, wo4.shape[2])
    dqs, dkn, dv = _attn_bwd(qs, kn, vb, tot, do)
    dqkv, dgq, dgk = _headnorm_bwd(qkv, gq, gk, dqs, dkn, dv)
    g_wqkv = _mm_col_bwd_weight("qkv_bwd_weight", h, [dqkv], wqkv4.shape[3], D // 4, 3, tm_pref=2048)[0]
    token = exchange([g_wqkv, g_wo])
    dh = _mm_col_bwd_data("qkv_bwd_data", [dqkv], [wqkv4], l, D // 4, 3, after=token)
    dx, dxb, dg = _rmsnorm_bwd(x, g_mix, dh, dx1)
    return dx, dxb, dg, dgq, dgk, token


def _pool_mix_fwd(x, g_mix, wp4, scale):
    T, D = x.shape
    C = D // N_POOL_GROUPS
    rq = C // N_SHARD
    h = _rmsnorm_fwd(x, g_mix, F32)
    p = _pool_fwd(h)
    tm = _pick(T, 512, 8)
    a_spec = _spec((tm, rq), lambda g, i, k: (i, g * N_SHARD + k))
    b_spec = _spec((None, None, rq, C), lambda g, i, k: (k, 0, g, 0))
    o_spec = _spec((tm, C), lambda g, i, k: (i, g))
    s_spec = _spec((1, C), lambda g, i, k: (0, g))
    x1, ypre = _mm("pool_mix", (N_POOL_GROUPS, T // tm, N_SHARD), [(p, a_spec)], [(wp4, b_spec)], [(0, 0, 0)], NN,
                   [(tm, C)], [((T, D), F32, o_spec), ((T, D), BF16, o_spec)],
                   epi=lambda accs, ex: [ex[0] + accs[0] * ex[1], accs[0]], extras=[(x, o_spec), (scale, s_spec)])
    return x1, (x, h, p, ypre)


def _pool_mix_bwd(dx1, dx1b, saved, g_mix, wp4, scale, after=()):
    x, h, p, ypre = saved
    T, D = x.shape
    C = D // N_POOL_GROUPS
    rq = C // N_SHARD
    dys, dscale = _pool_scale_bwd(dx1, ypre, scale)
    tm = _pick(T, 512, 8)
    dp = _mm("pool_mix_bwd_data", (N_POOL_GROUPS * N_SHARD, T // tm, 1),
             [(dys, _spec((tm, C), lambda j, i, k: (i, j // N_SHARD)))],
             [(wp4, _spec((None, None, rq, C), lambda j, i, k: (j % N_SHARD, 0, j // N_SHARD, 0)))],
             [(0, 0, 0)], NT, [(tm, rq)], [((T, D), F32, _spec((tm, rq), lambda j, i, k: (i, j)))], after=after)[0]
    tk = _pick(T, 512, 8)
    g_wp = _mm("pool_mix_bwd_weight", (N_POOL_GROUPS * N_SHARD, 1, T // tk),
               [(p, _spec((tk, rq), lambda j, n, k: (k, j)))],
               [(dys, _spec((tk, C), lambda j, n, k: (k, j // N_SHARD)))],
               [(0, 0, 0)], TN, [(rq, C)],
               [((N_SHARD, N_POOL_GROUPS * rq, C), BF16,
                 _spec((None, rq, C), lambda j, n, k: (j % N_SHARD, j // N_SHARD, 0)))])[0]
    dh = _pool_bwd(dp)
    dx, dxb, dg = _rmsnorm_bwd(x, g_mix, dh, dx1)
    return dx, dxb, dg, dscale, g_wp


def _conv_mix_fwd(x, g_mix, win4, taps4, wout4):
    D = x.shape[1]
    h = _rmsnorm_fwd(x, g_mix, BF16)
    bcx = _mm_col_fwd("conv_in", h, win4, 0, D // 4, F32, 3)[0]
    q = _conv_fwd(bcx, taps4)
    x1 = _mm_row_fwd("conv_out", q, wout4, 0, x)[0]
    return x1, (x, h, bcx, q)


def _conv_mix_bwd(dx1, dx1b, saved, g_mix, win4, taps4, wout4, after=()):
    x, h, bcx, q = saved
    D = x.shape[1]
    dq = _mm_row_bwd_data("conv_out_bwd_data", dx1b, wout4, 0, out_dtype=F32, after=after)[0]
    g_wout = _mm_row_bwd_weight("conv_out_bwd_weight", q, dx1b, wout4.shape[2])
    dbcx, dtaps = _conv_bwd(dq, bcx, taps4)
    dh = _mm_col_bwd_data("conv_in_bwd_data", [dbcx], [win4], 0, D // 4, 3)
    g_win = _mm_col_bwd_weight("conv_in_bwd_weight", h, [dbcx], win4.shape[3], D // 4, 3, tm_pref=2048)[0]
    dx, dxb, dg = _rmsnorm_bwd(x, g_mix, dh, dx1)
    return dx, dxb, dg, dtaps, g_wout, g_win


def _rows2d(w):
    return w.reshape(-1, w.shape[-1])


def _pad_rows(v, rows):
    return jnp.pad(v, ((0, rows - v.shape[0]), (0, 0)))


def kernel(x, norm_mix_g, norm_ffn_g, sb_w_qkv, sb_g_q, sb_g_k, sb_w_o, pool_w, pool_scale, conv_w_in, conv_w, conv_w_out, ffn_w_gate, ffn_w_up, ffn_w_down, loss_target, m_norm_mix_g, m_norm_ffn_g, m_sb_w_qkv, m_sb_g_q, m_sb_g_k, m_sb_w_o, m_pool_w, m_pool_scale, m_conv_w_in, m_conv_w, m_conv_w_out, m_ffn_w_gate, m_ffn_w_up, m_ffn_w_down, v_norm_mix_g, v_norm_ffn_g, v_sb_w_qkv, v_sb_g_q, v_sb_g_k, v_sb_w_o, v_pool_w, v_pool_scale, v_conv_w_in, v_conv_w, v_conv_w_out, v_ffn_w_gate, v_ffn_w_up, v_ffn_w_down):
    T, D = x.shape[1], x.shape[2]
    depth = norm_mix_g.shape[0]
    big = dict(sb_w_qkv=(sb_w_qkv, m_sb_w_qkv, v_sb_w_qkv), sb_w_o=(sb_w_o, m_sb_w_o, v_sb_w_o),
               pool_w=(pool_w, m_pool_w, v_pool_w), conv_w_in=(conv_w_in, m_conv_w_in, v_conv_w_in),
               conv_w_out=(conv_w_out, m_conv_w_out, v_conv_w_out), ffn_w_gate=(ffn_w_gate, m_ffn_w_gate, v_ffn_w_gate),
               ffn_w_up=(ffn_w_up, m_ffn_w_up, v_ffn_w_up), ffn_w_down=(ffn_w_down, m_ffn_w_down, v_ffn_w_down))
    names = list(big)

    cs_conv = conv_w.shape[2]
    taps_local = _pad_rows(conv_w[0], 16)
    mixer0_names = ["sb_w_qkv", "sb_w_o"]
    ffn0_names = ["ffn_w_gate", "ffn_w_up", "ffn_w_down"]
    first_names = mixer0_names + ffn0_names
    per_layer = {n: _rows2d(big[n][0]).shape[0] // big[n][0].shape[0] for n in names}

    def layer0(n):
        return _cast_into_slot(_rows2d(big[n][0]), BF16, 0, per_layer[n])

    first = _allgather_shards([layer0(n) for n in mixer0_names])
    gathering0, token0 = _gather_start("gather_start_ffn0", [layer0(n) for n in ffn0_names], first[0])
    rest_bufs = []
    layer1_bufs = [_cast_into_slot(_rows2d(big[n][0]), BF16, per_layer[n], per_layer[n]) for n in ffn0_names]
    layer1_bufs.append(_cast_into_slot(_rows2d(big["pool_w"][0]), BF16))
    gathering1, token1 = _gather_start("gather_start_layer1", layer1_bufs, token0)
    rest_names = [n for n in names if n != "pool_w"]
    done_layers = {n: 2 if n in ffn0_names else 1 if n in mixer0_names else 0 for n in rest_names}
    for n in rest_names:
        skip = done_layers[n] * per_layer[n]
        rest_bufs.append(_cast_into_slot(_rows2d(big[n][0]), BF16, skip, _rows2d(big[n][0]).shape[0] - skip))
    rest_bufs.append(_cast_into_slot(taps_local, F32))
    gathering, token = _gather_start("gather_start", rest_bufs, token1)

    def as_layers(g4, layers):
        return g4.reshape(N_SHARD, layers, g4.shape[1] // layers, g4.shape[2])

    w_first = {n: as_layers(g4, 1) for n, g4 in zip(mixer0_names, first)}
    w_layer1, w_rest, taps4 = {}, {}, None

    def weights(layer, *tensors):
        if layer == 0:
            return [w_first[t] for t in tensors] + [0]
        if layer == 1 and tensors[0] in ffn0_names:
            return [w_layer1[t] for t in tensors] + [0]
        return [w_rest[t] for t in tensors] + [layer - done_layers[tensors[0]]]

    xs = x.reshape(T, D)
    saved = []
    for i in range(depth):
        kind, j = i % 3, i // 3
        g_mix = norm_mix_g[i:i + 1]
        if i == 1:
            got = _forward_halves(_gather_wait("gather_wait_layer1", gathering1, xs))
            for n, g4 in zip(ffn0_names, got[:-1]):
                w_layer1[n] = as_layers(g4, 1)
            w_rest["pool_w"] = as_layers(got[-1], 1)
        if i == 2:
            gathered = _forward_halves(_gather_wait("gather_wait", gathering, xs))
            for n, g4 in zip(rest_names, gathered[:-1]):
                w_rest[n] = as_layers(g4, big[n][0].shape[0] - done_layers[n])
            taps4 = gathered[-1]
        if kind == 0:
            xs, sv = _sb_fwd(xs, g_mix, sb_g_q[j:j + 1], sb_g_k[j:j + 1], *weights(j, "sb_w_qkv", "sb_w_o"),
                             after=(token,) if i == 0 else ())
        elif kind == 1:
            xs, sv = _pool_mix_fwd(xs, g_mix, w_rest["pool_w"], pool_scale[j:j + 1])
        else:
            xs, sv = _conv_mix_fwd(xs, g_mix, w_rest["conv_w_in"], taps4, w_rest["conv_w_out"])
        if i == 0:
            for n, g4 in zip(ffn0_names, _forward_halves(_gather_wait("gather_wait_ffn0", gathering0, xs))):
                w_first[n] = as_layers(g4, 1)
        xs, sf = _ffn_fwd(xs, norm_ffn_g[i:i + 1], *weights(i, "ffn_w_gate", "ffn_w_up", "ffn_w_down"))
        saved.append((sv, sf))

    dxs, dxb, err2 = _loss_head(xs, loss_target.reshape(T, D))
    loss = lax.psum(0.5 * jnp.sum(err2) / D, ("x", "y", "c"))
    started = []
    small = {}
    token = ()

    def exchange(name, grads, layout):
        handle, tok = _exchange_start("exchange_start_" + name, grads)
        started.append((name, handle, layout))
        return (tok,)

    for i in reversed(range(depth)):
        kind, j = i % 3, i // 3
        sv, sf = saved[i]
        dxs, dxb, dg, g_gate, g_up, g_down = _ffn_bwd(dxs, dxb, sf, norm_ffn_g[i:i + 1],
                                                      *weights(i, "ffn_w_gate", "ffn_w_up", "ffn_w_down"), token)
        small[("norm_ffn_g", i)] = dg
        token = exchange(f"ffn_{i}", [g_gate, g_up, g_down], [("ffn_w_gate", i), ("ffn_w_up", i), ("ffn_w_down", i)])
        g_mix = norm_mix_g[i:i + 1]
        if kind == 0:
            dxs, dxb, dg, dgq, dgk, token = _sb_bwd(
                dxs, dxb, sv, g_mix, sb_g_q[j:j + 1], sb_g_k[j:j + 1], *weights(j, "sb_w_qkv", "sb_w_o"), token,
                lambda grads, i=i, j=j: exchange(f"mix_{i}", grads, [("sb_w_qkv", j), ("sb_w_o", j)]))
            small[("sb_g_q", j)], small[("sb_g_k", j)] = dgq, dgk
        elif kind == 1:
            dxs, dxb, dg, dscale, g_wp = _pool_mix_bwd(dxs, dxb, sv, g_mix, w_rest["pool_w"], pool_scale[j:j + 1],
                                                       after=token)
            small[("pool_scale", j)] = dscale
            token = exchange(f"mix_{i}", [g_wp], [("pool_w", j)])
        else:
            dxs, dxb, dg, dtaps, g_wout, g_win = _conv_mix_bwd(dxs, dxb, sv, g_mix, w_rest["conv_w_in"], taps4,
                                                              w_rest["conv_w_out"], after=token)
            small[("conv_w", j)] = dtaps
            token = exchange(f"mix_{i}", [g_win, g_wout], [("conv_w_in", j), ("conv_w_out", j)])
        small[("norm_mix_g", i)] = dg
    grad_x = dxs.reshape(x.shape)

    full = {}
    done = dg
    for name, handle, layout in started:
        grads, zones = _exchange_wait("exchange_wait_" + name, handle, done)
        for g, z, (t, l) in zip(grads, zones, layout):
            full[t] = done = _sum_into(z, g, l, big[t][0].shape[0], full.get(t))
    tensors = sorted(full)
    g_big = dict(zip(tensors, _share_halves([full[t] for t in tensors])))

    n_sb = sb_g_q.shape[0]
    gqk = jnp.concatenate([small[(n, j)] for j in range(n_sb) for n in ("sb_g_q", "sb_g_k")], axis=1)
    dtaps = small[("conv_w", 0)]
    taps_full = jnp.concatenate([dtaps[s, :3] for s in range(N_SHARD)], axis=1)
    pack = jnp.concatenate(
        [small[("norm_mix_g", i)] for i in range(depth)] + [small[("norm_ffn_g", i)] for i in range(depth)]
        + [small[("pool_scale", 0)], jnp.pad(gqk, ((0, 0), (0, D - gqk.shape[1]))), taps_full], axis=0)
    pack = _pad_rows(pack, SMALL_ROWS)
    g_small = _sum_devices(_allgather_small(pack))
    mine = 2 * lax.axis_index("x") + lax.axis_index("y")
    g_taps = lax.dynamic_slice(g_small, (2 * depth + 2, mine * cs_conv), (3, cs_conv))

    def pack_small(norm_mix, norm_ffn, scale, gq, gk, taps):
        qk = jnp.concatenate([v[j:j + 1] for j in range(n_sb) for v in (gq, gk)], axis=1)
        rows = jnp.concatenate([norm_mix, norm_ffn, scale, jnp.pad(qk, ((0, 0), (0, D - qk.shape[1]))),
                                jnp.pad(taps[0], ((0, 0), (0, D - cs_conv)))], axis=0)
        return _pad_rows(rows, SMALL_ROWS)

    g_pack = jnp.concatenate([g_small[:2 * depth + 2], jnp.pad(g_taps, ((0, 0), (0, D - cs_conv))),
                              jnp.zeros((SMALL_ROWS - 2 * depth - 5, D), F32)], axis=0)
    w_pack = pack_small(norm_mix_g, norm_ffn_g, pool_scale, sb_g_q, sb_g_k, conv_w)
    m_pack = pack_small(m_norm_mix_g, m_norm_ffn_g, m_pool_scale, m_sb_g_q, m_sb_g_k, m_conv_w)
    v_pack = pack_small(v_norm_mix_g, v_norm_ffn_g, v_pool_scale, v_sb_g_q, v_sb_g_k, v_conv_w)
    small_out = (g_pack,) + tuple(_adamw(w_pack, g_pack, m_pack, v_pack))

    def unpack_small(p):
        qk = p[2 * depth + 1]
        gq = jnp.stack([qk[(2 * j) * HEAD_DIM:(2 * j + 1) * HEAD_DIM] for j in range(n_sb)])
        gk = jnp.stack([qk[(2 * j + 1) * HEAD_DIM:(2 * j + 2) * HEAD_DIM] for j in range(n_sb)])
        return dict(norm_mix_g=p[:depth], norm_ffn_g=p[depth:2 * depth], pool_scale=p[2 * depth:2 * depth + 1],
                    sb_g_q=gq, sb_g_k=gk, conv_w=p[2 * depth + 2:2 * depth + 5, :cs_conv][None])

    results = [unpack_small(p) for p in small_out]
    for n in names:
        w, m, v = big[n]
        g = _rows2d(g_big[n])
        outs = (g,) + tuple(_adamw(_rows2d(w), g, _rows2d(m), _rows2d(v)))
        for r, o in zip(results, outs):
            r[n] = o.reshape(w.shape)

    order = ["norm_mix_g", "norm_ffn_g", "sb_w_qkv", "sb_g_q", "sb_g_k", "sb_w_o", "pool_w", "pool_scale",
             "conv_w_in", "conv_w", "conv_w_out", "ffn_w_gate", "ffn_w_up", "ffn_w_down"]
    return (loss, grad_x) + tuple(r[n] for r in results for n in order)
```
